```python
import math
import jax, jax.numpy as jnp
from jax import lax
import numpy as np

D_MODEL = 1024
BATCH = 32
SEQ = 256
DEPTH = 2
DEC_BATCH = 8
DEC_SEQ = 1024
PAST_LEN = 256

F32 = jnp.float32
GRID_W = 64
EPS = 1e-6
D_SSM = D_MODEL // 2
SSM_GROUP = 16
N_SSM_GROUPS = D_SSM // SSM_GROUP
N_STATE = 64
N_HEADS = 8
NOPE_DIM = 64
ROPE_DIM = 32
V_DIM = 64
Q_LORA = D_MODEL // 4
KV_LORA = D_MODEL // 8
ROPE_BASE = 10000.0
Q_BLOCK = 128
D_CONV = D_MODEL // 2
CONV_K = 31
N_EXPERTS = 32
TOP_K = 4
D_FF = D_MODEL
SWIGLU_ALPHA = 1.702
SWIGLU_LIMIT = 7.0
MOE_BLOCK = 128
N_BRANCH = 3
IN_W = D_SSM + Q_LORA + KV_LORA + ROPE_DIM + 2 * D_CONV + N_BRANCH * D_MODEL

kernel_name = 'hybrid_s5_mla_conformer_moe_diffusion_step'


def rmsnorm(x, g):
    xf = x.astype(F32)
    y = xf * lax.rsqrt(jnp.mean(xf * xf, axis=-1, keepdims=True) + EPS)
    return (y * g.astype(F32)).astype(x.dtype)


def layernorm(x, g, b):
    xf = x.astype(F32)
    mu = jnp.mean(xf, axis=-1, keepdims=True)
    var = jnp.mean(jnp.square(xf - mu), axis=-1, keepdims=True)
    y = (xf - mu) * lax.rsqrt(var + EPS)
    return (y * g.astype(F32) + b.astype(F32)).astype(x.dtype)


def grid_rope(L):
    rows = L // GRID_W
    row = jnp.repeat(jnp.arange(rows), GRID_W).astype(F32)
    col = jnp.tile(jnp.arange(GRID_W), rows).astype(F32)
    half = ROPE_DIM // 2
    freqs = ROPE_BASE ** (-jnp.arange(0, half, 2, dtype=F32) / half)
    ang_r = row[:, None] * freqs
    ang_c = col[:, None] * freqs
    ang = jnp.concatenate([ang_r, ang_r, ang_c, ang_c], axis=-1)
    return jnp.cos(ang), jnp.sin(ang)


def apply_rope(x, cos, sin):
    xs = x.reshape(x.shape[:-1] + (2, 2, ROPE_DIM // 4))
    rot = jnp.stack([-xs[..., 1, :], xs[..., 0, :]], axis=-2).reshape(x.shape)
    return (x.astype(F32) * cos + rot.astype(F32) * sin).astype(x.dtype)


def attend(q, k, v):
    B, Lq, H, Dh = q.shape
    nb = Lq // Q_BLOCK
    scale = (NOPE_DIM + ROPE_DIM) ** -0.5
    qb = q.reshape(B, nb, Q_BLOCK, H, Dh).transpose(1, 0, 2, 3, 4)

    def one_block(qq):
        s = jnp.einsum('bqhd,bkhd->bhqk', qq, k).astype(F32) * scale
        p = jax.nn.softmax(s, axis=-1).astype(v.dtype)
        return jnp.einsum('bhqk,bkhv->bqhv', p, v)

    o = lax.map(one_block, qb)
    return o.transpose(1, 0, 2, 3, 4).reshape(B, Lq, H * V_DIM)


def s5_discretize(lam_re, lam_im, log_dt, b_re, b_im):
    dt = jnp.exp(log_dt.astype(F32))[:, None]
    lr, li = lam_re.astype(F32), lam_im.astype(F32)
    mag = jnp.exp(lr * dt)
    a_re = mag * jnp.cos(li * dt)
    a_im = mag * jnp.sin(li * dt)
    den = lr * lr + li * li
    f_re = ((a_re - 1.0) * lr + a_im * li) / den
    f_im = (a_im * lr - (a_re - 1.0) * li) / den
    br, bi = b_re.astype(F32), b_im.astype(F32)
    bb_re = f_re[..., None] * br - f_im[..., None] * bi
    bb_im = f_re[..., None] * bi + f_im[..., None] * br
    return a_re, a_im, bb_re, bb_im


def cplx_combine(e1, e2):
    a1r, a1i, b1r, b1i = e1
    a2r, a2i, b2r, b2i = e2
    return (a2r * a1r - a2i * a1i,
            a2r * a1i + a2i * a1r,
            a2r * b1r - a2i * b1i + b2r,
            a2r * b1i + a2i * b1r + b2i)


def s5_scan(u, lam_re, lam_im, log_dt, b_re, b_im, c_re, c_im, h0, reverse):
    a_re, a_im, bb_re, bb_im = s5_discretize(lam_re, lam_im, log_dt, b_re, b_im)
    bu_re = jnp.einsum('gpc,blgc->blgp', bb_re, u)
    bu_im = jnp.einsum('gpc,blgc->blgp', bb_im, u)
    edge = -1 if reverse else 0
    if h0 is not None:
        h_re, h_im = h0
        bu_re = bu_re.at[:, edge].add(a_re * h_re - a_im * h_im)
        bu_im = bu_im.at[:, edge].add(a_re * h_im + a_im * h_re)
    ar = jnp.broadcast_to(a_re, bu_re.shape)
    ai = jnp.broadcast_to(a_im, bu_re.shape)
    _, _, x_re, x_im = lax.associative_scan(cplx_combine, (ar, ai, bu_re, bu_im), reverse=reverse, axis=1)
    y = (jnp.einsum('gcp,blgp->blgc', c_re.astype(F32), x_re)
         - jnp.einsum('gcp,blgp->blgc', c_im.astype(F32), x_im))
    return y, x_re[:, edge], x_im[:, edge]


def depthwise_conv(x, w, b):
    out = lax.conv_general_dilated(
        x, w[:, None, :].astype(x.dtype), window_strides=(1,),
        padding=[(CONV_K // 2, CONV_K // 2)],
        dimension_numbers=('NWC', 'WIO', 'NWC'),
        feature_group_count=x.shape[-1])
    return out + b


def clamped_swiglu(gu):
    g, u = gu[..., :D_FF], gu[..., D_FF:]
    g = jnp.minimum(g, SWIGLU_LIMIT)
    u = jnp.clip(u, -SWIGLU_LIMIT, SWIGLU_LIMIT)
    return (u + 1.0) * (g * jax.nn.sigmoid(SWIGLU_ALPHA * g))


def moe(h, router_w, router_b, w1, b1, w2, b2):
    B, L, D = h.shape
    T = B * L
    x = h.reshape(T, D)
    logits = (x @ router_w + router_b).astype(F32)
    top_val, top_idx = lax.top_k(logits, TOP_K)
    wts = jax.nn.softmax(top_val, axis=-1).astype(x.dtype)
    n = T * TOP_K
    flat_e = top_idx.reshape(n)
    flat_tok = jnp.repeat(jnp.arange(T, dtype=jnp.int32), TOP_K)
    flat_w = wts.reshape(n)
    order = jnp.argsort(flat_e)
    se, stok, sw = flat_e[order], flat_tok[order], flat_w[order]
    counts = jnp.zeros((N_EXPERTS,), jnp.int32).at[flat_e].add(1)
    padded = (counts + MOE_BLOCK - 1) // MOE_BLOCK * MOE_BLOCK
    pad_end = jnp.cumsum(padded)
    pad_start = pad_end - padded
    start = jnp.cumsum(counts) - counts
    dest = pad_start[se] + jnp.arange(n, dtype=jnp.int32) - start[se]
    nb = -(-n // MOE_BLOCK) + N_EXPERTS
    rows_tok = jnp.zeros((nb * MOE_BLOCK,), jnp.int32).at[dest].set(stok)
    rows_w = jnp.zeros((nb * MOE_BLOCK,), x.dtype).at[dest].set(sw)
    block_e = jnp.minimum(
        jnp.searchsorted(pad_end, jnp.arange(nb, dtype=jnp.int32) * MOE_BLOCK, side='right'),
        N_EXPERTS - 1)

    def expert_block(args):
        tok, w, e = args
        xb = x[tok]
        gu = xb @ w1[e] + b1[e]
        y = clamped_swiglu(gu) @ w2[e] + b2[e]
        return y * w[:, None]

    ys = lax.map(expert_block, (rows_tok.reshape(nb, MOE_BLOCK), rows_w.reshape(nb, MOE_BLOCK), block_e))
    out = jnp.zeros_like(x).at[rows_tok].add(ys.reshape(nb * MOE_BLOCK, D))
    return out.reshape(B, L, D)


def mixer(h, lp, ctx_state):
    B, L, _ = h.shape
    proj = h @ lp['w_in']
    cuts = [int(v) for v in np.cumsum([D_SSM, Q_LORA, KV_LORA, ROPE_DIM, 2 * D_CONV, D_MODEL, D_MODEL])]
    u, cq, ckv, kr, cin, g_a, g_b, g_c = jnp.split(proj, cuts, axis=-1)

    uf = u.astype(F32).reshape(B, L, N_SSM_GROUPS, SSM_GROUP)
    y_ssm = lp['ssm_d'].astype(F32).reshape(N_SSM_GROUPS, SSM_GROUP) * uf
    finals = []
    for d in range(2):
        if ctx_state is None:
            h0 = None
        else:
            st = ctx_state[2].astype(F32)
            h0 = (st[:, d, ..., 0], st[:, d, ..., 1])
        y_d, f_re, f_im = s5_scan(uf, lp['ssm_lam_re'][d], lp['ssm_lam_im'][d], lp['ssm_log_dt'][d],
                                  lp['ssm_b_re'][d], lp['ssm_b_im'][d], lp['ssm_c_re'][d], lp['ssm_c_im'][d],
                                  h0, reverse=(d == 1))
        y_ssm = y_ssm + y_d
        finals.append(jnp.stack([f_re, f_im], axis=-1))
    y_ssm = jax.nn.gelu(y_ssm).reshape(B, L, D_SSM).astype(h.dtype)
    glu = y_ssm @ lp['w_ssm_glu'] + lp['b_ssm_glu']
    o_ssm = glu[..., :D_MODEL] * jax.nn.sigmoid(glu[..., D_MODEL:])

    c_kv = rmsnorm(ckv, lp['kv_norm_g'])
    q = (rmsnorm(cq, lp['q_norm_g']) @ lp['w_uq']).reshape(B, L, N_HEADS, NOPE_DIM + ROPE_DIM)
    q_nope, q_rope = q[..., :NOPE_DIM], q[..., NOPE_DIM:]
    if ctx_state is None:
        ckv_all, kr_all = c_kv, kr
    else:
        cos, sin = grid_rope(L)
        q_rope = apply_rope(q_rope, cos[:, None, :], sin[:, None, :])
        ckv_all = jnp.concatenate([ctx_state[0].astype(h.dtype), c_kv], axis=1)
        kr_all = jnp.concatenate([ctx_state[1].astype(h.dtype), apply_rope(kr, cos, sin)], axis=1)
    Lk = ckv_all.shape[1]
    kv = (ckv_all @ lp['w_ukv']).reshape(B, Lk, N_HEADS, NOPE_DIM + V_DIM)
    k = jnp.concatenate([kv[..., :NOPE_DIM],
                         jnp.broadcast_to(kr_all[:, :, None, :], (B, Lk, N_HEADS, ROPE_DIM))], axis=-1)
    v = kv[..., NOPE_DIM:]
    o_mla = attend(jnp.concatenate([q_nope, q_rope], axis=-1), k, v) @ lp['w_o_mla']

    cg = cin[..., :D_CONV] * jax.nn.sigmoid(cin[..., D_CONV:])
    cg = depthwise_conv(cg, lp['conv_dw_w'], lp['conv_dw_b'])
    cg = jax.nn.silu(layernorm(cg, lp['conv_ln_g'], lp['conv_ln_b']))
    o_conv = cg @ lp['w_conv_pw2'] + lp['b_conv_pw2']

    merged = jax.nn.sigmoid(g_a) * o_ssm + jax.nn.sigmoid(g_b) * o_mla + jax.nn.sigmoid(g_c) * o_conv
    out = merged @ lp['w_out']
    if ctx_state is None:
        return out, (c_kv, kr, jnp.stack(finals, axis=1).astype(h.dtype))
    return out, None


def layer(x, mod, lp, ctx_state):
    sh1, sc1, g1, sh2, sc2, g2 = jnp.split(mod, 6, axis=-1)
    h = rmsnorm(x, lp['norm_mix_g']) * (1.0 + sc1) + sh1
    o, st = mixer(h, lp, ctx_state)
    x = x + g1 * o
    h = rmsnorm(x, lp['norm_ffn_g']) * (1.0 + sc2) + sh2
    x = x + g2 * moe(h, lp['router_w'], lp['router_b'], lp['moe_w1'], lp['moe_b1'], lp['moe_w2'], lp['moe_b2'])
    return x, st


def setup_inputs(seed: int = 0) -> dict:
    key = jax.random.key(seed)
    ks = jax.random.split(key, 48)

    def nrm(k, shape, s):
        return s * jax.random.normal(k, shape, F32)

    L, G, P = DEPTH, N_SSM_GROUPS, N_STATE
    gain = lambda k, shape: 1.0 + nrm(k, shape, 0.01)
    return {
        'x_prompt': nrm(ks[0], (BATCH, SEQ, D_MODEL), 1.0),
        'x_sample': nrm(ks[1], (DEC_BATCH, DEC_SEQ, D_MODEL), 1.0),
        'c': nrm(ks[2], (DEC_BATCH, D_MODEL), 1.0),
        'cache_kv_latent': nrm(ks[3], (DEC_BATCH, DEPTH, PAST_LEN, KV_LORA), 1.0),
        'cache_k_rope': nrm(ks[4], (DEC_BATCH, DEPTH, PAST_LEN, ROPE_DIM), 1.0),
        'state_ssm': nrm(ks[5], (DEC_BATCH, DEPTH, 2, G, P, 2), 0.5),
        'c_ctx': nrm(ks[6], (D_MODEL,), 1.0),
        'norm_mix_g': gain(ks[7], (L, D_MODEL)),
        'norm_ffn_g': gain(ks[8], (L, D_MODEL)),
        'final_norm_g': gain(ks[9], (D_MODEL,)),
        'w_ada': nrm(ks[10], (L, D_MODEL, 6 * D_MODEL), 0.3 * D_MODEL ** -0.5),
        'b_ada': nrm(ks[11], (L, 6 * D_MODEL), 0.02),
        'w_in': nrm(ks[12], (L, D_MODEL, IN_W), D_MODEL ** -0.5),
        'q_norm_g': gain(ks[13], (L, Q_LORA)),
        'w_uq': nrm(ks[14], (L, Q_LORA, N_HEADS * (NOPE_DIM + ROPE_DIM)), Q_LORA ** -0.5),
        'kv_norm_g': gain(ks[15], (L, KV_LORA)),
        'w_ukv': nrm(ks[16], (L, KV_LORA, N_HEADS * (NOPE_DIM + V_DIM)), KV_LORA ** -0.5),
        'w_o_mla': nrm(ks[17], (L, N_HEADS * V_DIM, D_MODEL), (N_HEADS * V_DIM) ** -0.5),
        'ssm_lam_re': -0.5 + nrm(ks[18], (L, 2, G, P), 0.01),
        'ssm_lam_im': math.pi * jnp.arange(P, dtype=F32) + nrm(ks[19], (L, 2, G, P), 0.01),
        'ssm_log_dt': jax.random.uniform(ks[20], (L, 2, G), F32, math.log(1e-3), math.log(1e-1)),
        'ssm_b_re': nrm(ks[21], (L, 2, G, P, SSM_GROUP), (2 * SSM_GROUP) ** -0.5),
        'ssm_b_im': nrm(ks[22], (L, 2, G, P, SSM_GROUP), (2 * SSM_GROUP) ** -0.5),
        'ssm_c_re': nrm(ks[23], (L, 2, G, SSM_GROUP, P), (2 * P) ** -0.5),
        'ssm_c_im': nrm(ks[24], (L, 2, G, SSM_GROUP, P), (2 * P) ** -0.5),
        'ssm_d': nrm(ks[25], (L, D_SSM), 1.0),
        'w_ssm_glu': nrm(ks[26], (L, D_SSM, 2 * D_MODEL), D_SSM ** -0.5),
        'b_ssm_glu': nrm(ks[27], (L, 2 * D_MODEL), 0.02),
        'conv_dw_w': nrm(ks[28], (L, CONV_K, D_CONV), CONV_K ** -0.5),
        'conv_dw_b': nrm(ks[29], (L, D_CONV), 0.02),
        'conv_ln_g': gain(ks[30], (L, D_CONV)),
        'conv_ln_b': nrm(ks[31], (L, D_CONV), 0.02),
        'w_conv_pw2': nrm(ks[32], (L, D_CONV, D_MODEL), D_CONV ** -0.5),
        'b_conv_pw2': nrm(ks[33], (L, D_MODEL), 0.02),
        'w_out': nrm(ks[34], (L, D_MODEL, D_MODEL), D_MODEL ** -0.5),
        'router_w': nrm(ks[35], (L, D_MODEL, N_EXPERTS), D_MODEL ** -0.5),
        'router_b': nrm(ks[36], (L, N_EXPERTS), 0.01),
        'moe_w1': nrm(ks[37], (L, N_EXPERTS, D_MODEL, 2 * D_FF), D_MODEL ** -0.5),
        'moe_b1': nrm(ks[38], (L, N_EXPERTS, 2 * D_FF), 0.01),
        'moe_w2': nrm(ks[39], (L, N_EXPERTS, D_FF, D_MODEL), D_FF ** -0.5),
        'moe_b2': nrm(ks[40], (L, N_EXPERTS, D_MODEL), 0.01),
    }


def reference(x_prompt, x_sample, c, cache_kv_latent, cache_k_rope, state_ssm, c_ctx,
              norm_mix_g, norm_ffn_g, final_norm_g, w_ada, b_ada, w_in, q_norm_g, w_uq, kv_norm_g,
              w_ukv, w_o_mla, ssm_lam_re, ssm_lam_im, ssm_log_dt, ssm_b_re, ssm_b_im, ssm_c_re,
              ssm_c_im, ssm_d, w_ssm_glu, b_ssm_glu, conv_dw_w, conv_dw_b, conv_ln_g, conv_ln_b,
              w_conv_pw2, b_conv_pw2, w_out, router_w, router_b, moe_w1, moe_b1, moe_w2, moe_b2):
    xp, xs = x_prompt, x_sample
    kv_list, kr_list, ssm_list = [], [], []
    for l in range(DEPTH):
        lp = dict(norm_mix_g=norm_mix_g[l], norm_ffn_g=norm_ffn_g[l], w_in=w_in[l],
                  q_norm_g=q_norm_g[l], w_uq=w_uq[l], kv_norm_g=kv_norm_g[l], w_ukv=w_ukv[l],
                  w_o_mla=w_o_mla[l], ssm_lam_re=ssm_lam_re[l], ssm_lam_im=ssm_lam_im[l],
                  ssm_log_dt=ssm_log_dt[l], ssm_b_re=ssm_b_re[l], ssm_b_im=ssm_b_im[l],
                  ssm_c_re=ssm_c_re[l], ssm_c_im=ssm_c_im[l], ssm_d=ssm_d[l],
                  w_ssm_glu=w_ssm_glu[l], b_ssm_glu=b_ssm_glu[l], conv_dw_w=conv_dw_w[l],
                  conv_dw_b=conv_dw_b[l], conv_ln_g=conv_ln_g[l], conv_ln_b=conv_ln_b[l],
                  w_conv_pw2=w_conv_pw2[l], b_conv_pw2=b_conv_pw2[l], w_out=w_out[l],
                  router_w=router_w[l], router_b=router_b[l], moe_w1=moe_w1[l], moe_b1=moe_b1[l],
                  moe_w2=moe_w2[l], moe_b2=moe_b2[l])
        mod_ctx = (jax.nn.silu(c_ctx) @ w_ada[l] + b_ada[l])[None, None, :]
        mod_lat = (jax.nn.silu(c) @ w_ada[l] + b_ada[l])[:, None, :]
        xp, (kv_l, kr_l, ssm_l) = layer(xp, mod_ctx, lp, None)
        xs, _ = layer(xs, mod_lat, lp, (cache_kv_latent[:, l], cache_k_rope[:, l], state_ssm[:, l]))
        kv_list.append(kv_l)
        kr_list.append(kr_l)
        ssm_list.append(ssm_l)
    y_prompt = rmsnorm(xp, final_norm_g)
    y_sample = rmsnorm(xs, final_norm_g)
    new_kv_latent = jnp.stack(kv_list, axis=1)
    new_k_rope = jnp.stack(kr_list, axis=1)
    new_state_ssm = jnp.stack(ssm_list, axis=1)
    return (y_prompt, y_sample, new_kv_latent, new_k_rope, new_state_ssm)
```

```python
import functools
import math

import numpy as np
import jax
import jax.numpy as jnp
from jax import lax
from jax.experimental import pallas as pl
from jax.experimental.pallas import tpu as pltpu

F32 = jnp.float32
BF16 = jnp.bfloat16

D_MODEL = 1024
BATCH = 32
SEQ = 256
DEPTH = 2
DEC_BATCH = 8
DEC_SEQ = 1024
PAST_LEN = 256
GRID_W = 64
EPS = 1e-6
D_SSM = 512
SSM_GROUP = 16
N_SSM_GROUPS = 32
N_STATE = 64
N_HEADS = 8
NOPE_DIM = 64
ROPE_DIM = 32
V_DIM = 64
Q_LORA = 256
KV_LORA = 128
ROPE_BASE = 10000.0
D_CONV = 512
CONV_K = 31
N_EXPERTS = 32
TOP_K = 4
D_FF = 1024
SWIGLU_ALPHA = 1.702
SWIGLU_LIMIT = 7.0

T_P = BATCH * SEQ
T_S = DEC_BATCH * DEC_SEQ
T = T_P + T_S
TB = 256
NTB = T // TB
NTB_P = T_P // TB
TB_PER_S = DEC_SEQ // TB
SSM_STATE = N_SSM_GROUPS * N_STATE
PROJ_W = 5120
MOE_BM = 256
MOE_NB = (T * TOP_K) // MOE_BM + N_EXPERTS
MOE_ROWS = MOE_NB * MOE_BM
VMEM_LIMIT = 56 * 1024 * 1024


def _cp(sem, vmem=None):
    return pltpu.CompilerParams(dimension_semantics=sem, vmem_limit_bytes=vmem)


def _mod_row(i):
    return jnp.where(i < NTB_P, DEC_BATCH, (i - NTB_P) // TB_PER_S)


def _rope_blk(i):
    return jnp.where(i < NTB_P, TB_PER_S, (i - NTB_P) % TB_PER_S)


def _ada_kernel(c_ref, w_ref, b_ref, o_ref):
    c = c_ref[...]
    s = c * jax.nn.sigmoid(c)
    o_ref[...] = jnp.dot(s, w_ref[...], preferred_element_type=F32,
                         precision=lax.Precision.HIGHEST) + b_ref[...]


def _ada(cc, w_ada, b_ada3, l):
    tn = 1024
    return pl.pallas_call(
        _ada_kernel,
        grid=(6 * D_MODEL // tn,),
        in_specs=[pl.BlockSpec((16, D_MODEL), lambda j: (0, 0)),
                  pl.BlockSpec((None, D_MODEL, tn), lambda j: (l, 0, j)),
                  pl.BlockSpec((None, 1, tn), lambda j: (l, 0, j))],
        out_specs=pl.BlockSpec((16, tn), lambda j: (0, j)),
        out_shape=jax.ShapeDtypeStruct((16, 6 * D_MODEL), F32),
        compiler_params=_cp(("arbitrary",)),
        name="ada",
    )(cc, w_ada, b_ada3)


def _norm1_kernel(x_ref, g_ref, mod_ref, h_ref):
    x = x_ref[...]
    y = x * lax.rsqrt(jnp.mean(x * x, axis=-1, keepdims=True) + EPS) * g_ref[...]
    h_ref[...] = (y * (1.0 + mod_ref[1:2, :]) + mod_ref[0:1, :]).astype(h_ref.dtype)


def _norm1(x, g3, modt, l):
    return pl.pallas_call(
        _norm1_kernel,
        grid=(NTB,),
        in_specs=[pl.BlockSpec((TB, D_MODEL), lambda i: (i, 0)),
                  pl.BlockSpec((None, 1, D_MODEL), lambda i: (l, 0, 0)),
                  pl.BlockSpec((None, 6, D_MODEL), lambda i: (_mod_row(i), 0, 0))],
        out_specs=pl.BlockSpec((TB, D_MODEL), lambda i: (i, 0)),
        out_shape=jax.ShapeDtypeStruct((T, D_MODEL), BF16),
        compiler_params=_cp(("arbitrary",)),
        name="norm1",
    )(x, g3, modt)


def _mm_kernel(a_ref, w_ref, o_ref, wb_ref):
    @pl.when(pl.program_id(1) == 0)
    def _():
        wb_ref[...] = w_ref[...].astype(BF16)

    o_ref[...] = jnp.dot(a_ref[...], wb_ref[...], preferred_element_type=F32).astype(o_ref.dtype)


def _mm(a, w, l, tm, tn, out_dtype=F32, name="mm"):
    m, k = a.shape
    n = w.shape[-1]
    return pl.pallas_call(
        _mm_kernel,
        grid=(n // tn, m // tm),
        in_specs=[pl.BlockSpec((tm, k), lambda j, i: (i, 0)),
                  pl.BlockSpec((None, k, tn), lambda j, i: (l, 0, j))],
        out_specs=pl.BlockSpec((tm, tn), lambda j, i: (i, j)),
        out_shape=jax.ShapeDtypeStruct((m, n), out_dtype),
        scratch_shapes=[pltpu.VMEM((k, tn), BF16)],
        compiler_params=_cp(("arbitrary", "arbitrary"), VMEM_LIMIT),
        name=name,
    )(a, w)


def _s5_disc_kernel(lr_ref, li_ref, ldt_ref, ar_ref, ai_ref, fr_ref, fi_ref):
    dt = jnp.exp(ldt_ref[...])
    lr = lr_ref[...]
    li = li_ref[...]
    mag = jnp.exp(lr * dt)
    a_re = mag * jnp.cos(li * dt)
    a_im = mag * jnp.sin(li * dt)
    den = lr * lr + li * li
    ar_ref[...] = a_re
    ai_ref[...] = a_im
    fr_ref[...] = ((a_re - 1.0) * lr + a_im * li) / den
    fi_ref[...] = (a_im * lr - (a_re - 1.0) * li) / den


def _s5_disc(lam_re, lam_im, log_dt):
    rows = DEPTH * 2 * N_SSM_GROUPS
    shp = jax.ShapeDtypeStruct((rows, N_STATE), F32)
    return pl.pallas_call(
        _s5_disc_kernel,
        out_shape=(shp, shp, shp, shp),
        name="s5_disc",
    )(lam_re.reshape(rows, N_STATE), lam_im.reshape(rows, N_STATE), log_dt.reshape(rows, 1))


S5_CHUNK = 256
S5_SUB = 32
S5_NSUB = S5_CHUNK // S5_SUB
S5_NCHUNK = 8


def _s5_kernel(uf_ref, ur_ref, wb_ref, wc_ref, are_ref, aim_ref, h0r_ref, h0i_ref,
               yf_ref, yr_ref, fin_ref,
               hre_ref, him_ref, ab_re, ab_im, xre_ref, xim_ref):
    k = pl.program_id(0)

    @pl.when(k < 4)
    def _():
        hre_ref[...] = jnp.zeros_like(hre_ref)
        him_ref[...] = jnp.zeros_like(him_ref)

    @pl.when(k == 4)
    def _():
        hre_ref[...] = h0r_ref[...]
        him_ref[...] = h0i_ref[...]

    for d, (u_ref, y_ref) in enumerate(((uf_ref, yf_ref), (ur_ref, yr_ref))):
        reverse = d == 1
        ab_re[...] = jnp.broadcast_to(are_ref[d], (8, SSM_STATE))
        ab_im[...] = jnp.broadcast_to(aim_ref[d], (8, SSM_STATE))

        def sub(s, carry, d=d, reverse=reverse, u_ref=u_ref, y_ref=y_ref):
            ss = (S5_NSUB - 1 - s) if reverse else s
            rows = pl.ds(pl.multiple_of(ss * S5_SUB, S5_SUB), S5_SUB)
            u2 = u_ref[rows].reshape(S5_SUB * 8, D_SSM).astype(BF16)
            for half in range(2):
                uh = u2[:, half * 256:(half + 1) * 256]
                cols = slice(half * 1024, (half + 1) * 1024)
                xre_ref[:, cols] = jnp.dot(uh, wb_ref[d, 0, half], preferred_element_type=F32)
                xim_ref[:, cols] = jnp.dot(uh, wb_ref[d, 1, half], preferred_element_type=F32)

            def step(t, h, reverse=reverse):
                tt = (S5_SUB - 1 - t) if reverse else t
                r = pl.ds(pl.multiple_of(tt * 8, 8), 8)
                h_re, h_im = h
                a_re = ab_re[...]
                a_im = ab_im[...]
                n_re = a_re * h_re - a_im * h_im + xre_ref[r, :]
                n_im = a_re * h_im + a_im * h_re + xim_ref[r, :]
                xre_ref[r, :] = n_re
                xim_ref[r, :] = n_im
                return n_re, n_im

            h_re, h_im = lax.fori_loop(0, S5_SUB, step, (hre_ref[d], him_ref[d]))
            hre_ref[d] = h_re
            him_ref[d] = h_im

            @pl.when((s == 0) & (k < 4))
            def _(d=d, reverse=reverse):
                r0 = (S5_SUB - 1) * 8 if reverse else 0
                fin_ref[d, 0] = xre_ref[r0:r0 + 8, :]
                fin_ref[d, 1] = xim_ref[r0:r0 + 8, :]

            x_re = xre_ref[...].astype(BF16)
            x_im = xim_ref[...].astype(BF16)
            for q in range(4):
                ks = slice(q * 512, (q + 1) * 512)
                yq = (jnp.dot(x_re[:, ks], wc_ref[d, 0, q], preferred_element_type=F32)
                      + jnp.dot(x_im[:, ks], wc_ref[d, 1, q], preferred_element_type=F32))
                y_ref[rows, :, q * 128:(q + 1) * 128] = yq.reshape(S5_SUB, 8, 128)
            return carry

        lax.fori_loop(0, S5_NSUB, sub, 0)


def _s5(u_all, wb, wc, a_re, a_im, h0_re, h0_im, l):
    def rev_chunk(k):
        return jnp.where(k < 4, k, 11 - k)

    blk = (None, S5_CHUNK, 8, D_SSM)
    y_shape = jax.ShapeDtypeStruct((S5_NCHUNK, S5_CHUNK, 8, D_SSM), F32)
    return pl.pallas_call(
        _s5_kernel,
        grid=(S5_NCHUNK,),
        in_specs=[pl.BlockSpec(blk, lambda k: (k, 0, 0, 0)),
                  pl.BlockSpec(blk, lambda k: (rev_chunk(k), 0, 0, 0)),
                  pl.BlockSpec((None, 2, 2, 2, 256, 1024), lambda k: (l, 0, 0, 0, 0, 0)),
                  pl.BlockSpec((None, 2, 2, 4, 512, 128), lambda k: (l, 0, 0, 0, 0, 0)),
                  pl.BlockSpec((None, 2, 1, SSM_STATE), lambda k: (l, 0, 0, 0)),
                  pl.BlockSpec((None, 2, 1, SSM_STATE), lambda k: (l, 0, 0, 0)),
                  pl.BlockSpec((2, 8, SSM_STATE), lambda k: (0, 0, 0)),
                  pl.BlockSpec((2, 8, SSM_STATE), lambda k: (0, 0, 0))],
        out_specs=[pl.BlockSpec(blk, lambda k: (k, 0, 0, 0)),
                   pl.BlockSpec(blk, lambda k: (rev_chunk(k), 0, 0, 0)),
                   pl.BlockSpec((None, 2, 2, 8, SSM_STATE), lambda k: (jnp.minimum(k, 3), 0, 0, 0, 0))],
        out_shape=(y_shape, y_shape,
                   jax.ShapeDtypeStruct((4, 2, 2, 8, SSM_STATE), F32)),
        scratch_shapes=[pltpu.VMEM((2, 8, SSM_STATE), F32), pltpu.VMEM((2, 8, SSM_STATE), F32),
                        pltpu.VMEM((8, SSM_STATE), F32), pltpu.VMEM((8, SSM_STATE), F32),
                        pltpu.VMEM((S5_SUB * 8, SSM_STATE), F32), pltpu.VMEM((S5_SUB * 8, SSM_STATE), F32)],
        compiler_params=_cp(("arbitrary",), VMEM_LIMIT),
        name="s5_scan",
    )(u_all, u_all, wb, wc, a_re, a_im, h0_re, h0_im)


def _ssm_glu_kernel(u_ref, ys_ref, d_ref, wa_ref, wg_ref, ba_ref, bg_ref, o_ref, yb_ref):
    @pl.when(pl.program_id(1) == 0)
    def _():
        yb_ref[...] = jax.nn.gelu(d_ref[...] * u_ref[...] + ys_ref[...]).astype(BF16)

    y = yb_ref[...]
    a = jnp.dot(y, wa_ref[...].astype(BF16), preferred_element_type=F32) + ba_ref[...]
    g = jnp.dot(y, wg_ref[...].astype(BF16), preferred_element_type=F32) + bg_ref[...]
    o_ref[...] = a * jax.nn.sigmoid(g)


def _ssm_glu(proj, ysum, ssm_d3, w_glu, b_glu3, l):
    tm, tn = 512, 512
    nj = D_MODEL // tn
    return pl.pallas_call(
        _ssm_glu_kernel,
        grid=(T // tm, nj),
        in_specs=[pl.BlockSpec((tm, D_SSM), lambda i, j: (i, 0)),
                  pl.BlockSpec((tm, D_SSM), lambda i, j: (i, 0)),
                  pl.BlockSpec((None, 1, D_SSM), lambda i, j: (l, 0, 0)),
                  pl.BlockSpec((None, D_SSM, tn), lambda i, j: (l, 0, j)),
                  pl.BlockSpec((None, D_SSM, tn), lambda i, j: (l, 0, j + nj)),
                  pl.BlockSpec((None, 1, tn), lambda i, j: (l, 0, j)),
                  pl.BlockSpec((None, 1, tn), lambda i, j: (l, 0, j + nj))],
        out_specs=pl.BlockSpec((tm, tn), lambda i, j: (i, j)),
        out_shape=jax.ShapeDtypeStruct((T, D_MODEL), F32),
        scratch_shapes=[pltpu.VMEM((tm, D_SSM), BF16)],
        compiler_params=_cp(("arbitrary", "arbitrary")),
        name="ssm_glu",
    )(proj, ysum, ssm_d3, w_glu, w_glu, b_glu3, b_glu3)


def _mla_prep_kernel(cq_ref, ckv_ref, kr_ref, qg_ref, kvg_ref, wqa_ref, wqb_ref,
                     ca_ref, sb_ref, ck_ref, sk_ref,
                     q_ref, ckvn_ref, krr_ref, wqa_b, wqb_b):
    @pl.when(pl.program_id(0) == 0)
    def _():
        wqa_b[...] = wqa_ref[...].astype(BF16)
        wqb_b[...] = wqb_ref[...].astype(BF16)

    cq = cq_ref[...]
    cqn = (cq * lax.rsqrt(jnp.mean(cq * cq, axis=-1, keepdims=True) + EPS) * qg_ref[...]).astype(BF16)
    qa = jnp.dot(cqn, wqa_b[...], preferred_element_type=F32)
    qb = jnp.dot(cqn, wqb_b[...], preferred_element_type=F32)
    ca = ca_ref[...]
    sb = sb_ref[...]
    for h in range(N_HEADS):
        cols = slice(h * 128, (h + 1) * 128)
        q_ref[:, cols] = (qa[:, cols] * ca + qb[:, cols] * sb).astype(q_ref.dtype)

    ckv = ckv_ref[...]
    ckvn_ref[...] = ckv * lax.rsqrt(jnp.mean(ckv * ckv, axis=-1, keepdims=True) + EPS) * kvg_ref[...]

    kr = kr_ref[...]
    krr_ref[...] = kr * ck_ref[...] + pltpu.roll(kr, 128 - ROPE_DIM, 1) * sk_ref[...]


def _mla_prep(proj, q_norm_g3, kv_norm_g3, wqa, wqb, tabs, l):
    ca, sb, ck, sk = tabs
    tab_spec = pl.BlockSpec((TB, 128), lambda i: (_rope_blk(i), 0))
    return pl.pallas_call(
        _mla_prep_kernel,
        grid=(NTB,),
        in_specs=[pl.BlockSpec((TB, Q_LORA), lambda i: (i, 2)),
                  pl.BlockSpec((TB, 128), lambda i: (i, 6)),
                  pl.BlockSpec((TB, 128), lambda i: (i, 7)),
                  pl.BlockSpec((None, 1, Q_LORA), lambda i: (l, 0, 0)),
                  pl.BlockSpec((None, 1, KV_LORA), lambda i: (l, 0, 0)),
                  pl.BlockSpec((None, Q_LORA, 1024), lambda i: (l, 0, 0)),
                  pl.BlockSpec((None, Q_LORA, 1024), lambda i: (l, 0, 0)),
                  tab_spec, tab_spec, tab_spec, tab_spec],
        out_specs=[pl.BlockSpec((TB, 1024), lambda i: (i, 0)),
                   pl.BlockSpec((TB, 128), lambda i: (i, 0)),
                   pl.BlockSpec((TB, 128), lambda i: (i, 0))],
        out_shape=(jax.ShapeDtypeStruct((T, 1024), BF16),
                   jax.ShapeDtypeStruct((T, 128), F32),
                   jax.ShapeDtypeStruct((T, 128), F32)),
        scratch_shapes=[pltpu.VMEM((Q_LORA, 1024), BF16), pltpu.VMEM((Q_LORA, 1024), BF16)],
        compiler_params=_cp(("arbitrary",)),
        name="mla_prep",
    )(proj, proj, proj, q_norm_g3, kv_norm_g3, wqa, wqb, ca, sb, ck, sk)


def _attn_kernel(q_ref, kvin_ref, wk_ref, wv_ref, wo_ref, o_ref,
                 k_scr, v_scr, o_scr, wk_b, wv_b, wo_b):
    s_id = pl.program_id(0)
    qb = pl.program_id(1)

    @pl.when((s_id == 0) & (qb == 0))
    def _():
        wk_b[...] = wk_ref[...].astype(BF16)
        wv_b[...] = wv_ref[...].astype(BF16)
        wo_b[...] = wo_ref[...].astype(BF16)

    @pl.when(qb == 0)
    def _():
        kv = kvin_ref[...].astype(BF16)
        k_scr[...] = jnp.dot(kv, wk_b[...], preferred_element_type=F32).astype(BF16)
        v_scr[...] = jnp.dot(kv[:, :KV_LORA], wv_b[...], preferred_element_type=F32).astype(BF16)

    scale = (NOPE_DIM + ROPE_DIM) ** -0.5
    for h in range(N_HEADS):
        qh = q_ref[:, h * 128:(h + 1) * 128]
        kh = k_scr[:, h * 128:(h + 1) * 128]
        s = lax.dot_general(qh, kh, (((1,), (1,)), ((), ())), preferred_element_type=F32) * scale
        m = jnp.max(s, axis=-1, keepdims=True)
        p = jnp.exp(s - m)
        den = jnp.sum(p, axis=-1, keepdims=True)
        oh = jnp.dot(p.astype(BF16), v_scr[:, h * V_DIM:(h + 1) * V_DIM], preferred_element_type=F32)
        o_scr[:, h * V_DIM:(h + 1) * V_DIM] = oh / den
    o_ref[...] = jnp.dot(o_scr[...].astype(BF16), wo_b[...], preferred_element_type=F32)


def _attn(q, kvin, wk2, wv, w_o, l, nseq, nqb, row_blk0):
    lk = kvin.shape[1]
    return pl.pallas_call(
        _attn_kernel,
        grid=(nseq, nqb),
        in_specs=[pl.BlockSpec((TB, 1024), lambda s, b: (row_blk0 + s * nqb + b, 0)),
                  pl.BlockSpec((None, lk, 256), lambda s, b: (s, 0, 0)),
                  pl.BlockSpec((None, 256, 1024), lambda s, b: (l, 0, 0)),
                  pl.BlockSpec((None, KV_LORA, 512), lambda s, b: (l, 0, 0)),
                  pl.BlockSpec((None, 512, D_MODEL), lambda s, b: (l, 0, 0))],
        out_specs=pl.BlockSpec((TB, D_MODEL), lambda s, b: (s * nqb + b, 0)),
        out_shape=jax.ShapeDtypeStruct((nseq * nqb * TB, D_MODEL), F32),
        scratch_shapes=[pltpu.VMEM((lk, 1024), BF16), pltpu.VMEM((lk, 512), BF16),
                        pltpu.VMEM((TB, 512), F32),
                        pltpu.VMEM((256, 1024), BF16), pltpu.VMEM((KV_LORA, 512), BF16),
                        pltpu.VMEM((512, D_MODEL), BF16)],
        compiler_params=_cp(("arbitrary", "arbitrary"), VMEM_LIMIT),
        name="attn",
    )(q, kvin, wk2, wv, w_o)


CONV_PAD = 16
CONV_CB = 128
CONV_RB = 32


def _conv_kernel(cin_ref, w_ref, b_ref, lg_ref, lb_ref, pw_ref, pb_ref, o_ref,
                 pad_scr, conv_scr, act_scr, pw_b, *, seq):
    @pl.when(pl.program_id(0) == 0)
    def _():
        pw_b[...] = pw_ref[...].astype(BF16)
        pad_scr[0:CONV_PAD, :] = jnp.zeros((CONV_PAD, D_CONV), F32)
        pad_scr[CONV_PAD + seq:, :] = jnp.zeros((CONV_PAD, D_CONV), F32)

    pad_scr[CONV_PAD:CONV_PAD + seq, :] = cin_ref[:, :D_CONV] * jax.nn.sigmoid(cin_ref[:, D_CONV:])

    win_rows = CONV_CB + 2 * CONV_PAD
    for c in range(D_CONV // 128):
        lanes = slice(c * 128, (c + 1) * 128)

        def cblk(rb, carry, lanes=lanes):
            base = pl.multiple_of(rb * CONV_CB, CONV_CB)
            win = pad_scr[pl.ds(base, win_rows), lanes]
            acc = jnp.broadcast_to(b_ref[:, lanes], (CONV_CB, 128))
            for b in range(8):
                wb = win if b == 0 else pltpu.roll(win, win_rows - b, 0)
                for a in range(2 * CONV_PAD // 8):
                    k = 8 * a + b - (CONV_PAD - CONV_K // 2)
                    if 0 <= k < CONV_K:
                        acc = acc + w_ref[k:k + 1, lanes] * wb[8 * a:8 * a + CONV_CB]
            conv_scr[pl.ds(base, CONV_CB), lanes] = acc
            return carry

        lax.fori_loop(0, seq // CONV_CB, cblk, 0)

    def blk(rb, carry):
        base = pl.multiple_of(rb * CONV_RB, CONV_RB)
        acc = conv_scr[pl.ds(base, CONV_RB), :]
        mu = jnp.mean(acc, axis=-1, keepdims=True)
        xc = acc - mu
        var = jnp.mean(xc * xc, axis=-1, keepdims=True)
        y = xc * lax.rsqrt(var + EPS) * lg_ref[...] + lb_ref[...]
        act_scr[pl.ds(base, CONV_RB), :] = (y * jax.nn.sigmoid(y)).astype(BF16)
        return carry

    lax.fori_loop(0, seq // CONV_RB, blk, 0)
    o_ref[...] = jnp.dot(act_scr[...], pw_b[...], preferred_element_type=F32) + pb_ref[...]


def _conv(proj, dw_w, dw_b3, ln_g3, ln_b3, pw, pb3, l, nseq, seq, row_blk0):
    vec = lambda n: pl.BlockSpec((None, 1, n), lambda s: (l, 0, 0))
    return pl.pallas_call(
        functools.partial(_conv_kernel, seq=seq),
        grid=(nseq,),
        in_specs=[pl.BlockSpec((seq, 2 * D_CONV), lambda s: (row_blk0 + s, 1)),
                  pl.BlockSpec((None, CONV_K, D_CONV), lambda s: (l, 0, 0)),
                  vec(D_CONV), vec(D_CONV), vec(D_CONV),
                  pl.BlockSpec((None, D_CONV, D_MODEL), lambda s: (l, 0, 0)),
                  vec(D_MODEL)],
        out_specs=pl.BlockSpec((seq, D_MODEL), lambda s: (s, 0)),
        out_shape=jax.ShapeDtypeStruct((nseq * seq, D_MODEL), F32),
        scratch_shapes=[pltpu.VMEM((seq + 2 * CONV_PAD, D_CONV), F32),
                        pltpu.VMEM((seq, D_CONV), F32),
                        pltpu.VMEM((seq, D_CONV), BF16),
                        pltpu.VMEM((D_CONV, D_MODEL), BF16)],
        compiler_params=_cp(("arbitrary",), VMEM_LIMIT),
        name="conv",
    )(proj, dw_w, dw_b3, ln_g3, ln_b3, pw, pb3)


def _merge_kernel(x_ref, os_ref, om_ref, oc_ref, ga_ref, gb_ref, gc_ref, wo_ref, mod_ref, ng_ref,
                  rw_ref, rb_ref,
                  x1_ref, h2_ref, ti_ref, tw_ref, wo_b):
    @pl.when(pl.program_id(0) == 0)
    def _():
        wo_b[...] = wo_ref[...].astype(BF16)

    merged = (jax.nn.sigmoid(ga_ref[...]) * os_ref[...]
              + jax.nn.sigmoid(gb_ref[...]) * om_ref[...]
              + jax.nn.sigmoid(gc_ref[...]) * oc_ref[...])
    out = jnp.dot(merged.astype(BF16), wo_b[...], preferred_element_type=F32)
    x1 = x_ref[...] + mod_ref[2:3, :] * out
    x1_ref[...] = x1
    y = x1 * lax.rsqrt(jnp.mean(x1 * x1, axis=-1, keepdims=True) + EPS) * ng_ref[...]
    h2 = y * (1.0 + mod_ref[4:5, :]) + mod_ref[3:4, :]
    h2_ref[...] = h2

    logits = jnp.dot(h2, rw_ref[...], preferred_element_type=F32,
                     precision=lax.Precision.HIGHEST) + rb_ref[...]
    lane_i = lax.broadcasted_iota(jnp.int32, logits.shape, 1)
    lane = lane_i.astype(F32)
    cur = logits
    vals, idxs = [], []
    for _ in range(TOP_K):
        m = jnp.max(cur, axis=-1, keepdims=True)
        idx = jnp.min(jnp.where(cur == m, lane, 128.0), axis=-1, keepdims=True)
        vals.append(m)
        idxs.append(idx)
        cur = jnp.where(lane == idx, -jnp.inf, cur)
    exps = [jnp.exp(v - vals[0]) for v in vals]
    tot = exps[0] + exps[1] + exps[2] + exps[3]
    ti = jnp.zeros(logits.shape, F32)
    tw = jnp.zeros(logits.shape, F32)
    for k in range(TOP_K):
        ti = jnp.where(lane_i == k, idxs[k], ti)
        tw = jnp.where(lane_i == k, exps[k] / tot, tw)
    ti_ref[...] = ti.astype(jnp.int32)
    tw_ref[...] = tw


def _merge(x, o_ssm, o_mla, o_conv, proj, w_out, modt, norm_ffn_g3, rw_p, rb_p, l):
    row = lambda i: (i, 0)
    return pl.pallas_call(
        _merge_kernel,
        grid=(NTB,),
        in_specs=[pl.BlockSpec((TB, D_MODEL), row), pl.BlockSpec((TB, D_MODEL), row),
                  pl.BlockSpec((TB, D_MODEL), row), pl.BlockSpec((TB, D_MODEL), row),
                  pl.BlockSpec((TB, D_MODEL), lambda i: (i, 2)),
                  pl.BlockSpec((TB, D_MODEL), lambda i: (i, 3)),
                  pl.BlockSpec((TB, D_MODEL), lambda i: (i, 4)),
                  pl.BlockSpec((None, D_MODEL, D_MODEL), lambda i: (l, 0, 0)),
                  pl.BlockSpec((None, 6, D_MODEL), lambda i: (_mod_row(i), 0, 0)),
                  pl.BlockSpec((None, 1, D_MODEL), lambda i: (l, 0, 0)),
                  pl.BlockSpec((None, D_MODEL, 128), lambda i: (l, 0, 0)),
                  pl.BlockSpec((None, 1, 128), lambda i: (l, 0, 0))],
        out_specs=[pl.BlockSpec((TB, D_MODEL), row), pl.BlockSpec((TB, D_MODEL), row),
                   pl.BlockSpec((TB, 128), row), pl.BlockSpec((TB, 128), row)],
        out_shape=(jax.ShapeDtypeStruct((T, D_MODEL), F32), jax.ShapeDtypeStruct((T, D_MODEL), F32),
                   jax.ShapeDtypeStruct((T, 128), jnp.int32), jax.ShapeDtypeStruct((T, 128), F32)),
        scratch_shapes=[pltpu.VMEM((D_MODEL, D_MODEL), BF16)],
        compiler_params=_cp(("arbitrary",), VMEM_LIMIT),
        name="merge",
    )(x, o_ssm, o_mla, o_conv, proj, proj, proj, w_out, modt, norm_ffn_g3, rw_p, rb_p)


def _row_copy(src_ref, dst_ref, sem, r_src, r_dst):
    return pltpu.make_async_copy(src_ref.at[pl.ds(r_src, 1)], dst_ref.at[pl.ds(r_dst, 1)], sem)


def _dispatch_kernel(dest_ref, h_ref, xs_in_ref, xs_ref, sem):
    del xs_in_ref

    def issue(r, carry):
        for k in range(TOP_K):
            _row_copy(h_ref, xs_ref, sem, r, dest_ref[r * TOP_K + k]).start()
        return carry

    lax.fori_loop(0, TB, issue, 0)

    def drain(r, carry):
        for k in range(TOP_K):
            _row_copy(h_ref, xs_ref, sem, r, dest_ref[r * TOP_K + k]).wait()
        return carry

    lax.fori_loop(0, TB, drain, 0)


def _dispatch(dest, h2, xs_zero):
    return pl.pallas_call(
        _dispatch_kernel,
        grid=(NTB,),
        in_specs=[pl.BlockSpec((TB * TOP_K,), lambda i: (i,), memory_space=pltpu.SMEM),
                  pl.BlockSpec((TB, D_MODEL), lambda i: (i, 0)),
                  pl.BlockSpec(memory_space=pl.ANY)],
        out_specs=pl.BlockSpec(memory_space=pl.ANY),
        out_shape=jax.ShapeDtypeStruct((MOE_ROWS, D_MODEL), F32),
        scratch_shapes=[pltpu.SemaphoreType.DMA(())],
        input_output_aliases={2: 0},
        compiler_params=_cp(("arbitrary",)),
        name="moe_dispatch",
    )(dest, h2, xs_zero)


def _expert_kernel(be_ref, nu_ref, x_ref, w1_ref, b1_ref, w2_ref, b2_ref, y_ref, w1_b, w2_b):
    i = pl.program_id(0)
    prev = be_ref[jnp.maximum(i - 1, 0)]

    @pl.when((i == 0) | (be_ref[i] != prev))
    def _():
        w1_b[...] = w1_ref[...].astype(BF16)
        w2_b[...] = w2_ref[...].astype(BF16)

    @pl.when(i < nu_ref[0])
    def _():
        gu = jnp.dot(x_ref[...].astype(BF16), w1_b[...], preferred_element_type=F32) + b1_ref[...]
        g = jnp.minimum(gu[:, :D_FF], SWIGLU_LIMIT)
        u = jnp.clip(gu[:, D_FF:], -SWIGLU_LIMIT, SWIGLU_LIMIT)
        act = (u + 1.0) * (g * jax.nn.sigmoid(SWIGLU_ALPHA * g))
        y_ref[...] = jnp.dot(act.astype(BF16), w2_b[...], preferred_element_type=F32) + b2_ref[...]

    @pl.when(i >= nu_ref[0])
    def _():
        y_ref[...] = jnp.zeros_like(y_ref)


def _experts(block_e, n_used, xs, w1, b1_4, w2, b2_4, l):
    def blk(i, be, nu):
        return (jnp.minimum(i, nu[0] - 1), 0)

    grid_spec = pltpu.PrefetchScalarGridSpec(
        num_scalar_prefetch=2,
        grid=(MOE_NB,),
        in_specs=[pl.BlockSpec((MOE_BM, D_MODEL), blk),
                  pl.BlockSpec((None, None, D_MODEL, 2 * D_FF), lambda i, be, nu: (l, be[i], 0, 0)),
                  pl.BlockSpec((None, None, 1, 2 * D_FF), lambda i, be, nu: (l, be[i], 0, 0)),
                  pl.BlockSpec((None, None, D_FF, D_MODEL), lambda i, be, nu: (l, be[i], 0, 0)),
                  pl.BlockSpec((None, None, 1, D_MODEL), lambda i, be, nu: (l, be[i], 0, 0))],
        out_specs=pl.BlockSpec((MOE_BM, D_MODEL), lambda i, be, nu: (i, 0)),
        scratch_shapes=[pltpu.VMEM((D_MODEL, 2 * D_FF), BF16), pltpu.VMEM((D_FF, D_MODEL), BF16)],
    )
    return pl.pallas_call(
        _expert_kernel,
        grid_spec=grid_spec,
        out_shape=jax.ShapeDtypeStruct((MOE_ROWS, D_MODEL), F32),
        compiler_params=_cp(("arbitrary",), VMEM_LIMIT),
        name="moe_experts",
    )(block_e, n_used, xs, w1, b1_4, w2, b2_4)


def _combine_kernel(dest_ref, x1_ref, tw_ref, mod_ref, fg_ref, ys_ref, x2_ref, yn_ref, stg, sem):
    def issue(r, carry):
        for k in range(TOP_K):
            _row_copy(ys_ref, stg.at[k], sem, dest_ref[r * TOP_K + k], r).start()
        return carry

    lax.fori_loop(0, TB, issue, 0)

    def drain(r, carry):
        for k in range(TOP_K):
            _row_copy(ys_ref, stg.at[k], sem, dest_ref[r * TOP_K + k], r).wait()
        return carry

    lax.fori_loop(0, TB, drain, 0)

    tw = tw_ref[...]
    moe = tw[:, 0:1] * stg[0]
    for k in range(1, TOP_K):
        moe = moe + tw[:, k:k + 1] * stg[k]
    x2 = x1_ref[...] + mod_ref[5:6, :] * moe
    x2_ref[...] = x2
    yn_ref[...] = x2 * lax.rsqrt(jnp.mean(x2 * x2, axis=-1, keepdims=True) + EPS) * fg_ref[...]


def _combine(dest, x1, topw, modt, final_g2, ys):
    row = lambda i: (i, 0)
    return pl.pallas_call(
        _combine_kernel,
        grid=(NTB,),
        in_specs=[pl.BlockSpec((TB * TOP_K,), lambda i: (i,), memory_space=pltpu.SMEM),
                  pl.BlockSpec((TB, D_MODEL), row),
                  pl.BlockSpec((TB, 128), row),
                  pl.BlockSpec((None, 6, D_MODEL), lambda i: (_mod_row(i), 0, 0)),
                  pl.BlockSpec((1, D_MODEL), lambda i: (0, 0)),
                  pl.BlockSpec(memory_space=pl.ANY)],
        out_specs=[pl.BlockSpec((TB, D_MODEL), row), pl.BlockSpec((TB, D_MODEL), row)],
        out_shape=(jax.ShapeDtypeStruct((T, D_MODEL), F32), jax.ShapeDtypeStruct((T, D_MODEL), F32)),
        scratch_shapes=[pltpu.VMEM((TOP_K, TB, D_MODEL), F32), pltpu.SemaphoreType.DMA(())],
        compiler_params=_cp(("arbitrary",), VMEM_LIMIT),
        name="moe_combine",
    )(dest, x1, topw, modt, final_g2, ys)


def _rot_cols(w):
    q = ROPE_DIM // 4
    parts = []
    for half in range(2):
        a = w[..., half * 2 * q:half * 2 * q + q]
        b = w[..., half * 2 * q + q:(half + 1) * 2 * q]
        parts += [-b, a]
    return jnp.concatenate(parts, axis=-1)


def _rope_tables():
    rows = DEC_SEQ // GRID_W
    row = jnp.repeat(jnp.arange(rows), GRID_W).astype(F32)
    col = jnp.tile(jnp.arange(GRID_W), rows).astype(F32)
    half = ROPE_DIM // 2
    freqs = ROPE_BASE ** (-jnp.arange(0, half, 2, dtype=F32) / half)
    ang_r = row[:, None] * freqs
    ang_c = col[:, None] * freqs
    ang = jnp.concatenate([ang_r, ang_r, ang_c, ang_c], axis=-1)
    cos = jnp.concatenate([jnp.cos(ang), jnp.ones((TB, ROPE_DIM), F32)], axis=0)
    sin = jnp.concatenate([jnp.sin(ang), jnp.zeros((TB, ROPE_DIM), F32)], axis=0)
    n = cos.shape[0]
    z = lambda w: jnp.zeros((n, w), F32)
    ca = jnp.concatenate([jnp.ones((n, NOPE_DIM), F32), cos, z(32)], axis=1)
    sb = jnp.concatenate([z(NOPE_DIM), sin, z(32)], axis=1)
    ck = jnp.concatenate([cos, z(96)], axis=1)
    sk = jnp.concatenate([sin, z(96)], axis=1)
    return ca, sb, ck, sk


def kernel(x_prompt, x_sample, c, cache_kv_latent, cache_k_rope, state_ssm, c_ctx, norm_mix_g, norm_ffn_g, final_norm_g, w_ada, b_ada, w_in, q_norm_g, w_uq, kv_norm_g, w_ukv, w_o_mla, ssm_lam_re, ssm_lam_im, ssm_log_dt, ssm_b_re, ssm_b_im, ssm_c_re, ssm_c_im, ssm_d, w_ssm_glu, b_ssm_glu, conv_dw_w, conv_dw_b, conv_ln_g, conv_ln_b, w_conv_pw2, b_conv_pw2, w_out, router_w, router_b, moe_w1, moe_b1, moe_w2, moe_b2):
    L = DEPTH
    v3 = lambda a: a.reshape(L, 1, a.shape[-1])

    x = jnp.concatenate([x_prompt.reshape(T_P, D_MODEL), x_sample.reshape(T_S, D_MODEL)], axis=0)
    cc = jnp.concatenate([c, c_ctx[None, :], jnp.zeros((16 - DEC_BATCH - 1, D_MODEL), F32)], axis=0)

    kr0 = D_SSM + Q_LORA + KV_LORA
    w_kr = w_in[:, :, kr0:kr0 + ROPE_DIM]
    w_in_p = jnp.concatenate([w_in[:, :, :kr0], w_kr, _rot_cols(w_kr), jnp.zeros((L, D_MODEL, 64), F32),
                              w_in[:, :, kr0 + ROPE_DIM:]], axis=-1)

    wq = w_uq.reshape(L, Q_LORA, N_HEADS, NOPE_DIM + ROPE_DIM)
    zq = lambda w: jnp.zeros((L, Q_LORA, N_HEADS, w), F32)
    wqa = jnp.concatenate([wq, zq(32)], axis=-1).reshape(L, Q_LORA, 1024)
    wqb = jnp.concatenate([zq(NOPE_DIM), _rot_cols(wq[..., NOPE_DIM:]), zq(32)], axis=-1).reshape(L, Q_LORA, 1024)
    wkv = w_ukv.reshape(L, KV_LORA, N_HEADS, NOPE_DIM + V_DIM)
    wk_lat = jnp.concatenate([wkv[..., :NOPE_DIM], jnp.zeros((L, KV_LORA, N_HEADS, 64), F32)], axis=-1)
    place = jnp.concatenate([jnp.zeros((ROPE_DIM, NOPE_DIM), F32), jnp.eye(ROPE_DIM, dtype=F32),
                             jnp.zeros((ROPE_DIM, 32), F32)], axis=-1)
    wk_rope = jnp.broadcast_to(place[None, :, None, :], (L, ROPE_DIM, N_HEADS, 128))
    wk2 = jnp.concatenate([wk_lat, wk_rope, jnp.zeros((L, 96, N_HEADS, 128), F32)], axis=1).reshape(L, 256, 1024)
    wv = wkv[..., NOPE_DIM:].reshape(L, KV_LORA, N_HEADS * V_DIM)
    tabs = _rope_tables()

    a_re, a_im, f_re, f_im = [v.reshape(L, 2, N_SSM_GROUPS, N_STATE)
                              for v in _s5_disc(ssm_lam_re, ssm_lam_im, ssm_log_dt)]
    bb_re = f_re[..., None] * ssm_b_re - f_im[..., None] * ssm_b_im
    bb_im = f_re[..., None] * ssm_b_im + f_im[..., None] * ssm_b_re
    eye16 = jnp.eye(16, dtype=F32)
    eye8 = jnp.eye(8, dtype=F32)

    def bd_in(bb):
        bb = bb.reshape(L, 2, 2, 16, N_STATE, SSM_GROUP)
        return jnp.einsum('ldhgpc,gk->ldhgckp', bb, eye16).reshape(L, 2, 2, 256, 1024)

    def bd_out(cm):
        cm = cm.reshape(L, 2, 4, 8, SSM_GROUP, N_STATE)
        return jnp.einsum('ldqgcp,gk->ldqgpkc', cm, eye8).reshape(L, 2, 4, 512, 128)

    wb = jnp.stack([bd_in(bb_re), bd_in(bb_im)], axis=2).astype(BF16)
    wc = jnp.stack([bd_out(ssm_c_re), bd_out(-ssm_c_im)], axis=2).astype(BF16)
    a_re4 = a_re.reshape(L, 2, 1, SSM_STATE)
    a_im4 = a_im.reshape(L, 2, 1, SSM_STATE)

    rw_p = jnp.concatenate([router_w, jnp.zeros((L, D_MODEL, 128 - N_EXPERTS), F32)], axis=-1)
    rb_p = jnp.concatenate([router_b, jnp.full((L, 128 - N_EXPERTS), -1e30, F32)], axis=-1).reshape(L, 1, 128)

    kv_out, kr_out, ssm_out = [], [], []
    y_norm = None
    for l in range(L):
        modt = _ada(cc, w_ada, v3(b_ada), l).reshape(16, 6, D_MODEL)
        h = _norm1(x, v3(norm_mix_g), modt, l)
        proj = _mm(h, w_in_p, l, tm=1024, tn=1024, name="w_in")

        u_p = proj[:T_P, :D_SSM].reshape(4, 8, SEQ, D_SSM).transpose(0, 2, 1, 3)
        u_s = proj[T_P:, :D_SSM].reshape(DEC_BATCH, 4, S5_CHUNK, D_SSM).transpose(1, 2, 0, 3)
        u_all = jnp.concatenate([u_p, u_s], axis=0)
        st = state_ssm[:, l]
        h0 = st.transpose(4, 1, 0, 2, 3).reshape(2, 2, DEC_BATCH, SSM_STATE)
        y_f, y_r, fin = _s5(u_all, wb, wc, a_re4, a_im4, h0[0], h0[1], l)
        ysum = y_f + y_r
        ysum = jnp.concatenate([ysum[:4].transpose(0, 2, 1, 3).reshape(T_P, D_SSM),
                                ysum[4:].transpose(2, 0, 1, 3).reshape(T_S, D_SSM)], axis=0)
        o_ssm = _ssm_glu(proj, ysum, v3(ssm_d), w_ssm_glu, v3(b_ssm_glu), l)
        ssm_out.append(fin.reshape(4, 2, 2, 8, N_SSM_GROUPS, N_STATE).transpose(0, 3, 1, 4, 5, 2)
                       .reshape(BATCH, 2, N_SSM_GROUPS, N_STATE, 2))

        q, ckvn, krr = _mla_prep(proj, v3(q_norm_g), v3(kv_norm_g), wqa, wqb, tabs, l)
        kvin_p = jnp.concatenate([ckvn[:T_P], krr[:T_P]], axis=-1).reshape(BATCH, SEQ, 256)
        cache_kr = jnp.concatenate([cache_k_rope[:, l], jnp.zeros((DEC_BATCH, PAST_LEN, 96), F32)], axis=-1)
        kvin_s = jnp.concatenate([
            jnp.concatenate([cache_kv_latent[:, l], cache_kr], axis=-1),
            jnp.concatenate([ckvn[T_P:], krr[T_P:]], axis=-1).reshape(DEC_BATCH, DEC_SEQ, 256)], axis=1)
        o_mla = jnp.concatenate([
            _attn(q, kvin_p, wk2, wv, w_o_mla, l, BATCH, 1, 0),
            _attn(q, kvin_s, wk2, wv, w_o_mla, l, DEC_BATCH, TB_PER_S, NTB_P)], axis=0)
        kv_out.append(ckvn[:T_P].reshape(BATCH, SEQ, KV_LORA))
        kr_out.append(proj[:T_P, 896:896 + ROPE_DIM].reshape(BATCH, SEQ, ROPE_DIM))

        conv_args = (conv_dw_w, v3(conv_dw_b), v3(conv_ln_g), v3(conv_ln_b), w_conv_pw2, v3(b_conv_pw2), l)
        o_conv = jnp.concatenate([
            _conv(proj, *conv_args, BATCH, SEQ, 0),
            _conv(proj, *conv_args, DEC_BATCH, DEC_SEQ, T_P // DEC_SEQ)], axis=0)

        x1, h2, topi, topw = _merge(x, o_ssm, o_mla, o_conv, proj, w_out, modt, v3(norm_ffn_g), rw_p, rb_p, l)

        e = topi[:, :TOP_K].reshape(-1)
        oh = (e[:, None] == jnp.arange(N_EXPERTS, dtype=jnp.int32)[None, :]).astype(jnp.int32)
        csum = jnp.cumsum(oh, axis=0)
        counts = csum[-1]
        padded = (counts + MOE_BM - 1) // MOE_BM * MOE_BM
        pad_end = jnp.cumsum(padded)
        pad_start = pad_end - padded
        dest = jnp.sum(oh * (csum - 1 + pad_start[None, :]), axis=1).astype(jnp.int32)
        block_e = jnp.minimum(
            jnp.searchsorted(pad_end, jnp.arange(MOE_NB, dtype=jnp.int32) * MOE_BM, side='right'),
            N_EXPERTS - 1).astype(jnp.int32)
        n_used = (pad_end[-1:] // MOE_BM).astype(jnp.int32)

        xs = _dispatch(dest, h2, jnp.zeros((MOE_ROWS, D_MODEL), F32))
        ys = _experts(block_e, n_used, xs, moe_w1, moe_b1.reshape(L, N_EXPERTS, 1, 2 * D_FF),
                      moe_w2, moe_b2.reshape(L, N_EXPERTS, 1, D_MODEL), l)
        x, y_norm = _combine(dest, x1, topw, modt, final_norm_g.reshape(1, D_MODEL), ys)

    y_prompt = y_norm[:T_P].reshape(BATCH, SEQ, D_MODEL)
    y_sample = y_norm[T_P:].reshape(DEC_BATCH, DEC_SEQ, D_MODEL)
    return (y_prompt, y_sample, jnp.stack(kv_out, axis=1), jnp.stack(kr_out, axis=1), jnp.stack(ssm_out, axis=1))
```

```python
import functools
import math

import numpy as np
import jax
import jax.numpy as jnp
from jax import lax
from jax.experimental import pallas as pl
from jax.experimental.pallas import tpu as pltpu

F32 = jnp.float32
BF16 = jnp.bfloat16

D_MODEL = 1024
BATCH = 32
SEQ = 256
DEPTH = 2
DEC_BATCH = 8
DEC_SEQ = 1024
PAST_LEN = 256
GRID_W = 64
EPS = 1e-6
D_SSM = 512
SSM_GROUP = 16
N_SSM_GROUPS = 32
N_STATE = 64
N_HEADS = 8
NOPE_DIM = 64
ROPE_DIM = 32
V_DIM = 64
Q_LORA = 256
KV_LORA = 128
ROPE_BASE = 10000.0
D_CONV = 512
CONV_K = 31
N_EXPERTS = 32
TOP_K = 4
D_FF = 1024
SWIGLU_ALPHA = 1.702
SWIGLU_LIMIT = 7.0

T_P = BATCH * SEQ
T_S = DEC_BATCH * DEC_SEQ
T = T_P + T_S
TB = 256
NTB = T // TB
NTB_P = T_P // TB
TB_PER_S = DEC_SEQ // TB
SSM_STATE = N_SSM_GROUPS * N_STATE
PROJ_W = 5120
MOE_BM = 512
SEG_ALIGN = 8
MOE_NB = (T * TOP_K + (SEG_ALIGN - 1) * NTB * N_EXPERTS) // MOE_BM + N_EXPERTS
MOE_ROWS = MOE_NB * MOE_BM
VMEM_LIMIT = 56 * 1024 * 1024


def _cp(sem, vmem=None):
    return pltpu.CompilerParams(dimension_semantics=sem, vmem_limit_bytes=vmem)


def _sigmoid(x):
    return 0.5 * jnp.tanh(0.5 * x) + 0.5


def _mod_row(i):
    return jnp.where(i < NTB_P, DEC_BATCH, (i - NTB_P) // TB_PER_S)


def _rope_blk(i):
    return jnp.where(i < NTB_P, TB_PER_S, (i - NTB_P) % TB_PER_S)


def _ada_kernel(c_ref, w_ref, b_ref, o_ref):
    c = c_ref[...]
    s = c * jax.nn.sigmoid(c)
    o_ref[...] = jnp.dot(s, w_ref[...], preferred_element_type=F32,
                         precision=lax.Precision.HIGHEST) + b_ref[...]


def _ada(cc, w_ada, b_ada3, l):
    tn = 1024
    return pl.pallas_call(
        _ada_kernel,
        grid=(6 * D_MODEL // tn,),
        in_specs=[pl.BlockSpec((16, D_MODEL), lambda j: (0, 0)),
                  pl.BlockSpec((None, D_MODEL, tn), lambda j: (l, 0, j)),
                  pl.BlockSpec((None, 1, tn), lambda j: (l, 0, j))],
        out_specs=pl.BlockSpec((16, tn), lambda j: (0, j)),
        out_shape=jax.ShapeDtypeStruct((16, 6 * D_MODEL), F32),
        compiler_params=_cp(("arbitrary",)),
        name="ada",
    )(cc, w_ada, b_ada3)


def _norm1_kernel(x_ref, g_ref, mod_ref, h_ref):
    x = x_ref[...]
    y = x * lax.rsqrt(jnp.mean(x * x, axis=-1, keepdims=True) + EPS) * g_ref[...]
    h_ref[...] = (y * (1.0 + mod_ref[1:2, :]) + mod_ref[0:1, :]).astype(h_ref.dtype)


def _norm1(x, g3, modt, l):
    return pl.pallas_call(
        _norm1_kernel,
        grid=(NTB,),
        in_specs=[pl.BlockSpec((TB, D_MODEL), lambda i: (i, 0)),
                  pl.BlockSpec((None, 1, D_MODEL), lambda i: (l, 0, 0)),
                  pl.BlockSpec((None, 6, D_MODEL), lambda i: (_mod_row(i), 0, 0))],
        out_specs=pl.BlockSpec((TB, D_MODEL), lambda i: (i, 0)),
        out_shape=jax.ShapeDtypeStruct((T, D_MODEL), BF16),
        compiler_params=_cp(("arbitrary",)),
        name="norm1",
    )(x, g3, modt)


def _mm_kernel(a_ref, w_ref, o_ref, wb_ref):
    @pl.when(pl.program_id(1) == 0)
    def _():
        wb_ref[...] = w_ref[...].astype(BF16)

    o_ref[...] = jnp.dot(a_ref[...], wb_ref[...], preferred_element_type=F32).astype(o_ref.dtype)


def _mm(a, w, l, tm, tn, out_dtype=F32, name="mm"):
    m, k = a.shape
    n = w.shape[-1]
    return pl.pallas_call(
        _mm_kernel,
        grid=(n // tn, m // tm),
        in_specs=[pl.BlockSpec((tm, k), lambda j, i: (i, 0)),
                  pl.BlockSpec((None, k, tn), lambda j, i: (l, 0, j))],
        out_specs=pl.BlockSpec((tm, tn), lambda j, i: (i, j)),
        out_shape=jax.ShapeDtypeStruct((m, n), out_dtype),
        scratch_shapes=[pltpu.VMEM((k, tn), BF16)],
        compiler_params=_cp(("arbitrary", "arbitrary"), VMEM_LIMIT),
        name=name,
    )(a, w)


def _s5_disc_kernel(lr_ref, li_ref, ldt_ref, ar_ref, ai_ref, fr_ref, fi_ref):
    dt = jnp.exp(ldt_ref[...])
    lr = lr_ref[...]
    li = li_ref[...]
    mag = jnp.exp(lr * dt)
    a_re = mag * jnp.cos(li * dt)
    a_im = mag * jnp.sin(li * dt)
    den = lr * lr + li * li
    ar_ref[...] = a_re
    ai_ref[...] = a_im
    fr_ref[...] = ((a_re - 1.0) * lr + a_im * li) / den
    fi_ref[...] = (a_im * lr - (a_re - 1.0) * li) / den


def _s5_disc(lam_re, lam_im, log_dt):
    rows = DEPTH * 2 * N_SSM_GROUPS
    shp = jax.ShapeDtypeStruct((rows, N_STATE), F32)
    return pl.pallas_call(
        _s5_disc_kernel,
        out_shape=(shp, shp, shp, shp),
        name="s5_disc",
    )(lam_re.reshape(rows, N_STATE), lam_im.reshape(rows, N_STATE), log_dt.reshape(rows, 1))


S5_CHUNK = 256
S5_SUB = 64
S5_NSUB = S5_CHUNK // S5_SUB
S5_NCHUNK = 8


def _s5_kernel(uf_ref, ur_ref, wb_ref, wc_ref, are_ref, aim_ref, h0r_ref, h0i_ref,
               yf_ref, yr_ref, fin_ref,
               hre_ref, him_ref, ab_re, ab_im, xre_ref, xim_ref):
    k = pl.program_id(0)

    @pl.when(k < 4)
    def _():
        hre_ref[...] = jnp.zeros_like(hre_ref)
        him_ref[...] = jnp.zeros_like(him_ref)

    @pl.when(k == 4)
    def _():
        hre_ref[...] = h0r_ref[...]
        him_ref[...] = h0i_ref[...]

    for d, (u_ref, y_ref) in enumerate(((uf_ref, yf_ref), (ur_ref, yr_ref))):
        reverse = d == 1
        ab_re[...] = jnp.broadcast_to(are_ref[d], (8, SSM_STATE))
        ab_im[...] = jnp.broadcast_to(aim_ref[d], (8, SSM_STATE))

        def sub(s, carry, d=d, reverse=reverse, u_ref=u_ref, y_ref=y_ref):
            ss = (S5_NSUB - 1 - s) if reverse else s
            rows = pl.ds(pl.multiple_of(ss * S5_SUB, S5_SUB), S5_SUB)
            u2 = u_ref[rows].reshape(S5_SUB * 8, D_SSM).astype(BF16)
            for half in range(2):
                uh = u2[:, half * 256:(half + 1) * 256]
                cols = slice(half * 1024, (half + 1) * 1024)
                xre_ref[:, cols] = jnp.dot(uh, wb_ref[d, 0, half], preferred_element_type=F32)
                xim_ref[:, cols] = jnp.dot(uh, wb_ref[d, 1, half], preferred_element_type=F32)

            def step(t, h, reverse=reverse):
                tt = (S5_SUB - 1 - t) if reverse else t
                r = pl.ds(pl.multiple_of(tt * 8, 8), 8)
                h_re, h_im = h
                a_re = ab_re[...]
                a_im = ab_im[...]
                n_re = a_re * h_re - a_im * h_im + xre_ref[r, :]
                n_im = a_re * h_im + a_im * h_re + xim_ref[r, :]
                xre_ref[r, :] = n_re
                xim_ref[r, :] = n_im
                return n_re, n_im

            h_re, h_im = lax.fori_loop(0, S5_SUB, step, (hre_ref[d], him_ref[d]))
            hre_ref[d] = h_re
            him_ref[d] = h_im

            @pl.when((s == 0) & (k < 4))
            def _(d=d, reverse=reverse):
                r0 = (S5_SUB - 1) * 8 if reverse else 0
                fin_ref[d, 0] = xre_ref[r0:r0 + 8, :]
                fin_ref[d, 1] = xim_ref[r0:r0 + 8, :]

            x_re = xre_ref[...].astype(BF16)
            x_im = xim_ref[...].astype(BF16)
            for q in range(4):
                ks = slice(q * 512, (q + 1) * 512)
                yq = (jnp.dot(x_re[:, ks], wc_ref[d, 0, q], preferred_element_type=F32)
                      + jnp.dot(x_im[:, ks], wc_ref[d, 1, q], preferred_element_type=F32))
                y_ref[rows, :, q * 128:(q + 1) * 128] = yq.reshape(S5_SUB, 8, 128)
            return carry

        lax.fori_loop(0, S5_NSUB, sub, 0)


def _s5(u_all, wb, wc, a_re, a_im, h0_re, h0_im, l):
    def rev_chunk(k):
        return jnp.where(k < 4, k, 11 - k)

    blk = (None, S5_CHUNK, 8, D_SSM)
    y_shape = jax.ShapeDtypeStruct((S5_NCHUNK, S5_CHUNK, 8, D_SSM), F32)
    return pl.pallas_call(
        _s5_kernel,
        grid=(S5_NCHUNK,),
        in_specs=[pl.BlockSpec(blk, lambda k: (k, 0, 0, 0)),
                  pl.BlockSpec(blk, lambda k: (rev_chunk(k), 0, 0, 0)),
                  pl.BlockSpec((None, 2, 2, 2, 256, 1024), lambda k: (l, 0, 0, 0, 0, 0)),
                  pl.BlockSpec((None, 2, 2, 4, 512, 128), lambda k: (l, 0, 0, 0, 0, 0)),
                  pl.BlockSpec((None, 2, 1, SSM_STATE), lambda k: (l, 0, 0, 0)),
                  pl.BlockSpec((None, 2, 1, SSM_STATE), lambda k: (l, 0, 0, 0)),
                  pl.BlockSpec((2, 8, SSM_STATE), lambda k: (0, 0, 0)),
                  pl.BlockSpec((2, 8, SSM_STATE), lambda k: (0, 0, 0))],
        out_specs=[pl.BlockSpec(blk, lambda k: (k, 0, 0, 0)),
                   pl.BlockSpec(blk, lambda k: (rev_chunk(k), 0, 0, 0)),
                   pl.BlockSpec((None, 2, 2, 8, SSM_STATE), lambda k: (jnp.minimum(k, 3), 0, 0, 0, 0))],
        out_shape=(y_shape, y_shape,
                   jax.ShapeDtypeStruct((4, 2, 2, 8, SSM_STATE), F32)),
        scratch_shapes=[pltpu.VMEM((2, 8, SSM_STATE), F32), pltpu.VMEM((2, 8, SSM_STATE), F32),
                        pltpu.VMEM((8, SSM_STATE), F32), pltpu.VMEM((8, SSM_STATE), F32),
                        pltpu.VMEM((S5_SUB * 8, SSM_STATE), F32), pltpu.VMEM((S5_SUB * 8, SSM_STATE), F32)],
        compiler_params=_cp(("arbitrary",), VMEM_LIMIT),
        name="s5_scan",
    )(u_all, u_all, wb, wc, a_re, a_im, h0_re, h0_im)


GLU_TM = 512


def _ssm_glu_kernel(u_ref, ysp_ref, yss_ref, d_ref, w_ref, b_ref, o_ref, w_b):
    i = pl.program_id(0)

    @pl.when(i == 0)
    def _():
        w_b[...] = w_ref[...].astype(BF16)

    ys = jnp.where(i < T_P // GLU_TM, ysp_ref[...], yss_ref[...])
    y = jax.nn.gelu(d_ref[...] * u_ref[...] + ys).astype(BF16)
    glu = jnp.dot(y, w_b[...], preferred_element_type=F32) + b_ref[...]
    o_ref[...] = glu[:, :D_MODEL] * _sigmoid(glu[:, D_MODEL:])


def _ssm_glu(proj, ys_p, ys_s, ssm_d3, w_glu, b_glu3, l):
    nbp = T_P // GLU_TM
    return pl.pallas_call(
        _ssm_glu_kernel,
        grid=(T // GLU_TM,),
        in_specs=[pl.BlockSpec((GLU_TM, D_SSM), lambda i: (i, 0)),
                  pl.BlockSpec((GLU_TM, D_SSM), lambda i: (jnp.minimum(i, nbp - 1), 0)),
                  pl.BlockSpec((GLU_TM, D_SSM), lambda i: (jnp.maximum(i - nbp, 0), 0)),
                  pl.BlockSpec((None, 1, D_SSM), lambda i: (l, 0, 0)),
                  pl.BlockSpec((None, D_SSM, 2 * D_MODEL), lambda i: (l, 0, 0)),
                  pl.BlockSpec((None, 1, 2 * D_MODEL), lambda i: (l, 0, 0))],
        out_specs=pl.BlockSpec((GLU_TM, D_MODEL), lambda i: (i, 0)),
        out_shape=jax.ShapeDtypeStruct((T, D_MODEL), F32),
        scratch_shapes=[pltpu.VMEM((D_SSM, 2 * D_MODEL), BF16)],
        compiler_params=_cp(("arbitrary",), VMEM_LIMIT),
        name="ssm_glu",
    )(proj, ys_p, ys_s, ssm_d3, w_glu, b_glu3)


def _mla_prep_kernel(cq_ref, ckv_ref, kr_ref, qg_ref, kvg_ref, wqa_ref, wqb_ref,
                     ca_ref, sb_ref, ck_ref, sk_ref,
                     q_ref, ckvn_ref, krr_ref, wqa_b, wqb_b):
    @pl.when(pl.program_id(0) == 0)
    def _():
        wqa_b[...] = wqa_ref[...].astype(BF16)
        wqb_b[...] = wqb_ref[...].astype(BF16)

    cq = cq_ref[...]
    cqn = (cq * lax.rsqrt(jnp.mean(cq * cq, axis=-1, keepdims=True) + EPS) * qg_ref[...]).astype(BF16)
    qa = jnp.dot(cqn, wqa_b[...], preferred_element_type=F32)
    qb = jnp.dot(cqn, wqb_b[...], preferred_element_type=F32)
    ca = ca_ref[...]
    sb = sb_ref[...]
    for h in range(N_HEADS):
        cols = slice(h * 128, (h + 1) * 128)
        q_ref[:, cols] = (qa[:, cols] * ca + qb[:, cols] * sb).astype(q_ref.dtype)

    ckv = ckv_ref[...]
    ckvn_ref[...] = ckv * lax.rsqrt(jnp.mean(ckv * ckv, axis=-1, keepdims=True) + EPS) * kvg_ref[...]

    kr = kr_ref[...]
    krr_ref[...] = kr * ck_ref[...] + pltpu.roll(kr, 128 - ROPE_DIM, 1) * sk_ref[...]


def _mla_prep(proj, q_norm_g3, kv_norm_g3, wqa, wqb, tabs, l):
    ca, sb, ck, sk = tabs
    tab_spec = pl.BlockSpec((TB, 128), lambda i: (_rope_blk(i), 0))
    return pl.pallas_call(
        _mla_prep_kernel,
        grid=(NTB,),
        in_specs=[pl.BlockSpec((TB, Q_LORA), lambda i: (i, 2)),
                  pl.BlockSpec((TB, 128), lambda i: (i, 6)),
                  pl.BlockSpec((TB, 128), lambda i: (i, 7)),
                  pl.BlockSpec((None, 1, Q_LORA), lambda i: (l, 0, 0)),
                  pl.BlockSpec((None, 1, KV_LORA), lambda i: (l, 0, 0)),
                  pl.BlockSpec((None, Q_LORA, 1024), lambda i: (l, 0, 0)),
                  pl.BlockSpec((None, Q_LORA, 1024), lambda i: (l, 0, 0)),
                  tab_spec, tab_spec, tab_spec, tab_spec],
        out_specs=[pl.BlockSpec((TB, 1024), lambda i: (i, 0)),
                   pl.BlockSpec((TB, 128), lambda i: (i, 0)),
                   pl.BlockSpec((TB, 128), lambda i: (i, 0))],
        out_shape=(jax.ShapeDtypeStruct((T, 1024), BF16),
                   jax.ShapeDtypeStruct((T, 128), F32),
                   jax.ShapeDtypeStruct((T, 128), F32)),
        scratch_shapes=[pltpu.VMEM((Q_LORA, 1024), BF16), pltpu.VMEM((Q_LORA, 1024), BF16)],
        compiler_params=_cp(("arbitrary",)),
        name="mla_prep",
    )(proj, proj, proj, q_norm_g3, kv_norm_g3, wqa, wqb, ca, sb, ck, sk)


ATT_TQ = 512


def _attn_kernel(q_ref, kvin_ref, wk_ref, wv_ref, wo_ref, o_ref,
                 k_scr, v_scr, o_scr, wk_b, wv_b, wo_b):
    s_id = pl.program_id(0)
    qb = pl.program_id(1)

    @pl.when((s_id == 0) & (qb == 0))
    def _():
        wk_b[...] = wk_ref[...].astype(BF16)
        wv_b[...] = wv_ref[...].astype(BF16)
        wo_b[...] = wo_ref[...].astype(BF16)

    @pl.when(qb == 0)
    def _():
        kv = kvin_ref[...].astype(BF16)
        k_scr[...] = jnp.dot(kv, wk_b[...], preferred_element_type=F32).astype(BF16)
        v_scr[...] = jnp.dot(kv[:, :KV_LORA], wv_b[...], preferred_element_type=F32).astype(BF16)

    scale = (NOPE_DIM + ROPE_DIM) ** -0.5
    for h in range(N_HEADS):
        qh = q_ref[:, h * 128:(h + 1) * 128]
        kh = k_scr[:, h * 128:(h + 1) * 128]
        s = lax.dot_general(qh, kh, (((1,), (1,)), ((), ())), preferred_element_type=F32) * scale
        m = jnp.max(s, axis=-1, keepdims=True)
        p = jnp.exp(s - m)
        den = jnp.sum(p, axis=-1, keepdims=True)
        oh = jnp.dot(p.astype(BF16), v_scr[:, h * V_DIM:(h + 1) * V_DIM], preferred_element_type=F32)
        o_scr[:, h * V_DIM:(h + 1) * V_DIM] = oh / den
    o_ref[...] = jnp.dot(o_scr[...].astype(BF16), wo_b[...], preferred_element_type=F32)


def _attn(q, kvin, wk2, wv, w_o, l, nseq, nqb, tq, row_blk0):
    lk = kvin.shape[1]
    return pl.pallas_call(
        _attn_kernel,
        grid=(nseq, nqb),
        in_specs=[pl.BlockSpec((tq, 1024), lambda s, b: (row_blk0 + s * nqb + b, 0)),
                  pl.BlockSpec((None, lk, 256), lambda s, b: (s, 0, 0)),
                  pl.BlockSpec((None, 256, 1024), lambda s, b: (l, 0, 0)),
                  pl.BlockSpec((None, KV_LORA, 512), lambda s, b: (l, 0, 0)),
                  pl.BlockSpec((None, 512, D_MODEL), lambda s, b: (l, 0, 0))],
        out_specs=pl.BlockSpec((tq, D_MODEL), lambda s, b: (s * nqb + b, 0)),
        out_shape=jax.ShapeDtypeStruct((nseq * nqb * tq, D_MODEL), F32),
        scratch_shapes=[pltpu.VMEM((lk, 1024), BF16), pltpu.VMEM((lk, 512), BF16),
                        pltpu.VMEM((tq, 512), F32),
                        pltpu.VMEM((256, 1024), BF16), pltpu.VMEM((KV_LORA, 512), BF16),
                        pltpu.VMEM((512, D_MODEL), BF16)],
        compiler_params=_cp(("arbitrary", "arbitrary"), VMEM_LIMIT),
        name="attn",
    )(q, kvin, wk2, wv, w_o)


CONV_PAD = 16
CONV_CB = 128
CONV_RB = 128


def _conv_kernel(cin_ref, w_ref, b_ref, lg_ref, lb_ref, pw_ref, pb_ref, o_ref,
                 pad_scr, conv_scr, act_scr, pw_b, *, seq):
    @pl.when(pl.program_id(0) == 0)
    def _():
        pw_b[...] = pw_ref[...].astype(BF16)
        pad_scr[0:CONV_PAD, :] = jnp.zeros((CONV_PAD, D_CONV), F32)
        pad_scr[CONV_PAD + seq:, :] = jnp.zeros((CONV_PAD, D_CONV), F32)

    pad_scr[CONV_PAD:CONV_PAD + seq, :] = cin_ref[:, :D_CONV] * _sigmoid(cin_ref[:, D_CONV:])

    win_rows = CONV_CB + 2 * CONV_PAD
    for c in range(D_CONV // 128):
        lanes = slice(c * 128, (c + 1) * 128)

        def cblk(rb, carry, lanes=lanes):
            base = pl.multiple_of(rb * CONV_CB, CONV_CB)
            win = pad_scr[pl.ds(base, win_rows), lanes]
            acc = jnp.broadcast_to(b_ref[:, lanes], (CONV_CB, 128))
            for b in range(8):
                wb = win if b == 0 else pltpu.roll(win, win_rows - b, 0)
                for a in range(2 * CONV_PAD // 8):
                    k = 8 * a + b - (CONV_PAD - CONV_K // 2)
                    if 0 <= k < CONV_K:
                        acc = acc + w_ref[k:k + 1, lanes] * wb[8 * a:8 * a + CONV_CB]
            conv_scr[pl.ds(base, CONV_CB), lanes] = acc
            return carry

        lax.fori_loop(0, seq // CONV_CB, cblk, 0)

    def blk(rb, carry):
        base = pl.multiple_of(rb * CONV_RB, CONV_RB)
        acc = conv_scr[pl.ds(base, CONV_RB), :]
        mu = jnp.mean(acc, axis=-1, keepdims=True)
        xc = acc - mu
        var = jnp.mean(xc * xc, axis=-1, keepdims=True)
        y = xc * lax.rsqrt(var + EPS) * lg_ref[...] + lb_ref[...]
        act_scr[pl.ds(base, CONV_RB), :] = (y * _sigmoid(y)).astype(BF16)
        return carry

    lax.fori_loop(0, seq // CONV_RB, blk, 0)
    o_ref[...] = jnp.dot(act_scr[...], pw_b[...], preferred_element_type=F32) + pb_ref[...]


def _conv(proj, dw_w, dw_b3, ln_g3, ln_b3, pw, pb3, l, nseq, seq, row_blk0):
    vec = lambda n: pl.BlockSpec((None, 1, n), lambda s: (l, 0, 0))
    return pl.pallas_call(
        functools.partial(_conv_kernel, seq=seq),
        grid=(nseq,),
        in_specs=[pl.BlockSpec((seq, 2 * D_CONV), lambda s: (row_blk0 + s, 1)),
                  pl.BlockSpec((None, CONV_K, D_CONV), lambda s: (l, 0, 0)),
                  vec(D_CONV), vec(D_CONV), vec(D_CONV),
                  pl.BlockSpec((None, D_CONV, D_MODEL), lambda s: (l, 0, 0)),
                  vec(D_MODEL)],
        out_specs=pl.BlockSpec((seq, D_MODEL), lambda s: (s, 0)),
        out_shape=jax.ShapeDtypeStruct((nseq * seq, D_MODEL), F32),
        scratch_shapes=[pltpu.VMEM((seq + 2 * CONV_PAD, D_CONV), F32),
                        pltpu.VMEM((seq, D_CONV), F32),
                        pltpu.VMEM((seq, D_CONV), BF16),
                        pltpu.VMEM((D_CONV, D_MODEL), BF16)],
        compiler_params=_cp(("arbitrary",), VMEM_LIMIT),
        name="conv",
    )(proj, dw_w, dw_b3, ln_g3, ln_b3, pw, pb3)


def _merge_kernel(x_ref, os_ref, omp_ref, oms_ref, ocp_ref, ocs_ref, ga_ref, gb_ref, gc_ref,
                  wo_ref, mod_ref, ng_ref, rw_ref, rb_ref,
                  x1_ref, h2_ref, tw_ref, sc_ref, sr_ref, nb_ref, wo_b, rw_hi, rw_lo):
    i = pl.program_id(0)

    @pl.when(i == 0)
    def _():
        wo_b[...] = wo_ref[...].astype(BF16)
        rw = rw_ref[...]
        rw_hi[...] = rw.astype(BF16)
        rw_lo[...] = (rw - rw_hi[...].astype(F32)).astype(BF16)

    is_prompt = i < NTB_P
    o_mla = jnp.where(is_prompt, omp_ref[...], oms_ref[...])
    o_conv = jnp.where(is_prompt, ocp_ref[...], ocs_ref[...])
    merged = (_sigmoid(ga_ref[...]) * os_ref[...]
              + _sigmoid(gb_ref[...]) * o_mla
              + _sigmoid(gc_ref[...]) * o_conv)
    out = jnp.dot(merged.astype(BF16), wo_b[...], preferred_element_type=F32)
    x1 = x_ref[...] + mod_ref[2:3, :] * out
    x1_ref[...] = x1
    y = x1 * lax.rsqrt(jnp.mean(x1 * x1, axis=-1, keepdims=True) + EPS) * ng_ref[...]
    h2 = y * (1.0 + mod_ref[4:5, :]) + mod_ref[3:4, :]
    h2_hi = h2.astype(BF16)
    h2_ref[...] = h2_hi

    h2_lo = (h2 - h2_hi.astype(F32)).astype(BF16)
    logits = (jnp.dot(h2_hi, rw_hi[...], preferred_element_type=F32)
              + jnp.dot(h2_lo, rw_hi[...], preferred_element_type=F32)
              + jnp.dot(h2_hi, rw_lo[...], preferred_element_type=F32)) + rb_ref[...]
    lane_i = lax.broadcasted_iota(jnp.int32, logits.shape, 1)
    lane = lane_i.astype(F32)
    cur = logits
    vals, idxs = [], []
    for _ in range(TOP_K):
        m = jnp.max(cur, axis=-1, keepdims=True)
        idx = jnp.min(jnp.where(cur == m, lane, 128.0), axis=-1, keepdims=True)
        vals.append(m)
        idxs.append(idx)
        cur = jnp.where(lane == idx, -jnp.inf, cur)
    exps = [jnp.exp(v - vals[0]) for v in vals]
    tot = exps[0] + exps[1] + exps[2] + exps[3]
    tw = jnp.zeros(logits.shape, F32)
    for k in range(TOP_K):
        tw = jnp.where(lane_i == k, exps[k] / tot, tw)
    tw_ref[...] = tw

    ohs = [jnp.where(lane == idxs[k], 1.0, 0.0) for k in range(TOP_K)]
    cnts = [jnp.sum(oh, axis=0, keepdims=True) for oh in ohs]
    n_b = jnp.ceil((cnts[0] + cnts[1] + cnts[2] + cnts[3]) * (1.0 / SEG_ALIGN)) * SEG_ALIGN
    inc = jnp.broadcast_to(n_b, (8, 128))
    lane8 = lax.broadcasted_iota(jnp.int32, (8, 128), 1)
    for s in (1, 2, 4, 8, 16, 32, 64):
        inc = inc + jnp.where(lane8 >= s, pltpu.roll(inc, s, 1), 0.0)
    before = (inc - n_b)[0:1, :]
    r_i = lax.broadcasted_iota(jnp.int32, (TB, TB), 0)
    c_i = lax.broadcasted_iota(jnp.int32, (TB, TB), 1)
    earlier = jnp.where(r_i > c_i, 1.0, 0.0).astype(BF16)
    slots = jnp.zeros(logits.shape, F32)
    for k in range(TOP_K):
        pre = jnp.dot(earlier, ohs[k].astype(BF16), preferred_element_type=F32)
        slot_k = jnp.sum(ohs[k] * (pre + before), axis=-1, keepdims=True)
        slots = jnp.where(lane_i == k, slot_k, slots)
        before = before + cnts[k]
    sc_ref[...] = slots.astype(jnp.int32)
    sr_ref[...] = slots.T[0:8, :].astype(jnp.int32)
    nb_ref[...] = jnp.broadcast_to(n_b, (8, 128)).astype(jnp.int32)


def _merge(x, o_ssm, o_mla_p, o_mla_s, o_conv_p, o_conv_s, proj, w_out, modt, norm_ffn_g3, rw_p, rb_p, l):
    row = lambda i: (i, 0)
    row_p = lambda i: (jnp.minimum(i, NTB_P - 1), 0)
    row_s = lambda i: (jnp.maximum(i - NTB_P, 0), 0)
    blk = lambda m: pl.BlockSpec((TB, D_MODEL), m)
    return pl.pallas_call(
        _merge_kernel,
        grid=(NTB,),
        in_specs=[blk(row), blk(row), blk(row_p), blk(row_s), blk(row_p), blk(row_s),
                  blk(lambda i: (i, 2)), blk(lambda i: (i, 3)), blk(lambda i: (i, 4)),
                  pl.BlockSpec((None, D_MODEL, D_MODEL), lambda i: (l, 0, 0)),
                  pl.BlockSpec((None, 6, D_MODEL), lambda i: (_mod_row(i), 0, 0)),
                  pl.BlockSpec((None, 1, D_MODEL), lambda i: (l, 0, 0)),
                  pl.BlockSpec((None, D_MODEL, 128), lambda i: (l, 0, 0)),
                  pl.BlockSpec((None, 1, 128), lambda i: (l, 0, 0))],
        out_specs=[blk(row), blk(row), pl.BlockSpec((TB, 128), row), pl.BlockSpec((TB, 128), row),
                   pl.BlockSpec((None, 8, TB), lambda i: (i, 0, 0)),
                   pl.BlockSpec((None, 8, 128), lambda i: (i, 0, 0))],
        out_shape=(jax.ShapeDtypeStruct((T, D_MODEL), F32), jax.ShapeDtypeStruct((T, D_MODEL), BF16),
                   jax.ShapeDtypeStruct((T, 128), F32), jax.ShapeDtypeStruct((T, 128), jnp.int32),
                   jax.ShapeDtypeStruct((NTB, 8, TB), jnp.int32),
                   jax.ShapeDtypeStruct((NTB, 8, 128), jnp.int32)),
        scratch_shapes=[pltpu.VMEM((D_MODEL, D_MODEL), BF16),
                        pltpu.VMEM((D_MODEL, 128), BF16), pltpu.VMEM((D_MODEL, 128), BF16)],
        compiler_params=_cp(("arbitrary",), VMEM_LIMIT),
        name="merge",
    )(x, o_ssm, o_mla_p, o_mla_s, o_conv_p, o_conv_s, proj, proj, proj, w_out, modt, norm_ffn_g3, rw_p, rb_p)


PAIRS = TB * TOP_K
G_ROWS = -(-(PAIRS + (SEG_ALIGN - 1) * N_EXPERTS) // 256) * 256
SEG_BITS = 9


def _aligned(off):
    return off if isinstance(off, int) else pl.multiple_of(off, SEG_ALIGN)


def _segment_copies(src_ref, dst_ref, sem, n, src_off, dst_off):
    copies = []
    done = 0
    for bit in range(SEG_BITS - 1, SEG_ALIGN.bit_length() - 2, -1):
        size = 1 << bit
        part = n & size
        copies.append((part != 0,
                       pltpu.make_async_copy(src_ref.at[pl.ds(_aligned(src_off + done), size)],
                                             dst_ref.at[pl.ds(_aligned(dst_off + done), size)], sem)))
        done = done + part
    return copies


def _start_all(copies):
    for pred, cp in copies:
        pl.when(pred)(cp.start)


def _wait_all(copies):
    for pred, cp in copies:
        pl.when(pred)(cp.wait)


def _dispatch_kernel(seg_ref, nbe_ref, cnt_ref, pst_ref, nu_ref, h_ref, sr_ref, xs_ref, g_scr, z_scr, sem):
    b = pl.program_id(0)

    @pl.when(b < NTB)
    def _():
        sub = lax.broadcasted_iota(jnp.int32, (G_ROWS, TB), 0)
        sr = sr_ref[...]
        hit = sub == sr[0:1, :]
        for k in range(1, TOP_K):
            hit = hit | (sub == sr[k:k + 1, :])
        g_scr[...] = jnp.dot(jnp.where(hit, 1.0, 0.0).astype(BF16), h_ref[...], preferred_element_type=F32)

        def segs(e):
            return nbe_ref[b * N_EXPERTS + e], seg_ref[b * N_EXPERTS + e]

        def start(e, off):
            n, dst = segs(e)
            _start_all(_segment_copies(g_scr, xs_ref, sem, n, off, dst))
            return off + n

        def wait(e, off):
            n, dst = segs(e)
            _wait_all(_segment_copies(g_scr, xs_ref, sem, n, off, dst))
            return off + n

        lax.fori_loop(0, N_EXPERTS, start, 0)
        lax.fori_loop(0, N_EXPERTS, wait, 0)

    @pl.when(b == NTB)
    def _():
        z_scr[...] = jnp.zeros_like(z_scr)

        def pad(e):
            cnt = cnt_ref[e]
            padded = (cnt + MOE_BM - 1) // MOE_BM * MOE_BM
            return padded - cnt, pst_ref[e] + cnt

        def start(e, carry):
            n, dst = pad(e)
            _start_all(_segment_copies(z_scr, xs_ref, sem, n, 0, dst))
            return carry

        def wait(e, carry):
            n, dst = pad(e)
            _wait_all(_segment_copies(z_scr, xs_ref, sem, n, 0, dst))
            return carry

        lax.fori_loop(0, N_EXPERTS, start, 0)
        lax.fori_loop(0, N_EXPERTS, wait, 0)

        def tail_copy(blk):
            return pltpu.make_async_copy(z_scr, xs_ref.at[pl.ds(blk * MOE_BM, MOE_BM)], sem)

        def tail_start(blk, carry):
            tail_copy(blk).start()
            return carry

        def tail_wait(blk, carry):
            tail_copy(blk).wait()
            return carry

        lax.fori_loop(nu_ref[0], MOE_NB, tail_start, 0)
        lax.fori_loop(nu_ref[0], MOE_NB, tail_wait, 0)


def _dispatch(seg_start, n_be, counts, pad_start, n_used, h2, slot_row):
    last = lambda b, *_: (jnp.minimum(b, NTB - 1), 0)
    grid_spec = pltpu.PrefetchScalarGridSpec(
        num_scalar_prefetch=5,
        grid=(NTB + 1,),
        in_specs=[pl.BlockSpec((TB, D_MODEL), last),
                  pl.BlockSpec((None, 8, TB), lambda b, *_: (jnp.minimum(b, NTB - 1), 0, 0))],
        out_specs=pl.BlockSpec(memory_space=pl.ANY),
        scratch_shapes=[pltpu.VMEM((G_ROWS, D_MODEL), F32),
                        pltpu.VMEM((MOE_BM, D_MODEL), F32),
                        pltpu.SemaphoreType.DMA(())],
    )
    return pl.pallas_call(
        _dispatch_kernel,
        grid_spec=grid_spec,
        out_shape=jax.ShapeDtypeStruct((MOE_ROWS, D_MODEL), F32),
        compiler_params=_cp(("arbitrary",), VMEM_LIMIT),
        name="moe_dispatch",
    )(seg_start, n_be, counts, pad_start, n_used, h2, slot_row)


def _expert_kernel(be_ref, nu_ref, x_ref, w1_ref, b1_ref, w2_ref, b2_ref, y_ref, w1_b, w2_b):
    i = pl.program_id(0)
    prev = be_ref[jnp.maximum(i - 1, 0)]

    @pl.when((i == 0) | (be_ref[i] != prev))
    def _():
        w1_b[...] = w1_ref[...].astype(BF16)
        w2_b[...] = w2_ref[...].astype(BF16)

    @pl.when(i < nu_ref[0])
    def _():
        gu = jnp.dot(x_ref[...].astype(BF16), w1_b[...], preferred_element_type=F32) + b1_ref[...]
        g = jnp.minimum(gu[:, :D_FF], SWIGLU_LIMIT)
        u = jnp.clip(gu[:, D_FF:], -SWIGLU_LIMIT, SWIGLU_LIMIT)
        act = (u + 1.0) * (g * _sigmoid(SWIGLU_ALPHA * g))
        y_ref[...] = jnp.dot(act.astype(BF16), w2_b[...], preferred_element_type=F32) + b2_ref[...]

    @pl.when(i >= nu_ref[0])
    def _():
        y_ref[...] = jnp.zeros_like(y_ref)


def _experts(block_e, n_used, xs, w1, b1_4, w2, b2_4, l):
    grid_spec = pltpu.PrefetchScalarGridSpec(
        num_scalar_prefetch=2,
        grid=(MOE_NB,),
        in_specs=[pl.BlockSpec((MOE_BM, D_MODEL), lambda i, be, nu: (jnp.minimum(i, nu[0] - 1), 0)),
                  pl.BlockSpec((None, None, D_MODEL, 2 * D_FF), lambda i, be, nu: (l, be[i], 0, 0)),
                  pl.BlockSpec((None, None, 1, 2 * D_FF), lambda i, be, nu: (l, be[i], 0, 0)),
                  pl.BlockSpec((None, None, D_FF, D_MODEL), lambda i, be, nu: (l, be[i], 0, 0)),
                  pl.BlockSpec((None, None, 1, D_MODEL), lambda i, be, nu: (l, be[i], 0, 0))],
        out_specs=pl.BlockSpec((MOE_BM, D_MODEL), lambda i, be, nu: (i, 0)),
        scratch_shapes=[pltpu.VMEM((D_MODEL, 2 * D_FF), BF16), pltpu.VMEM((D_FF, D_MODEL), BF16)],
    )
    return pl.pallas_call(
        _expert_kernel,
        grid_spec=grid_spec,
        out_shape=jax.ShapeDtypeStruct((MOE_ROWS, D_MODEL), F32),
        compiler_params=_cp(("arbitrary",), VMEM_LIMIT),
        name="moe_experts",
    )(block_e, n_used, xs, w1, b1_4, w2, b2_4)


def _combine_kernel(seg_ref, nbe_ref, x1_ref, tw_ref, sc_ref, mod_ref, fg_ref, ys_ref, *rest, final):
    if final:
        x2_ref, yn_ref, g_scr, sem = rest
    else:
        x2_ref, g_scr, sem = rest
    b = pl.program_id(0)

    def copies(blk, e, off):
        n = nbe_ref[blk * N_EXPERTS + e]
        src = seg_ref[blk * N_EXPERTS + e]
        slot = blk % 2
        return n, _segment_copies(ys_ref, g_scr.at[slot], sem.at[slot], n, src, off)

    def fetch(blk):
        g_scr[blk % 2, PAIRS:, :] = jnp.zeros((G_ROWS - PAIRS, D_MODEL), F32)

        def start(e, off):
            n, cps = copies(blk, e, off)
            _start_all(cps)
            return off + n

        lax.fori_loop(0, N_EXPERTS, start, 0)

    @pl.when(b == 0)
    def _():
        fetch(b)

    @pl.when(b + 1 < NTB)
    def _():
        fetch(b + 1)

    def wait(e, off):
        n, cps = copies(b, e, off)
        _wait_all(cps)
        return off + n

    lax.fori_loop(0, N_EXPERTS, wait, 0)

    sc = sc_ref[...]
    tw = tw_ref[...]
    lane = lax.broadcasted_iota(jnp.int32, (TB, G_ROWS), 1)
    pw = jnp.zeros((TB, G_ROWS), F32)
    for k in range(TOP_K):
        pw = pw + jnp.where(lane == sc[:, k:k + 1], tw[:, k:k + 1], 0.0)
    moe = jnp.dot(pw.astype(BF16), g_scr[b % 2].astype(BF16), preferred_element_type=F32)
    x2 = x1_ref[...] + mod_ref[5:6, :] * moe
    x2_ref[...] = x2
    if final:
        yn_ref[...] = x2 * lax.rsqrt(jnp.mean(x2 * x2, axis=-1, keepdims=True) + EPS) * fg_ref[...]


def _combine(seg_start, n_be, x1, topw, slot_col, modt, final_g2, ys, final):
    row = lambda b, *_: (b, 0)
    out_blk = pl.BlockSpec((TB, D_MODEL), row)
    out_sds = jax.ShapeDtypeStruct((T, D_MODEL), F32)
    grid_spec = pltpu.PrefetchScalarGridSpec(
        num_scalar_prefetch=2,
        grid=(NTB,),
        in_specs=[pl.BlockSpec((TB, D_MODEL), row),
                  pl.BlockSpec((TB, 128), row),
                  pl.BlockSpec((TB, 128), row),
                  pl.BlockSpec((None, 6, D_MODEL), lambda b, *_: (_mod_row(b), 0, 0)),
                  pl.BlockSpec((1, D_MODEL), lambda b, *_: (0, 0)),
                  pl.BlockSpec(memory_space=pl.ANY)],
        out_specs=[out_blk, out_blk] if final else [out_blk],
        scratch_shapes=[pltpu.VMEM((2, G_ROWS, D_MODEL), F32), pltpu.SemaphoreType.DMA((2,))],
    )
    return pl.pallas_call(
        functools.partial(_combine_kernel, final=final),
        grid_spec=grid_spec,
        out_shape=(out_sds, out_sds) if final else (out_sds,),
        compiler_params=_cp(("arbitrary",), VMEM_LIMIT),
        name="moe_combine",
    )(seg_start, n_be, x1, topw, slot_col, modt, final_g2, ys)


def _rot_cols(w):
    q = ROPE_DIM // 4
    parts = []
    for half in range(2):
        a = w[..., half * 2 * q:half * 2 * q + q]
        b = w[..., half * 2 * q + q:(half + 1) * 2 * q]
        parts += [-b, a]
    return jnp.concatenate(parts, axis=-1)


def _rope_tables():
    rows = DEC_SEQ // GRID_W
    row = jnp.repeat(jnp.arange(rows), GRID_W).astype(F32)
    col = jnp.tile(jnp.arange(GRID_W), rows).astype(F32)
    half = ROPE_DIM // 2
    freqs = ROPE_BASE ** (-jnp.arange(0, half, 2, dtype=F32) / half)
    ang_r = row[:, None] * freqs
    ang_c = col[:, None] * freqs
    ang = jnp.concatenate([ang_r, ang_r, ang_c, ang_c], axis=-1)
    cos = jnp.concatenate([jnp.cos(ang), jnp.ones((TB, ROPE_DIM), F32)], axis=0)
    sin = jnp.concatenate([jnp.sin(ang), jnp.zeros((TB, ROPE_DIM), F32)], axis=0)
    n = cos.shape[0]
    z = lambda w: jnp.zeros((n, w), F32)
    ca = jnp.concatenate([jnp.ones((n, NOPE_DIM), F32), cos, z(32)], axis=1)
    sb = jnp.concatenate([z(NOPE_DIM), sin, z(32)], axis=1)
    ck = jnp.concatenate([cos, z(96)], axis=1)
    sk = jnp.concatenate([sin, z(96)], axis=1)
    return ca, sb, ck, sk


def kernel(x_prompt, x_sample, c, cache_kv_latent, cache_k_rope, state_ssm, c_ctx, norm_mix_g, norm_ffn_g, final_norm_g, w_ada, b_ada, w_in, q_norm_g, w_uq, kv_norm_g, w_ukv, w_o_mla, ssm_lam_re, ssm_lam_im, ssm_log_dt, ssm_b_re, ssm_b_im, ssm_c_re, ssm_c_im, ssm_d, w_ssm_glu, b_ssm_glu, conv_dw_w, conv_dw_b, conv_ln_g, conv_ln_b, w_conv_pw2, b_conv_pw2, w_out, router_w, router_b, moe_w1, moe_b1, moe_w2, moe_b2):
    L = DEPTH
    v3 = lambda a: a.reshape(L, 1, a.shape[-1])

    x = jnp.concatenate([x_prompt.reshape(T_P, D_MODEL), x_sample.reshape(T_S, D_MODEL)], axis=0)
    cc = jnp.concatenate([c, c_ctx[None, :], jnp.zeros((16 - DEC_BATCH - 1, D_MODEL), F32)], axis=0)

    kr0 = D_SSM + Q_LORA + KV_LORA
    w_kr = w_in[:, :, kr0:kr0 + ROPE_DIM]
    w_in_p = jnp.concatenate([w_in[:, :, :kr0], w_kr, _rot_cols(w_kr), jnp.zeros((L, D_MODEL, 64), F32),
                              w_in[:, :, kr0 + ROPE_DIM:]], axis=-1)

    wq = w_uq.reshape(L, Q_LORA, N_HEADS, NOPE_DIM + ROPE_DIM)
    zq = lambda w: jnp.zeros((L, Q_LORA, N_HEADS, w), F32)
    wqa = jnp.concatenate([wq, zq(32)], axis=-1).reshape(L, Q_LORA, 1024)
    wqb = jnp.concatenate([zq(NOPE_DIM), _rot_cols(wq[..., NOPE_DIM:]), zq(32)], axis=-1).reshape(L, Q_LORA, 1024)
    wkv = w_ukv.reshape(L, KV_LORA, N_HEADS, NOPE_DIM + V_DIM)
    wk_lat = jnp.concatenate([wkv[..., :NOPE_DIM], jnp.zeros((L, KV_LORA, N_HEADS, 64), F32)], axis=-1)
    place = jnp.concatenate([jnp.zeros((ROPE_DIM, NOPE_DIM), F32), jnp.eye(ROPE_DIM, dtype=F32),
                             jnp.zeros((ROPE_DIM, 32), F32)], axis=-1)
    wk_rope = jnp.broadcast_to(place[None, :, None, :], (L, ROPE_DIM, N_HEADS, 128))
    wk2 = jnp.concatenate([wk_lat, wk_rope, jnp.zeros((L, 96, N_HEADS, 128), F32)], axis=1).reshape(L, 256, 1024)
    wv = wkv[..., NOPE_DIM:].reshape(L, KV_LORA, N_HEADS * V_DIM)
    tabs = _rope_tables()

    a_re, a_im, f_re, f_im = [v.reshape(L, 2, N_SSM_GROUPS, N_STATE)
                              for v in _s5_disc(ssm_lam_re, ssm_lam_im, ssm_log_dt)]
    bb_re = f_re[..., None] * ssm_b_re - f_im[..., None] * ssm_b_im
    bb_im = f_re[..., None] * ssm_b_im + f_im[..., None] * ssm_b_re
    eye16 = jnp.eye(16, dtype=F32)
    eye8 = jnp.eye(8, dtype=F32)

    def bd_in(bb):
        bb = bb.reshape(L, 2, 2, 16, N_STATE, SSM_GROUP)
        return jnp.einsum('ldhgpc,gk->ldhgckp', bb, eye16).reshape(L, 2, 2, 256, 1024)

    def bd_out(cm):
        cm = cm.reshape(L, 2, 4, 8, SSM_GROUP, N_STATE)
        return jnp.einsum('ldqgcp,gk->ldqgpkc', cm, eye8).reshape(L, 2, 4, 512, 128)

    wb = jnp.stack([bd_in(bb_re), bd_in(bb_im)], axis=2).astype(BF16)
    wc = jnp.stack([bd_out(ssm_c_re), bd_out(-ssm_c_im)], axis=2).astype(BF16)
    a_re4 = a_re.reshape(L, 2, 1, SSM_STATE)
    a_im4 = a_im.reshape(L, 2, 1, SSM_STATE)

    rw_p = jnp.concatenate([router_w, jnp.zeros((L, D_MODEL, 128 - N_EXPERTS), F32)], axis=-1)
    rb_p = jnp.concatenate([router_b, jnp.full((L, 128 - N_EXPERTS), -1e30, F32)], axis=-1).reshape(L, 1, 128)

    kv_out, kr_out, ssm_out = [], [], []
    y_norm = None
    for l in range(L):
        modt = _ada(cc, w_ada, v3(b_ada), l).reshape(16, 6, D_MODEL)
        h = _norm1(x, v3(norm_mix_g), modt, l)
        proj = _mm(h, w_in_p, l, tm=1024, tn=1024, name="w_in")

        u_p = proj[:T_P, :D_SSM].reshape(4, 8, SEQ, D_SSM).transpose(0, 2, 1, 3)
        u_s = proj[T_P:, :D_SSM].reshape(DEC_BATCH, 4, S5_CHUNK, D_SSM).transpose(1, 2, 0, 3)
        u_all = jnp.concatenate([u_p, u_s], axis=0)
        st = state_ssm[:, l]
        h0 = st.transpose(4, 1, 0, 2, 3).reshape(2, 2, DEC_BATCH, SSM_STATE)
        y_f, y_r, fin = _s5(u_all, wb, wc, a_re4, a_im4, h0[0], h0[1], l)
        ysum = y_f + y_r
        o_ssm = _ssm_glu(proj, ysum[:4].transpose(0, 2, 1, 3).reshape(T_P, D_SSM),
                         ysum[4:].transpose(2, 0, 1, 3).reshape(T_S, D_SSM),
                         v3(ssm_d), w_ssm_glu, v3(b_ssm_glu), l)
        ssm_out.append(fin.reshape(4, 2, 2, 8, N_SSM_GROUPS, N_STATE).transpose(0, 3, 1, 4, 5, 2)
                       .reshape(BATCH, 2, N_SSM_GROUPS, N_STATE, 2))

        q, ckvn, krr = _mla_prep(proj, v3(q_norm_g), v3(kv_norm_g), wqa, wqb, tabs, l)
        kvin_p = jnp.concatenate([ckvn[:T_P], krr[:T_P]], axis=-1).reshape(BATCH, SEQ, 256)
        cache_kr = jnp.concatenate([cache_k_rope[:, l], jnp.zeros((DEC_BATCH, PAST_LEN, 96), F32)], axis=-1)
        kvin_s = jnp.concatenate([
            jnp.concatenate([cache_kv_latent[:, l], cache_kr], axis=-1),
            jnp.concatenate([ckvn[T_P:], krr[T_P:]], axis=-1).reshape(DEC_BATCH, DEC_SEQ, 256)], axis=1)
        o_mla_p = _attn(q, kvin_p, wk2, wv, w_o_mla, l, BATCH, 1, SEQ, 0)
        o_mla_s = _attn(q, kvin_s, wk2, wv, w_o_mla, l, DEC_BATCH, DEC_SEQ // ATT_TQ, ATT_TQ, T_P // ATT_TQ)
        kv_out.append(ckvn[:T_P].reshape(BATCH, SEQ, KV_LORA))
        kr_out.append(proj[:T_P, 896:896 + ROPE_DIM].reshape(BATCH, SEQ, ROPE_DIM))

        conv_args = (conv_dw_w, v3(conv_dw_b), v3(conv_ln_g), v3(conv_ln_b), w_conv_pw2, v3(b_conv_pw2), l)
        o_conv_p = _conv(proj, *conv_args, BATCH, SEQ, 0)
        o_conv_s = _conv(proj, *conv_args, DEC_BATCH, DEC_SEQ, T_P // DEC_SEQ)

        x1, h2, topw, slot_col, slot_row, n_blk = _merge(
            x, o_ssm, o_mla_p, o_mla_s, o_conv_p, o_conv_s, proj, w_out, modt, v3(norm_ffn_g), rw_p, rb_p, l)

        n_be = n_blk[:, 0, :N_EXPERTS]
        counts = jnp.sum(n_be, axis=0)
        padded = (counts + MOE_BM - 1) // MOE_BM * MOE_BM
        pad_end = jnp.cumsum(padded)
        pad_start = pad_end - padded
        seg_start = (pad_start[None, :] + jnp.cumsum(n_be, axis=0) - n_be).reshape(-1).astype(jnp.int32)
        blk_row = jnp.arange(MOE_NB, dtype=jnp.int32) * MOE_BM
        block_e = jnp.minimum(jnp.sum((pad_end[None, :] <= blk_row[:, None]).astype(jnp.int32), axis=1),
                              N_EXPERTS - 1).astype(jnp.int32)
        n_used = (pad_end[-1:] // MOE_BM).astype(jnp.int32)
        n_be = n_be.reshape(-1)

        xs = _dispatch(seg_start, n_be, counts.astype(jnp.int32), pad_start.astype(jnp.int32), n_used, h2, slot_row)
        ys = _experts(block_e, n_used, xs, moe_w1, moe_b1.reshape(L, N_EXPERTS, 1, 2 * D_FF),
                      moe_w2, moe_b2.reshape(L, N_EXPERTS, 1, D_MODEL), l)
        outs = _combine(seg_start, n_be, x1, topw, slot_col, modt, final_norm_g.reshape(1, D_MODEL), ys,
                        final=(l == L - 1))
        x = outs[0]
        if l == L - 1:
            y_norm = outs[1]

    y_prompt = y_norm[:T_P].reshape(BATCH, SEQ, D_MODEL)
    y_sample = y_norm[T_P:].reshape(DEC_BATCH, DEC_SEQ, D_MODEL)
    return (y_prompt, y_sample, jnp.stack(kv_out, axis=1), jnp.stack(kr_out, axis=1), jnp.stack(ssm_out, axis=1))
```

```python
import functools
import math

import numpy as np
import jax
import jax.numpy as jnp
from jax import lax
from jax.experimental import pallas as pl
from jax.experimental.pallas import tpu as pltpu

F32 = jnp.float32
BF16 = jnp.bfloat16

D_MODEL = 1024
BATCH = 32
SEQ = 256
DEPTH = 2
DEC_BATCH = 8
DEC_SEQ = 1024
PAST_LEN = 256
GRID_W = 64
EPS = 1e-6
D_SSM = 512
SSM_GROUP = 16
N_SSM_GROUPS = 32
N_STATE = 64
N_HEADS = 8
NOPE_DIM = 64
ROPE_DIM = 32
V_DIM = 64
Q_LORA = 256
KV_LORA = 128
ROPE_BASE = 10000.0
D_CONV = 512
CONV_K = 31
N_EXPERTS = 32
TOP_K = 4
D_FF = 1024
SWIGLU_ALPHA = 1.702
SWIGLU_LIMIT = 7.0

T_P = BATCH * SEQ
T_S = DEC_BATCH * DEC_SEQ
T = T_P + T_S
TB = 256
NTB = T // TB
NTB_P = T_P // TB
TB_PER_S = DEC_SEQ // TB
SSM_STATE = N_SSM_GROUPS * N_STATE
PROJ_W = 5120
MOE_BM = 512
SEG_ALIGN = 8
MOE_NB = (T * TOP_K + (SEG_ALIGN - 1) * NTB * N_EXPERTS) // MOE_BM + N_EXPERTS
MOE_ROWS = MOE_NB * MOE_BM
VMEM_LIMIT = 56 * 1024 * 1024


def _cp(sem, vmem=None):
    return pltpu.CompilerParams(dimension_semantics=sem, vmem_limit_bytes=vmem)


def _sigmoid(x):
    return 0.5 * jnp.tanh(0.5 * x) + 0.5


def _mod_row(i):
    return jnp.where(i < NTB_P, DEC_BATCH, (i - NTB_P) // TB_PER_S)


def _rope_blk(i):
    return jnp.where(i < NTB_P, TB_PER_S, (i - NTB_P) % TB_PER_S)


def _ada_kernel(c_ref, w_ref, b_ref, o_ref):
    c = c_ref[...]
    s = c * jax.nn.sigmoid(c)
    o_ref[...] = jnp.dot(s, w_ref[...], preferred_element_type=F32,
                         precision=lax.Precision.HIGHEST) + b_ref[...]


def _ada(cc, w_ada, b_ada3, l):
    tn = 1024
    return pl.pallas_call(
        _ada_kernel,
        grid=(6 * D_MODEL // tn,),
        in_specs=[pl.BlockSpec((16, D_MODEL), lambda j: (0, 0)),
                  pl.BlockSpec((None, D_MODEL, tn), lambda j: (l, 0, j)),
                  pl.BlockSpec((None, 1, tn), lambda j: (l, 0, j))],
        out_specs=pl.BlockSpec((16, tn), lambda j: (0, j)),
        out_shape=jax.ShapeDtypeStruct((16, 6 * D_MODEL), F32),
        compiler_params=_cp(("arbitrary",)),
        name="ada",
    )(cc, w_ada, b_ada3)


WIN_TM = 1024
WIN_TN = 1024
KR0 = D_SSM + Q_LORA + KV_LORA
IN_W = KR0 + ROPE_DIM + 2 * D_CONV + 3 * D_MODEL
PROJ_SHIFT = PROJ_W - IN_W


def _win_kernel(xa_ref, xb_ref, g_ref, mod_ref, wp_ref, wc_ref, o_ref, wb_ref, *, n_a):
    j = pl.program_id(0)
    i = pl.program_id(1)

    @pl.when((i == 0) & (j == 0))
    def _():
        w = wc_ref[...]
        kr = w[:, KR0:KR0 + ROPE_DIM]
        wb_ref[...] = jnp.concatenate(
            [w[:, :KR0 + ROPE_DIM], _rot_cols(kr), jnp.zeros((D_MODEL, PROJ_SHIFT - ROPE_DIM), F32)],
            axis=1).astype(BF16)

    @pl.when((i == 0) & (j > 0))
    def _():
        wb_ref[...] = jnp.concatenate([wp_ref[:, WIN_TN - PROJ_SHIFT:], wc_ref[:, :WIN_TN - PROJ_SHIFT]],
                                      axis=1).astype(BF16)

    x = jnp.where(i < n_a, xa_ref[...], xb_ref[...])
    y = x * lax.rsqrt(jnp.mean(x * x, axis=-1, keepdims=True) + EPS) * g_ref[...]
    h = (y * (1.0 + mod_ref[1:2, :]) + mod_ref[0:1, :]).astype(BF16)
    o_ref[...] = jnp.dot(h, wb_ref[...], preferred_element_type=F32)


def _win(xa, xb, g3, modt, w_in, l):
    n_a = xa.shape[0] // WIN_TM if xb is not xa else T // WIN_TM
    ni = T // WIN_TM
    nbp = T_P // WIN_TM
    return pl.pallas_call(
        functools.partial(_win_kernel, n_a=n_a),
        grid=(PROJ_W // WIN_TN, ni),
        in_specs=[pl.BlockSpec((WIN_TM, D_MODEL), lambda j, i: (jnp.minimum(i, n_a - 1), 0)),
                  pl.BlockSpec((WIN_TM, D_MODEL), lambda j, i: (jnp.maximum(i - n_a, 0), 0)),
                  pl.BlockSpec((None, 1, D_MODEL), lambda j, i: (l, 0, 0)),
                  pl.BlockSpec((None, 6, D_MODEL),
                               lambda j, i: (jnp.where(i < nbp, DEC_BATCH, i - nbp), 0, 0)),
                  pl.BlockSpec((None, D_MODEL, WIN_TN), lambda j, i: (l, 0, jnp.maximum(j - 1, 0))),
                  pl.BlockSpec((None, D_MODEL, WIN_TN), lambda j, i: (l, 0, j))],
        out_specs=pl.BlockSpec((WIN_TM, WIN_TN), lambda j, i: (i, j)),
        out_shape=jax.ShapeDtypeStruct((T, PROJ_W), F32),
        scratch_shapes=[pltpu.VMEM((D_MODEL, WIN_TN), BF16)],
        compiler_params=_cp(("arbitrary", "arbitrary"), VMEM_LIMIT),
        name="w_in",
    )(xa, xb, g3, modt, w_in, w_in)


def _s5_disc_kernel(lr_ref, li_ref, ldt_ref, ar_ref, ai_ref, fr_ref, fi_ref):
    dt = jnp.exp(ldt_ref[...])
    lr = lr_ref[...]
    li = li_ref[...]
    mag = jnp.exp(lr * dt)
    a_re = mag * jnp.cos(li * dt)
    a_im = mag * jnp.sin(li * dt)
    den = lr * lr + li * li
    ar_ref[...] = a_re
    ai_ref[...] = a_im
    fr_ref[...] = ((a_re - 1.0) * lr + a_im * li) / den
    fi_ref[...] = (a_im * lr - (a_re - 1.0) * li) / den


def _s5_disc(lam_re, lam_im, log_dt):
    rows = DEPTH * 2 * N_SSM_GROUPS
    shp = jax.ShapeDtypeStruct((rows, N_STATE), F32)
    return pl.pallas_call(
        _s5_disc_kernel,
        out_shape=(shp, shp, shp, shp),
        name="s5_disc",
    )(lam_re.reshape(rows, N_STATE), lam_im.reshape(rows, N_STATE), log_dt.reshape(rows, 1))


S5_CHUNK = 256
S5_SUB = 64
S5_NSUB = S5_CHUNK // S5_SUB
S5_NCHUNK = 8


def _s5_kernel(uf_ref, ur_ref, wb_ref, wc_ref, are_ref, aim_ref, h0r_ref, h0i_ref,
               yf_ref, yr_ref, fin_ref,
               hre_ref, him_ref, ab_re, ab_im, xre_ref, xim_ref):
    k = pl.program_id(0)

    @pl.when(k < 4)
    def _():
        hre_ref[...] = jnp.zeros_like(hre_ref)
        him_ref[...] = jnp.zeros_like(him_ref)

    @pl.when(k == 4)
    def _():
        hre_ref[...] = h0r_ref[...]
        him_ref[...] = h0i_ref[...]

    for d, (u_ref, y_ref) in enumerate(((uf_ref, yf_ref), (ur_ref, yr_ref))):
        reverse = d == 1
        ab_re[...] = jnp.broadcast_to(are_ref[d], (8, SSM_STATE))
        ab_im[...] = jnp.broadcast_to(aim_ref[d], (8, SSM_STATE))

        def sub(s, carry, d=d, reverse=reverse, u_ref=u_ref, y_ref=y_ref):
            ss = (S5_NSUB - 1 - s) if reverse else s
            rows = pl.ds(pl.multiple_of(ss * S5_SUB, S5_SUB), S5_SUB)
            u2 = u_ref[rows].reshape(S5_SUB * 8, D_SSM).astype(BF16)
            for half in range(2):
                uh = u2[:, half * 256:(half + 1) * 256]
                cols = slice(half * 1024, (half + 1) * 1024)
                xre_ref[:, cols] = jnp.dot(uh, wb_ref[d, 0, half], preferred_element_type=F32)
                xim_ref[:, cols] = jnp.dot(uh, wb_ref[d, 1, half], preferred_element_type=F32)

            def step(t, h, reverse=reverse):
                tt = (S5_SUB - 1 - t) if reverse else t
                r = pl.ds(pl.multiple_of(tt * 8, 8), 8)
                h_re, h_im = h
                a_re = ab_re[...]
                a_im = ab_im[...]
                n_re = a_re * h_re - a_im * h_im + xre_ref[r, :]
                n_im = a_re * h_im + a_im * h_re + xim_ref[r, :]
                xre_ref[r, :] = n_re
                xim_ref[r, :] = n_im
                return n_re, n_im

            h_re, h_im = lax.fori_loop(0, S5_SUB, step, (hre_ref[d], him_ref[d]))
            hre_ref[d] = h_re
            him_ref[d] = h_im

            @pl.when((s == 0) & (k < 4))
            def _(d=d, reverse=reverse):
                r0 = (S5_SUB - 1) * 8 if reverse else 0
                fin_ref[d, 0] = xre_ref[r0:r0 + 8, :]
                fin_ref[d, 1] = xim_ref[r0:r0 + 8, :]

            x_re = xre_ref[...].astype(BF16)
            x_im = xim_ref[...].astype(BF16)
            for q in range(4):
                ks = slice(q * 512, (q + 1) * 512)
                yq = (jnp.dot(x_re[:, ks], wc_ref[d, 0, q], preferred_element_type=F32)
                      + jnp.dot(x_im[:, ks], wc_ref[d, 1, q], preferred_element_type=F32))
                y_ref[rows, :, q * 128:(q + 1) * 128] = yq.reshape(S5_SUB, 8, 128)
            return carry

        lax.fori_loop(0, S5_NSUB, sub, 0)


def _s5(u_all, wb, wc, a_re, a_im, h0_re, h0_im, l):
    def rev_chunk(k):
        return jnp.where(k < 4, k, 11 - k)

    blk = (None, S5_CHUNK, 8, D_SSM)
    y_shape = jax.ShapeDtypeStruct((S5_NCHUNK, S5_CHUNK, 8, D_SSM), F32)
    return pl.pallas_call(
        _s5_kernel,
        grid=(S5_NCHUNK,),
        in_specs=[pl.BlockSpec(blk, lambda k: (k, 0, 0, 0)),
                  pl.BlockSpec(blk, lambda k: (rev_chunk(k), 0, 0, 0)),
                  pl.BlockSpec((None, 2, 2, 2, 256, 1024), lambda k: (l, 0, 0, 0, 0, 0)),
                  pl.BlockSpec((None, 2, 2, 4, 512, 128), lambda k: (l, 0, 0, 0, 0, 0)),
                  pl.BlockSpec((None, 2, 1, SSM_STATE), lambda k: (l, 0, 0, 0)),
                  pl.BlockSpec((None, 2, 1, SSM_STATE), lambda k: (l, 0, 0, 0)),
                  pl.BlockSpec((2, 8, SSM_STATE), lambda k: (0, 0, 0)),
                  pl.BlockSpec((2, 8, SSM_STATE), lambda k: (0, 0, 0))],
        out_specs=[pl.BlockSpec(blk, lambda k: (k, 0, 0, 0)),
                   pl.BlockSpec(blk, lambda k: (rev_chunk(k), 0, 0, 0)),
                   pl.BlockSpec((None, 2, 2, 8, SSM_STATE), lambda k: (jnp.minimum(k, 3), 0, 0, 0, 0))],
        out_shape=(y_shape, y_shape,
                   jax.ShapeDtypeStruct((4, 2, 2, 8, SSM_STATE), F32)),
        scratch_shapes=[pltpu.VMEM((2, 8, SSM_STATE), F32), pltpu.VMEM((2, 8, SSM_STATE), F32),
                        pltpu.VMEM((8, SSM_STATE), F32), pltpu.VMEM((8, SSM_STATE), F32),
                        pltpu.VMEM((S5_SUB * 8, SSM_STATE), F32), pltpu.VMEM((S5_SUB * 8, SSM_STATE), F32)],
        compiler_params=_cp(("arbitrary",), VMEM_LIMIT),
        name="s5_scan",
    )(u_all, u_all, wb, wc, a_re, a_im, h0_re, h0_im)


GLU_TM = 512


def _ssm_glu_kernel(u_ref, ysp_ref, yss_ref, d_ref, w_ref, b_ref, o_ref, w_b):
    i = pl.program_id(0)

    @pl.when(i == 0)
    def _():
        w_b[...] = w_ref[...].astype(BF16)

    ys = jnp.where(i < T_P // GLU_TM, ysp_ref[...], yss_ref[...])
    y = jax.nn.gelu(d_ref[...] * u_ref[...] + ys).astype(BF16)
    glu = jnp.dot(y, w_b[...], preferred_element_type=F32) + b_ref[...]
    o_ref[...] = glu[:, :D_MODEL] * _sigmoid(glu[:, D_MODEL:])


def _ssm_glu(proj, ys_p, ys_s, ssm_d3, w_glu, b_glu3, l):
    nbp = T_P // GLU_TM
    return pl.pallas_call(
        _ssm_glu_kernel,
        grid=(T // GLU_TM,),
        in_specs=[pl.BlockSpec((GLU_TM, D_SSM), lambda i: (i, 0)),
                  pl.BlockSpec((GLU_TM, D_SSM), lambda i: (jnp.minimum(i, nbp - 1), 0)),
                  pl.BlockSpec((GLU_TM, D_SSM), lambda i: (jnp.maximum(i - nbp, 0), 0)),
                  pl.BlockSpec((None, 1, D_SSM), lambda i: (l, 0, 0)),
                  pl.BlockSpec((None, D_SSM, 2 * D_MODEL), lambda i: (l, 0, 0)),
                  pl.BlockSpec((None, 1, 2 * D_MODEL), lambda i: (l, 0, 0))],
        out_specs=pl.BlockSpec((GLU_TM, D_MODEL), lambda i: (i, 0)),
        out_shape=jax.ShapeDtypeStruct((T, D_MODEL), F32),
        scratch_shapes=[pltpu.VMEM((D_SSM, 2 * D_MODEL), BF16)],
        compiler_params=_cp(("arbitrary",), VMEM_LIMIT),
        name="ssm_glu",
    )(proj, ys_p, ys_s, ssm_d3, w_glu, b_glu3)


def _mla_prep_kernel(cq_ref, ckv_ref, kr_ref, qg_ref, kvg_ref, wqa_ref, wqb_ref,
                     ca_ref, sb_ref, ck_ref, sk_ref,
                     q_ref, ckvn_ref, krr_ref, wqa_b, wqb_b):
    @pl.when(pl.program_id(0) == 0)
    def _():
        wqa_b[...] = wqa_ref[...].astype(BF16)
        wqb_b[...] = wqb_ref[...].astype(BF16)

    cq = cq_ref[...]
    cqn = (cq * lax.rsqrt(jnp.mean(cq * cq, axis=-1, keepdims=True) + EPS) * qg_ref[...]).astype(BF16)
    qa = jnp.dot(cqn, wqa_b[...], preferred_element_type=F32)
    qb = jnp.dot(cqn, wqb_b[...], preferred_element_type=F32)
    ca = ca_ref[...]
    sb = sb_ref[...]
    for h in range(N_HEADS):
        cols = slice(h * 128, (h + 1) * 128)
        q_ref[:, cols] = (qa[:, cols] * ca + qb[:, cols] * sb).astype(q_ref.dtype)

    ckv = ckv_ref[...]
    ckvn_ref[...] = ckv * lax.rsqrt(jnp.mean(ckv * ckv, axis=-1, keepdims=True) + EPS) * kvg_ref[...]

    kr = kr_ref[...]
    krr_ref[...] = kr * ck_ref[...] + pltpu.roll(kr, 128 - ROPE_DIM, 1) * sk_ref[...]


def _mla_prep(proj, q_norm_g3, kv_norm_g3, wqa, wqb, tabs, l):
    ca, sb, ck, sk = tabs
    tab_spec = pl.BlockSpec((TB, 128), lambda i: (_rope_blk(i), 0))
    return pl.pallas_call(
        _mla_prep_kernel,
        grid=(NTB,),
        in_specs=[pl.BlockSpec((TB, Q_LORA), lambda i: (i, 2)),
                  pl.BlockSpec((TB, 128), lambda i: (i, 6)),
                  pl.BlockSpec((TB, 128), lambda i: (i, 7)),
                  pl.BlockSpec((None, 1, Q_LORA), lambda i: (l, 0, 0)),
                  pl.BlockSpec((None, 1, KV_LORA), lambda i: (l, 0, 0)),
                  pl.BlockSpec((None, Q_LORA, 1024), lambda i: (l, 0, 0)),
                  pl.BlockSpec((None, Q_LORA, 1024), lambda i: (l, 0, 0)),
                  tab_spec, tab_spec, tab_spec, tab_spec],
        out_specs=[pl.BlockSpec((TB, 1024), lambda i: (i, 0)),
                   pl.BlockSpec((TB, 128), lambda i: (i, 0)),
                   pl.BlockSpec((TB, 128), lambda i: (i, 0))],
        out_shape=(jax.ShapeDtypeStruct((T, 1024), BF16),
                   jax.ShapeDtypeStruct((T, 128), F32),
                   jax.ShapeDtypeStruct((T, 128), F32)),
        scratch_shapes=[pltpu.VMEM((Q_LORA, 1024), BF16), pltpu.VMEM((Q_LORA, 1024), BF16)],
        compiler_params=_cp(("arbitrary",)),
        name="mla_prep",
    )(proj, proj, proj, q_norm_g3, kv_norm_g3, wqa, wqb, ca, sb, ck, sk)


ATT_TQ = 512


def _attn_kernel(q_ref, kvin_ref, wk_ref, wv_ref, wo_ref, o_ref,
                 k_scr, v_scr, o_scr, wk_b, wv_b, wo_b):
    s_id = pl.program_id(0)
    qb = pl.program_id(1)

    @pl.when((s_id == 0) & (qb == 0))
    def _():
        wk_b[...] = wk_ref[...].astype(BF16)
        wv_b[...] = wv_ref[...].astype(BF16)
        wo_b[...] = wo_ref[...].astype(BF16)

    @pl.when(qb == 0)
    def _():
        kv = kvin_ref[...].astype(BF16)
        k_scr[...] = jnp.dot(kv, wk_b[...], preferred_element_type=F32).astype(BF16)
        v_scr[...] = jnp.dot(kv[:, :KV_LORA], wv_b[...], preferred_element_type=F32).astype(BF16)

    scale = (NOPE_DIM + ROPE_DIM) ** -0.5
    for h in range(N_HEADS):
        qh = q_ref[:, h * 128:(h + 1) * 128]
        kh = k_scr[:, h * 128:(h + 1) * 128]
        s = lax.dot_general(qh, kh, (((1,), (1,)), ((), ())), preferred_element_type=F32) * scale
        m = jnp.max(s, axis=-1, keepdims=True)
        p = jnp.exp(s - m)
        den = jnp.sum(p, axis=-1, keepdims=True)
        oh = jnp.dot(p.astype(BF16), v_scr[:, h * V_DIM:(h + 1) * V_DIM], preferred_element_type=F32)
        o_scr[:, h * V_DIM:(h + 1) * V_DIM] = oh / den
    o_ref[...] = jnp.dot(o_scr[...].astype(BF16), wo_b[...], preferred_element_type=F32)


def _attn(q, kvin, wk2, wv, w_o, l, nseq, nqb, tq, row_blk0):
    lk = kvin.shape[1]
    return pl.pallas_call(
        _attn_kernel,
        grid=(nseq, nqb),
        in_specs=[pl.BlockSpec((tq, 1024), lambda s, b: (row_blk0 + s * nqb + b, 0)),
                  pl.BlockSpec((None, lk, 256), lambda s, b: (s, 0, 0)),
                  pl.BlockSpec((None, 256, 1024), lambda s, b: (l, 0, 0)),
                  pl.BlockSpec((None, KV_LORA, 512), lambda s, b: (l, 0, 0)),
                  pl.BlockSpec((None, 512, D_MODEL), lambda s, b: (l, 0, 0))],
        out_specs=pl.BlockSpec((tq, D_MODEL), lambda s, b: (s * nqb + b, 0)),
        out_shape=jax.ShapeDtypeStruct((nseq * nqb * tq, D_MODEL), F32),
        scratch_shapes=[pltpu.VMEM((lk, 1024), BF16), pltpu.VMEM((lk, 512), BF16),
                        pltpu.VMEM((tq, 512), F32),
                        pltpu.VMEM((256, 1024), BF16), pltpu.VMEM((KV_LORA, 512), BF16),
                        pltpu.VMEM((512, D_MODEL), BF16)],
        compiler_params=_cp(("arbitrary", "arbitrary"), VMEM_LIMIT),
        name="attn",
    )(q, kvin, wk2, wv, w_o)


CONV_PAD = 16
CONV_CB = 128
CONV_RB = 128


def _conv_kernel(cin_ref, w_ref, b_ref, lg_ref, lb_ref, pw_ref, pb_ref, o_ref,
                 pad_scr, conv_scr, act_scr, pw_b, *, seq):
    @pl.when(pl.program_id(0) == 0)
    def _():
        pw_b[...] = pw_ref[...].astype(BF16)
        pad_scr[0:CONV_PAD, :] = jnp.zeros((CONV_PAD, D_CONV), F32)
        pad_scr[CONV_PAD + seq:, :] = jnp.zeros((CONV_PAD, D_CONV), F32)

    pad_scr[CONV_PAD:CONV_PAD + seq, :] = cin_ref[:, :D_CONV] * _sigmoid(cin_ref[:, D_CONV:])

    win_rows = CONV_CB + 2 * CONV_PAD
    for c in range(D_CONV // 128):
        lanes = slice(c * 128, (c + 1) * 128)

        def cblk(rb, carry, lanes=lanes):
            base = pl.multiple_of(rb * CONV_CB, CONV_CB)
            win = pad_scr[pl.ds(base, win_rows), lanes]
            acc = jnp.broadcast_to(b_ref[:, lanes], (CONV_CB, 128))
            for b in range(8):
                wb = win if b == 0 else pltpu.roll(win, win_rows - b, 0)
                for a in range(2 * CONV_PAD // 8):
                    k = 8 * a + b - (CONV_PAD - CONV_K // 2)
                    if 0 <= k < CONV_K:
                        acc = acc + w_ref[k:k + 1, lanes] * wb[8 * a:8 * a + CONV_CB]
            conv_scr[pl.ds(base, CONV_CB), lanes] = acc
            return carry

        lax.fori_loop(0, seq // CONV_CB, cblk, 0)

    def blk(rb, carry):
        base = pl.multiple_of(rb * CONV_RB, CONV_RB)
        acc = conv_scr[pl.ds(base, CONV_RB), :]
        mu = jnp.mean(acc, axis=-1, keepdims=True)
        xc = acc - mu
        var = jnp.mean(xc * xc, axis=-1, keepdims=True)
        y = xc * lax.rsqrt(var + EPS) * lg_ref[...] + lb_ref[...]
        act_scr[pl.ds(base, CONV_RB), :] = (y * _sigmoid(y)).astype(BF16)
        return carry

    lax.fori_loop(0, seq // CONV_RB, blk, 0)
    o_ref[...] = jnp.dot(act_scr[...], pw_b[...], preferred_element_type=F32) + pb_ref[...]


def _conv(proj, dw_w, dw_b3, ln_g3, ln_b3, pw, pb3, l, nseq, seq, row_blk0):
    vec = lambda n: pl.BlockSpec((None, 1, n), lambda s: (l, 0, 0))
    return pl.pallas_call(
        functools.partial(_conv_kernel, seq=seq),
        grid=(nseq,),
        in_specs=[pl.BlockSpec((seq, 2 * D_CONV), lambda s: (row_blk0 + s, 1)),
                  pl.BlockSpec((None, CONV_K, D_CONV), lambda s: (l, 0, 0)),
                  vec(D_CONV), vec(D_CONV), vec(D_CONV),
                  pl.BlockSpec((None, D_CONV, D_MODEL), lambda s: (l, 0, 0)),
                  vec(D_MODEL)],
        out_specs=pl.BlockSpec((seq, D_MODEL), lambda s: (s, 0)),
        out_shape=jax.ShapeDtypeStruct((nseq * seq, D_MODEL), F32),
        scratch_shapes=[pltpu.VMEM((seq + 2 * CONV_PAD, D_CONV), F32),
                        pltpu.VMEM((seq, D_CONV), F32),
                        pltpu.VMEM((seq, D_CONV), BF16),
                        pltpu.VMEM((D_CONV, D_MODEL), BF16)],
        compiler_params=_cp(("arbitrary",), VMEM_LIMIT),
        name="conv",
    )(proj, dw_w, dw_b3, ln_g3, ln_b3, pw, pb3)


def _merge_kernel(xa_ref, xb_ref, os_ref, omp_ref, oms_ref, ocp_ref, ocs_ref, ga_ref, gb_ref, gc_ref,
                  wo_ref, mod_ref, ng_ref, rw_ref, rb_ref,
                  x1_ref, h2_ref, tw_ref, sc_ref, sr_ref, nb_ref, wo_b, rw_hi, rw_lo, *, n_a):
    i = pl.program_id(0)

    @pl.when(i == 0)
    def _():
        wo_b[...] = wo_ref[...].astype(BF16)
        rw = rw_ref[...]
        rw_hi[...] = rw.astype(BF16)
        rw_lo[...] = (rw - rw_hi[...].astype(F32)).astype(BF16)

    is_prompt = i < NTB_P
    o_mla = jnp.where(is_prompt, omp_ref[...], oms_ref[...])
    o_conv = jnp.where(is_prompt, ocp_ref[...], ocs_ref[...])
    merged = (_sigmoid(ga_ref[...]) * os_ref[...]
              + _sigmoid(gb_ref[...]) * o_mla
              + _sigmoid(gc_ref[...]) * o_conv)
    out = jnp.dot(merged.astype(BF16), wo_b[...], preferred_element_type=F32)
    x1 = jnp.where(i < n_a, xa_ref[...], xb_ref[...]) + mod_ref[2:3, :] * out
    x1_ref[...] = x1
    y = x1 * lax.rsqrt(jnp.mean(x1 * x1, axis=-1, keepdims=True) + EPS) * ng_ref[...]
    h2 = y * (1.0 + mod_ref[4:5, :]) + mod_ref[3:4, :]
    h2_hi = h2.astype(BF16)
    h2_ref[...] = h2_hi

    h2_lo = (h2 - h2_hi.astype(F32)).astype(BF16)
    logits = (jnp.dot(h2_hi, rw_hi[...], preferred_element_type=F32)
              + jnp.dot(h2_lo, rw_hi[...], preferred_element_type=F32)
              + jnp.dot(h2_hi, rw_lo[...], preferred_element_type=F32)) + rb_ref[...]
    lane_i = lax.broadcasted_iota(jnp.int32, logits.shape, 1)
    lane = lane_i.astype(F32)
    cur = logits
    vals, idxs = [], []
    for _ in range(TOP_K):
        m = jnp.max(cur, axis=-1, keepdims=True)
        idx = jnp.min(jnp.where(cur == m, lane, 128.0), axis=-1, keepdims=True)
        vals.append(m)
        idxs.append(idx)
        cur = jnp.where(lane == idx, -jnp.inf, cur)
    exps = [jnp.exp(v - vals[0]) for v in vals]
    tot = exps[0] + exps[1] + exps[2] + exps[3]
    tw = jnp.zeros(logits.shape, F32)
    for k in range(TOP_K):
        tw = jnp.where(lane_i == k, exps[k] / tot, tw)
    tw_ref[...] = tw

    ohs = [jnp.where(lane == idxs[k], 1.0, 0.0) for k in range(TOP_K)]
    cnts = [jnp.sum(oh, axis=0, keepdims=True) for oh in ohs]
    n_b = jnp.ceil((cnts[0] + cnts[1] + cnts[2] + cnts[3]) * (1.0 / SEG_ALIGN)) * SEG_ALIGN
    inc = jnp.broadcast_to(n_b, (8, 128))
    lane8 = lax.broadcasted_iota(jnp.int32, (8, 128), 1)
    for s in (1, 2, 4, 8, 16, 32, 64):
        inc = inc + jnp.where(lane8 >= s, pltpu.roll(inc, s, 1), 0.0)
    before = (inc - n_b)[0:1, :]
    r_i = lax.broadcasted_iota(jnp.int32, (TB, TB), 0)
    c_i = lax.broadcasted_iota(jnp.int32, (TB, TB), 1)
    earlier = jnp.where(r_i > c_i, 1.0, 0.0).astype(BF16)
    slots = jnp.zeros(logits.shape, F32)
    for k in range(TOP_K):
        pre = jnp.dot(earlier, ohs[k].astype(BF16), preferred_element_type=F32)
        slot_k = jnp.sum(ohs[k] * (pre + before), axis=-1, keepdims=True)
        slots = jnp.where(lane_i == k, slot_k, slots)
        before = before + cnts[k]
    sc_ref[...] = slots.astype(jnp.int32)
    sr_ref[...] = slots.T[0:8, :].astype(jnp.int32)
    nb_ref[...] = jnp.broadcast_to(n_b, (8, 128)).astype(jnp.int32)


def _merge(xa, xb, o_ssm, o_mla_p, o_mla_s, o_conv_p, o_conv_s, proj, w_out, modt, norm_ffn_g3, rw_p, rb_p, l):
    n_a = xa.shape[0] // TB if xb is not xa else NTB
    row = lambda i: (i, 0)
    row_p = lambda i: (jnp.minimum(i, NTB_P - 1), 0)
    row_s = lambda i: (jnp.maximum(i - NTB_P, 0), 0)
    blk = lambda m: pl.BlockSpec((TB, D_MODEL), m)
    return pl.pallas_call(
        functools.partial(_merge_kernel, n_a=n_a),
        grid=(NTB,),
        in_specs=[blk(lambda i: (jnp.minimum(i, n_a - 1), 0)), blk(lambda i: (jnp.maximum(i - n_a, 0), 0)),
                  blk(row), blk(row_p), blk(row_s), blk(row_p), blk(row_s),
                  blk(lambda i: (i, 2)), blk(lambda i: (i, 3)), blk(lambda i: (i, 4)),
                  pl.BlockSpec((None, D_MODEL, D_MODEL), lambda i: (l, 0, 0)),
                  pl.BlockSpec((None, 6, D_MODEL), lambda i: (_mod_row(i), 0, 0)),
                  pl.BlockSpec((None, 1, D_MODEL), lambda i: (l, 0, 0)),
                  pl.BlockSpec((None, D_MODEL, 128), lambda i: (l, 0, 0)),
                  pl.BlockSpec((None, 1, 128), lambda i: (l, 0, 0))],
        out_specs=[blk(row), blk(row), pl.BlockSpec((TB, 128), row), pl.BlockSpec((TB, 128), row),
                   pl.BlockSpec((None, 8, TB), lambda i: (i, 0, 0)),
                   pl.BlockSpec((None, 8, 128), lambda i: (i, 0, 0))],
        out_shape=(jax.ShapeDtypeStruct((T, D_MODEL), F32), jax.ShapeDtypeStruct((T, D_MODEL), BF16),
                   jax.ShapeDtypeStruct((T, 128), F32), jax.ShapeDtypeStruct((T, 128), jnp.int32),
                   jax.ShapeDtypeStruct((NTB, 8, TB), jnp.int32),
                   jax.ShapeDtypeStruct((NTB, 8, 128), jnp.int32)),
        scratch_shapes=[pltpu.VMEM((D_MODEL, D_MODEL), BF16),
                        pltpu.VMEM((D_MODEL, 128), BF16), pltpu.VMEM((D_MODEL, 128), BF16)],
        compiler_params=_cp(("arbitrary",), VMEM_LIMIT),
        name="merge",
    )(xa, xb, o_ssm, o_mla_p, o_mla_s, o_conv_p, o_conv_s, proj, proj, proj, w_out, modt, norm_ffn_g3, rw_p, rb_p)


PAIRS = TB * TOP_K
G_ROWS = -(-(PAIRS + (SEG_ALIGN - 1) * N_EXPERTS) // 256) * 256
SEG_BITS = 9


def _aligned(off):
    return off if isinstance(off, int) else pl.multiple_of(off, SEG_ALIGN)


def _segment_copies(src_ref, dst_ref, sem, n, src_off, dst_off):
    copies = []
    done = 0
    for bit in range(SEG_BITS - 1, SEG_ALIGN.bit_length() - 2, -1):
        size = 1 << bit
        part = n & size
        copies.append((part != 0,
                       pltpu.make_async_copy(src_ref.at[pl.ds(_aligned(src_off + done), size)],
                                             dst_ref.at[pl.ds(_aligned(dst_off + done), size)], sem)))
        done = done + part
    return copies


def _start_all(copies):
    for pred, cp in copies:
        pl.when(pred)(cp.start)


def _wait_all(copies):
    for pred, cp in copies:
        pl.when(pred)(cp.wait)


def _dispatch_kernel(seg_ref, nbe_ref, cnt_ref, pst_ref, nu_ref, h_ref, sr_ref, xs_ref, g_scr, z_scr, sem):
    b = pl.program_id(0)
    zsem = sem.at[2]

    def copies(blk, e, off):
        n = nbe_ref[blk * N_EXPERTS + e]
        dst = seg_ref[blk * N_EXPERTS + e]
        slot = blk % 2
        return n, _segment_copies(g_scr.at[slot], xs_ref, sem.at[slot], n, off, dst)

    def drain(blk):
        def wait(e, off):
            n, cps = copies(blk, e, off)
            _wait_all(cps)
            return off + n

        lax.fori_loop(0, N_EXPERTS, wait, 0)

    @pl.when(b < NTB)
    def _():
        sub = lax.broadcasted_iota(jnp.int32, (G_ROWS, TB), 0)
        sr = sr_ref[...]
        hit = sub == sr[0:1, :]
        for k in range(1, TOP_K):
            hit = hit | (sub == sr[k:k + 1, :])
        g_scr[b % 2] = jnp.dot(jnp.where(hit, 1.0, 0.0).astype(BF16), h_ref[...], preferred_element_type=F32)

        def start(e, off):
            n, cps = copies(b, e, off)
            _start_all(cps)
            return off + n

        lax.fori_loop(0, N_EXPERTS, start, 0)

    @pl.when(b > 0)
    def _():
        drain(b - 1)

    @pl.when(b == NTB)
    def _():
        sem = zsem
        z_scr[...] = jnp.zeros_like(z_scr)

        def pad(e):
            cnt = cnt_ref[e]
            padded = (cnt + MOE_BM - 1) // MOE_BM * MOE_BM
            return padded - cnt, pst_ref[e] + cnt

        def start(e, carry):
            n, dst = pad(e)
            _start_all(_segment_copies(z_scr, xs_ref, sem, n, 0, dst))
            return carry

        def wait(e, carry):
            n, dst = pad(e)
            _wait_all(_segment_copies(z_scr, xs_ref, sem, n, 0, dst))
            return carry

        lax.fori_loop(0, N_EXPERTS, start, 0)
        lax.fori_loop(0, N_EXPERTS, wait, 0)

        def tail_copy(blk):
            return pltpu.make_async_copy(z_scr, xs_ref.at[pl.ds(blk * MOE_BM, MOE_BM)], sem)

        def tail_start(blk, carry):
            tail_copy(blk).start()
            return carry

        def tail_wait(blk, carry):
            tail_copy(blk).wait()
            return carry

        lax.fori_loop(nu_ref[0], MOE_NB, tail_start, 0)
        lax.fori_loop(nu_ref[0], MOE_NB, tail_wait, 0)


def _dispatch(seg_start, n_be, counts, pad_start, n_used, h2, slot_row):
    last = lambda b, *_: (jnp.minimum(b, NTB - 1), 0)
    grid_spec = pltpu.PrefetchScalarGridSpec(
        num_scalar_prefetch=5,
        grid=(NTB + 1,),
        in_specs=[pl.BlockSpec((TB, D_MODEL), last),
                  pl.BlockSpec((None, 8, TB), lambda b, *_: (jnp.minimum(b, NTB - 1), 0, 0))],
        out_specs=pl.BlockSpec(memory_space=pl.ANY),
        scratch_shapes=[pltpu.VMEM((2, G_ROWS, D_MODEL), F32),
                        pltpu.VMEM((MOE_BM, D_MODEL), F32),
                        pltpu.SemaphoreType.DMA((3,))],
    )
    return pl.pallas_call(
        _dispatch_kernel,
        grid_spec=grid_spec,
        out_shape=jax.ShapeDtypeStruct((MOE_ROWS, D_MODEL), F32),
        compiler_params=_cp(("arbitrary",), VMEM_LIMIT),
        name="moe_dispatch",
    )(seg_start, n_be, counts, pad_start, n_used, h2, slot_row)


def _expert_kernel(be_ref, nu_ref, x_ref, w1_ref, b1_ref, w2_ref, b2_ref, y_ref, w1_b, w2_b):
    i = pl.program_id(0)
    prev = be_ref[jnp.maximum(i - 1, 0)]

    @pl.when((i == 0) | (be_ref[i] != prev))
    def _():
        w1_b[...] = w1_ref[...].astype(BF16)
        w2_b[...] = w2_ref[...].astype(BF16)

    @pl.when(i < nu_ref[0])
    def _():
        gu = jnp.dot(x_ref[...].astype(BF16), w1_b[...], preferred_element_type=F32) + b1_ref[...]
        g = jnp.minimum(gu[:, :D_FF], SWIGLU_LIMIT)
        u = jnp.clip(gu[:, D_FF:], -SWIGLU_LIMIT, SWIGLU_LIMIT)
        act = (u + 1.0) * (g * _sigmoid(SWIGLU_ALPHA * g))
        y_ref[...] = jnp.dot(act.astype(BF16), w2_b[...], preferred_element_type=F32) + b2_ref[...]

    @pl.when(i >= nu_ref[0])
    def _():
        y_ref[...] = jnp.zeros_like(y_ref)


def _experts(block_e, n_used, xs, w1, b1_4, w2, b2_4, l):
    grid_spec = pltpu.PrefetchScalarGridSpec(
        num_scalar_prefetch=2,
        grid=(MOE_NB,),
        in_specs=[pl.BlockSpec((MOE_BM, D_MODEL), lambda i, be, nu: (jnp.minimum(i, nu[0] - 1), 0)),
                  pl.BlockSpec((None, None, D_MODEL, 2 * D_FF), lambda i, be, nu: (l, be[i], 0, 0)),
                  pl.BlockSpec((None, None, 1, 2 * D_FF), lambda i, be, nu: (l, be[i], 0, 0)),
                  pl.BlockSpec((None, None, D_FF, D_MODEL), lambda i, be, nu: (l, be[i], 0, 0)),
                  pl.BlockSpec((None, None, 1, D_MODEL), lambda i, be, nu: (l, be[i], 0, 0))],
        out_specs=pl.BlockSpec((MOE_BM, D_MODEL), lambda i, be, nu: (i, 0)),
        scratch_shapes=[pltpu.VMEM((D_MODEL, 2 * D_FF), BF16), pltpu.VMEM((D_FF, D_MODEL), BF16)],
    )
    return pl.pallas_call(
        _expert_kernel,
        grid_spec=grid_spec,
        out_shape=jax.ShapeDtypeStruct((MOE_ROWS, D_MODEL), F32),
        compiler_params=_cp(("arbitrary",), VMEM_LIMIT),
        name="moe_experts",
    )(block_e, n_used, xs, w1, b1_4, w2, b2_4)


def _combine_kernel(seg_ref, nbe_ref, x1_ref, tw_ref, sc_ref, mod_ref, fg_ref, ys_ref, *rest, final):
    if final:
        x2_ref, yn_ref, g_scr, sem = rest
    else:
        x2_ref, g_scr, sem = rest
    b = pl.program_id(0)

    def copies(blk, e, off):
        n = nbe_ref[blk * N_EXPERTS + e]
        src = seg_ref[blk * N_EXPERTS + e]
        slot = blk % 2
        return n, _segment_copies(ys_ref, g_scr.at[slot], sem.at[slot], n, src, off)

    def fetch(blk):
        g_scr[blk % 2, PAIRS:, :] = jnp.zeros((G_ROWS - PAIRS, D_MODEL), F32)

        def start(e, off):
            n, cps = copies(blk, e, off)
            _start_all(cps)
            return off + n

        lax.fori_loop(0, N_EXPERTS, start, 0)

    @pl.when(b == 0)
    def _():
        fetch(b)

    @pl.when(b + 1 < NTB)
    def _():
        fetch(b + 1)

    def wait(e, off):
        n, cps = copies(b, e, off)
        _wait_all(cps)
        return off + n

    lax.fori_loop(0, N_EXPERTS, wait, 0)

    sc = sc_ref[...]
    tw = tw_ref[...]
    lane = lax.broadcasted_iota(jnp.int32, (TB, G_ROWS), 1)
    pw = jnp.zeros((TB, G_ROWS), F32)
    for k in range(TOP_K):
        pw = pw + jnp.where(lane == sc[:, k:k + 1], tw[:, k:k + 1], 0.0)
    moe = jnp.dot(pw.astype(BF16), g_scr[b % 2].astype(BF16), preferred_element_type=F32)
    x2 = x1_ref[...] + mod_ref[5:6, :] * moe
    x2_ref[...] = x2
    if final:
        yn_ref[...] = x2 * lax.rsqrt(jnp.mean(x2 * x2, axis=-1, keepdims=True) + EPS) * fg_ref[...]


def _combine(seg_start, n_be, x1, topw, slot_col, modt, final_g2, ys, final):
    row = lambda b, *_: (b, 0)
    out_blk = pl.BlockSpec((TB, D_MODEL), row)
    out_sds = jax.ShapeDtypeStruct((T, D_MODEL), F32)
    grid_spec = pltpu.PrefetchScalarGridSpec(
        num_scalar_prefetch=2,
        grid=(NTB,),
        in_specs=[pl.BlockSpec((TB, D_MODEL), row),
                  pl.BlockSpec((TB, 128), row),
                  pl.BlockSpec((TB, 128), row),
                  pl.BlockSpec((None, 6, D_MODEL), lambda b, *_: (_mod_row(b), 0, 0)),
                  pl.BlockSpec((1, D_MODEL), lambda b, *_: (0, 0)),
                  pl.BlockSpec(memory_space=pl.ANY)],
        out_specs=[out_blk, out_blk] if final else [out_blk],
        scratch_shapes=[pltpu.VMEM((2, G_ROWS, D_MODEL), F32), pltpu.SemaphoreType.DMA((2,))],
    )
    return pl.pallas_call(
        functools.partial(_combine_kernel, final=final),
        grid_spec=grid_spec,
        out_shape=(out_sds, out_sds) if final else (out_sds,),
        compiler_params=_cp(("arbitrary",), VMEM_LIMIT),
        name="moe_combine",
    )(seg_start, n_be, x1, topw, slot_col, modt, final_g2, ys)


def _rot_cols(w):
    q = ROPE_DIM // 4
    parts = []
    for half in range(2):
        a = w[..., half * 2 * q:half * 2 * q + q]
        b = w[..., half * 2 * q + q:(half + 1) * 2 * q]
        parts += [-b, a]
    return jnp.concatenate(parts, axis=-1)


def _rope_tables():
    rows = DEC_SEQ // GRID_W
    row = jnp.repeat(jnp.arange(rows), GRID_W).astype(F32)
    col = jnp.tile(jnp.arange(GRID_W), rows).astype(F32)
    half = ROPE_DIM // 2
    freqs = ROPE_BASE ** (-jnp.arange(0, half, 2, dtype=F32) / half)
    ang_r = row[:, None] * freqs
    ang_c = col[:, None] * freqs
    ang = jnp.concatenate([ang_r, ang_r, ang_c, ang_c], axis=-1)
    cos = jnp.concatenate([jnp.cos(ang), jnp.ones((TB, ROPE_DIM), F32)], axis=0)
    sin = jnp.concatenate([jnp.sin(ang), jnp.zeros((TB, ROPE_DIM), F32)], axis=0)
    n = cos.shape[0]
    z = lambda w: jnp.zeros((n, w), F32)
    ca = jnp.concatenate([jnp.ones((n, NOPE_DIM), F32), cos, z(32)], axis=1)
    sb = jnp.concatenate([z(NOPE_DIM), sin, z(32)], axis=1)
    ck = jnp.concatenate([cos, z(96)], axis=1)
    sk = jnp.concatenate([sin, z(96)], axis=1)
    return ca, sb, ck, sk


def kernel(x_prompt, x_sample, c, cache_kv_latent, cache_k_rope, state_ssm, c_ctx, norm_mix_g, norm_ffn_g, final_norm_g, w_ada, b_ada, w_in, q_norm_g, w_uq, kv_norm_g, w_ukv, w_o_mla, ssm_lam_re, ssm_lam_im, ssm_log_dt, ssm_b_re, ssm_b_im, ssm_c_re, ssm_c_im, ssm_d, w_ssm_glu, b_ssm_glu, conv_dw_w, conv_dw_b, conv_ln_g, conv_ln_b, w_conv_pw2, b_conv_pw2, w_out, router_w, router_b, moe_w1, moe_b1, moe_w2, moe_b2):
    L = DEPTH
    v3 = lambda a: a.reshape(L, 1, a.shape[-1])

    xa, xb = x_prompt.reshape(T_P, D_MODEL), x_sample.reshape(T_S, D_MODEL)
    cc = jnp.concatenate([c, c_ctx[None, :], jnp.zeros((16 - DEC_BATCH - 1, D_MODEL), F32)], axis=0)


    wq = w_uq.reshape(L, Q_LORA, N_HEADS, NOPE_DIM + ROPE_DIM)
    zq = lambda w: jnp.zeros((L, Q_LORA, N_HEADS, w), F32)
    wqa = jnp.concatenate([wq, zq(32)], axis=-1).reshape(L, Q_LORA, 1024)
    wqb = jnp.concatenate([zq(NOPE_DIM), _rot_cols(wq[..., NOPE_DIM:]), zq(32)], axis=-1).reshape(L, Q_LORA, 1024)
    wkv = w_ukv.reshape(L, KV_LORA, N_HEADS, NOPE_DIM + V_DIM)
    wk_lat = jnp.concatenate([wkv[..., :NOPE_DIM], jnp.zeros((L, KV_LORA, N_HEADS, 64), F32)], axis=-1)
    place = jnp.concatenate([jnp.zeros((ROPE_DIM, NOPE_DIM), F32), jnp.eye(ROPE_DIM, dtype=F32),
                             jnp.zeros((ROPE_DIM, 32), F32)], axis=-1)
    wk_rope = jnp.broadcast_to(place[None, :, None, :], (L, ROPE_DIM, N_HEADS, 128))
    wk2 = jnp.concatenate([wk_lat, wk_rope, jnp.zeros((L, 96, N_HEADS, 128), F32)], axis=1).reshape(L, 256, 1024)
    wv = wkv[..., NOPE_DIM:].reshape(L, KV_LORA, N_HEADS * V_DIM)
    tabs = _rope_tables()

    a_re, a_im, f_re, f_im = [v.reshape(L, 2, N_SSM_GROUPS, N_STATE)
                              for v in _s5_disc(ssm_lam_re, ssm_lam_im, ssm_log_dt)]
    bb_re = f_re[..., None] * ssm_b_re - f_im[..., None] * ssm_b_im
    bb_im = f_re[..., None] * ssm_b_im + f_im[..., None] * ssm_b_re
    eye16 = jnp.eye(16, dtype=F32)
    eye8 = jnp.eye(8, dtype=F32)

    def bd_in(bb):
        bb = bb.reshape(L, 2, 2, 16, N_STATE, SSM_GROUP)
        return jnp.einsum('ldhgpc,gk->ldhgckp', bb, eye16).reshape(L, 2, 2, 256, 1024)

    def bd_out(cm):
        cm = cm.reshape(L, 2, 4, 8, SSM_GROUP, N_STATE)
        return jnp.einsum('ldqgcp,gk->ldqgpkc', cm, eye8).reshape(L, 2, 4, 512, 128)

    wb = jnp.stack([bd_in(bb_re), bd_in(bb_im)], axis=2).astype(BF16)
    wc = jnp.stack([bd_out(ssm_c_re), bd_out(-ssm_c_im)], axis=2).astype(BF16)
    a_re4 = a_re.reshape(L, 2, 1, SSM_STATE)
    a_im4 = a_im.reshape(L, 2, 1, SSM_STATE)

    rw_p = jnp.concatenate([router_w, jnp.zeros((L, D_MODEL, 128 - N_EXPERTS), F32)], axis=-1)
    rb_p = jnp.concatenate([router_b, jnp.full((L, 128 - N_EXPERTS), -1e30, F32)], axis=-1).reshape(L, 1, 128)

    kv_out, kr_out, ssm_out = [], [], []
    y_norm = None
    for l in range(L):
        modt = _ada(cc, w_ada, v3(b_ada), l).reshape(16, 6, D_MODEL)
        proj = _win(xa, xb, v3(norm_mix_g), modt, w_in, l)

        u_p = proj[:T_P, :D_SSM].reshape(4, 8, SEQ, D_SSM).transpose(0, 2, 1, 3)
        u_s = proj[T_P:, :D_SSM].reshape(DEC_BATCH, 4, S5_CHUNK, D_SSM).transpose(1, 2, 0, 3)
        u_all = jnp.concatenate([u_p, u_s], axis=0)
        st = state_ssm[:, l]
        h0 = st.transpose(4, 1, 0, 2, 3).reshape(2, 2, DEC_BATCH, SSM_STATE)
        y_f, y_r, fin = _s5(u_all, wb, wc, a_re4, a_im4, h0[0], h0[1], l)
        ysum = y_f + y_r
        o_ssm = _ssm_glu(proj, ysum[:4].transpose(0, 2, 1, 3).reshape(T_P, D_SSM),
                         ysum[4:].transpose(2, 0, 1, 3).reshape(T_S, D_SSM),
                         v3(ssm_d), w_ssm_glu, v3(b_ssm_glu), l)
        ssm_out.append(fin.reshape(4, 2, 2, 8, N_SSM_GROUPS, N_STATE).transpose(0, 3, 1, 4, 5, 2)
                       .reshape(BATCH, 2, N_SSM_GROUPS, N_STATE, 2))

        q, ckvn, krr = _mla_prep(proj, v3(q_norm_g), v3(kv_norm_g), wqa, wqb, tabs, l)
        kvin_p = jnp.concatenate([ckvn[:T_P], krr[:T_P]], axis=-1).reshape(BATCH, SEQ, 256)
        cache_kr = jnp.concatenate([cache_k_rope[:, l], jnp.zeros((DEC_BATCH, PAST_LEN, 96), F32)], axis=-1)
        kvin_s = jnp.concatenate([
            jnp.concatenate([cache_kv_latent[:, l], cache_kr], axis=-1),
            jnp.concatenate([ckvn[T_P:], krr[T_P:]], axis=-1).reshape(DEC_BATCH, DEC_SEQ, 256)], axis=1)
        o_mla_p = _attn(q, kvin_p, wk2, wv, w_o_mla, l, BATCH, 1, SEQ, 0)
        o_mla_s = _attn(q, kvin_s, wk2, wv, w_o_mla, l, DEC_BATCH, DEC_SEQ // ATT_TQ, ATT_TQ, T_P // ATT_TQ)
        kv_out.append(ckvn[:T_P].reshape(BATCH, SEQ, KV_LORA))
        kr_out.append(proj[:T_P, KR0:KR0 + ROPE_DIM].reshape(BATCH, SEQ, ROPE_DIM))

        conv_args = (conv_dw_w, v3(conv_dw_b), v3(conv_ln_g), v3(conv_ln_b), w_conv_pw2, v3(b_conv_pw2), l)
        o_conv_p = _conv(proj, *conv_args, BATCH, SEQ, 0)
        o_conv_s = _conv(proj, *conv_args, DEC_BATCH, DEC_SEQ, T_P // DEC_SEQ)

        x1, h2, topw, slot_col, slot_row, n_blk = _merge(
            xa, xb, o_ssm, o_mla_p, o_mla_s, o_conv_p, o_conv_s, proj, w_out, modt, v3(norm_ffn_g), rw_p, rb_p, l)

        n_be = n_blk[:, 0, :N_EXPERTS]
        counts = jnp.sum(n_be, axis=0)
        padded = (counts + MOE_BM - 1) // MOE_BM * MOE_BM
        pad_end = jnp.cumsum(padded)
        pad_start = pad_end - padded
        seg_start = (pad_start[None, :] + jnp.cumsum(n_be, axis=0) - n_be).reshape(-1).astype(jnp.int32)
        blk_row = jnp.arange(MOE_NB, dtype=jnp.int32) * MOE_BM
        block_e = jnp.minimum(jnp.sum((pad_end[None, :] <= blk_row[:, None]).astype(jnp.int32), axis=1),
                              N_EXPERTS - 1).astype(jnp.int32)
        n_used = (pad_end[-1:] // MOE_BM).astype(jnp.int32)
        n_be = n_be.reshape(-1)

        xs = _dispatch(seg_start, n_be, counts.astype(jnp.int32), pad_start.astype(jnp.int32), n_used, h2, slot_row)
        ys = _experts(block_e, n_used, xs, moe_w1, moe_b1.reshape(L, N_EXPERTS, 1, 2 * D_FF),
                      moe_w2, moe_b2.reshape(L, N_EXPERTS, 1, D_MODEL), l)
        outs = _combine(seg_start, n_be, x1, topw, slot_col, modt, final_norm_g.reshape(1, D_MODEL), ys,
                        final=(l == L - 1))
        xa = xb = outs[0]
        if l == L - 1:
            y_norm = outs[1]

    y_prompt = y_norm[:T_P].reshape(BATCH, SEQ, D_MODEL)
    y_sample = y_norm[T_P:].reshape(DEC_BATCH, DEC_SEQ, D_MODEL)
    return (y_prompt, y_sample, jnp.stack(kv_out, axis=1), jnp.stack(kr_out, axis=1), jnp.stack(ssm_out, axis=1))
```

```python
import functools
import math

import numpy as np
import jax
import jax.numpy as jnp
from jax import lax
from jax.experimental import pallas as pl
from jax.experimental.pallas import tpu as pltpu

F32 = jnp.float32
BF16 = jnp.bfloat16

D_MODEL = 1024
BATCH = 32
SEQ = 256
DEPTH = 2
DEC_BATCH = 8
DEC_SEQ = 1024
PAST_LEN = 256
GRID_W = 64
EPS = 1e-6
D_SSM = 512
SSM_GROUP = 16
N_SSM_GROUPS = 32
N_STATE = 64
N_HEADS = 8
NOPE_DIM = 64
ROPE_DIM = 32
V_DIM = 64
Q_LORA = 256
KV_LORA = 128
ROPE_BASE = 10000.0
D_CONV = 512
CONV_K = 31
N_EXPERTS = 32
TOP_K = 4
D_FF = 1024
SWIGLU_ALPHA = 1.702
SWIGLU_LIMIT = 7.0

T_P = BATCH * SEQ
T_S = DEC_BATCH * DEC_SEQ
T = T_P + T_S
TB = 256
NTB = T // TB
NTB_P = T_P // TB
TB_PER_S = DEC_SEQ // TB
SSM_STATE = N_SSM_GROUPS * N_STATE
PROJ_W = 5120
MOE_BM = 512
SEG_ALIGN = 8
MOE_NB = (T * TOP_K + (SEG_ALIGN - 1) * NTB * N_EXPERTS) // MOE_BM + N_EXPERTS
MOE_ROWS = MOE_NB * MOE_BM
VMEM_LIMIT = 56 * 1024 * 1024
BRANCH_DTYPE = BF16


def _cp(sem, vmem=None):
    return pltpu.CompilerParams(dimension_semantics=sem, vmem_limit_bytes=vmem)


def _sigmoid(x):
    return 0.5 * jnp.tanh(0.5 * x) + 0.5


def _mod_row(i):
    return jnp.where(i < NTB_P, DEC_BATCH, (i - NTB_P) // TB_PER_S)


def _ada_kernel(c_ref, w_ref, b_ref, o_ref):
    c = c_ref[...]
    s = c * jax.nn.sigmoid(c)
    o_ref[...] = jnp.dot(s, w_ref[...], preferred_element_type=F32,
                         precision=lax.Precision.HIGHEST) + b_ref[...]


def _ada(cc, w_ada, b_ada3, l):
    tn = 1024
    return pl.pallas_call(
        _ada_kernel,
        grid=(6 * D_MODEL // tn,),
        in_specs=[pl.BlockSpec((16, D_MODEL), lambda j: (0, 0)),
                  pl.BlockSpec((None, D_MODEL, tn), lambda j: (l, 0, j)),
                  pl.BlockSpec((None, 1, tn), lambda j: (l, 0, j))],
        out_specs=pl.BlockSpec((16, tn), lambda j: (0, j)),
        out_shape=jax.ShapeDtypeStruct((16, 6 * D_MODEL), F32),
        compiler_params=_cp(("arbitrary",)),
        name="ada",
    )(cc, w_ada, b_ada3)


WIN_TM = 1024
WIN_TN = 1024
KR0 = D_SSM + Q_LORA + KV_LORA
IN_W = KR0 + ROPE_DIM + 2 * D_CONV + 3 * D_MODEL
PROJ_SHIFT = PROJ_W - IN_W


def _win_kernel(xa_ref, xb_ref, g_ref, mod_ref, wp_ref, wc_ref, o_ref, wb_ref, *, n_a):
    j = pl.program_id(0)
    i = pl.program_id(1)

    @pl.when((i == 0) & (j == 0))
    def _():
        w = wc_ref[...]
        kr = w[:, KR0:KR0 + ROPE_DIM]
        wb_ref[...] = jnp.concatenate(
            [w[:, :KR0 + ROPE_DIM], _rot_cols(kr), jnp.zeros((D_MODEL, PROJ_SHIFT - ROPE_DIM), F32)],
            axis=1).astype(BF16)

    @pl.when((i == 0) & (j > 0))
    def _():
        wb_ref[...] = jnp.concatenate([wp_ref[:, WIN_TN - PROJ_SHIFT:], wc_ref[:, :WIN_TN - PROJ_SHIFT]],
                                      axis=1).astype(BF16)

    x = jnp.where(i < n_a, xa_ref[...], xb_ref[...])
    y = x * lax.rsqrt(jnp.mean(x * x, axis=-1, keepdims=True) + EPS) * g_ref[...]
    h = (y * (1.0 + mod_ref[1:2, :]) + mod_ref[0:1, :]).astype(BF16)
    o_ref[...] = jnp.dot(h, wb_ref[...], preferred_element_type=F32)


def _win(xa, xb, g3, modt, w_in, l):
    n_a = xa.shape[0] // WIN_TM if xb is not xa else T // WIN_TM
    ni = T // WIN_TM
    nbp = T_P // WIN_TM
    return pl.pallas_call(
        functools.partial(_win_kernel, n_a=n_a),
        grid=(PROJ_W // WIN_TN, ni),
        in_specs=[pl.BlockSpec((WIN_TM, D_MODEL), lambda j, i: (jnp.minimum(i, n_a - 1), 0)),
                  pl.BlockSpec((WIN_TM, D_MODEL), lambda j, i: (jnp.maximum(i - n_a, 0), 0)),
                  pl.BlockSpec((None, 1, D_MODEL), lambda j, i: (l, 0, 0)),
                  pl.BlockSpec((None, 6, D_MODEL),
                               lambda j, i: (jnp.where(i < nbp, DEC_BATCH, i - nbp), 0, 0)),
                  pl.BlockSpec((None, D_MODEL, WIN_TN), lambda j, i: (l, 0, jnp.maximum(j - 1, 0))),
                  pl.BlockSpec((None, D_MODEL, WIN_TN), lambda j, i: (l, 0, j))],
        out_specs=pl.BlockSpec((WIN_TM, WIN_TN), lambda j, i: (i, j)),
        out_shape=jax.ShapeDtypeStruct((T, PROJ_W), F32),
        scratch_shapes=[pltpu.VMEM((D_MODEL, WIN_TN), BF16)],
        compiler_params=_cp(("arbitrary", "arbitrary"), VMEM_LIMIT),
        name="w_in",
    )(xa, xb, g3, modt, w_in, w_in)


def _s5_disc_kernel(lr_ref, li_ref, ldt_ref, ar_ref, ai_ref, fr_ref, fi_ref):
    dt = jnp.exp(ldt_ref[...])
    lr = lr_ref[...]
    li = li_ref[...]
    mag = jnp.exp(lr * dt)
    a_re = mag * jnp.cos(li * dt)
    a_im = mag * jnp.sin(li * dt)
    den = lr * lr + li * li
    ar_ref[...] = a_re
    ai_ref[...] = a_im
    fr_ref[...] = ((a_re - 1.0) * lr + a_im * li) / den
    fi_ref[...] = (a_im * lr - (a_re - 1.0) * li) / den


def _s5_disc(lam_re, lam_im, log_dt):
    rows = DEPTH * 2 * N_SSM_GROUPS
    shp = jax.ShapeDtypeStruct((rows, N_STATE), F32)
    return pl.pallas_call(
        _s5_disc_kernel,
        out_shape=(shp, shp, shp, shp),
        name="s5_disc",
    )(lam_re.reshape(rows, N_STATE), lam_im.reshape(rows, N_STATE), log_dt.reshape(rows, 1))


S5_CHUNK = 256
S5_SUB = 64
S5_NSUB = S5_CHUNK // S5_SUB
S5_NCHUNK = 8


def _s5_kernel(up_ref, usf_ref, usr_ref, wb_ref, wc_ref, are_ref, aim_ref, h0r_ref, h0i_ref,
               yf_ref, yr_ref, fin_ref,
               hre_ref, him_ref, ab_re, ab_im, xre_ref, xim_ref):
    k = pl.program_id(0)

    @pl.when(k < 4)
    def _():
        hre_ref[...] = jnp.zeros_like(hre_ref)
        him_ref[...] = jnp.zeros_like(him_ref)

    @pl.when(k == 4)
    def _():
        hre_ref[...] = h0r_ref[...]
        him_ref[...] = h0i_ref[...]

    for d, (u_ref, y_ref) in enumerate(((usf_ref, yf_ref), (usr_ref, yr_ref))):
        reverse = d == 1
        ab_re[...] = jnp.broadcast_to(are_ref[d], (8, SSM_STATE))
        ab_im[...] = jnp.broadcast_to(aim_ref[d], (8, SSM_STATE))

        def sub(s, carry, d=d, reverse=reverse, u_ref=u_ref, y_ref=y_ref):
            ss = (S5_NSUB - 1 - s) if reverse else s
            rows = pl.ds(pl.multiple_of(ss * S5_SUB, S5_SUB), S5_SUB)
            u2 = jnp.where(k < 4, up_ref[rows], u_ref[rows]).reshape(S5_SUB * 8, D_SSM).astype(BF16)
            for half in range(2):
                uh = u2[:, half * 256:(half + 1) * 256]
                cols = slice(half * 1024, (half + 1) * 1024)
                xre_ref[:, cols] = jnp.dot(uh, wb_ref[d, 0, half], preferred_element_type=F32)
                xim_ref[:, cols] = jnp.dot(uh, wb_ref[d, 1, half], preferred_element_type=F32)

            def step(t, h, reverse=reverse):
                tt = (S5_SUB - 1 - t) if reverse else t
                r = pl.ds(pl.multiple_of(tt * 8, 8), 8)
                h_re, h_im = h
                a_re = ab_re[...]
                a_im = ab_im[...]
                n_re = a_re * h_re - a_im * h_im + xre_ref[r, :]
                n_im = a_re * h_im + a_im * h_re + xim_ref[r, :]
                xre_ref[r, :] = n_re
                xim_ref[r, :] = n_im
                return n_re, n_im

            h_re, h_im = lax.fori_loop(0, S5_SUB, step, (hre_ref[d], him_ref[d]))
            hre_ref[d] = h_re
            him_ref[d] = h_im

            @pl.when((s == 0) & (k < 4))
            def _(d=d, reverse=reverse):
                r0 = (S5_SUB - 1) * 8 if reverse else 0
                fin_ref[d, 0] = xre_ref[r0:r0 + 8, :]
                fin_ref[d, 1] = xim_ref[r0:r0 + 8, :]

            x_re = xre_ref[...].astype(BF16)
            x_im = xim_ref[...].astype(BF16)
            for q in range(4):
                ks = slice(q * 512, (q + 1) * 512)
                yq = (jnp.dot(x_re[:, ks], wc_ref[d, 0, q], preferred_element_type=F32)
                      + jnp.dot(x_im[:, ks], wc_ref[d, 1, q], preferred_element_type=F32))
                y_ref[rows, :, q * 128:(q + 1) * 128] = yq.reshape(S5_SUB, 8, 128)
            return carry

        lax.fori_loop(0, S5_NSUB, sub, 0)


def _s5(u_p, u_s, wb, wc, a_re, a_im, h0_re, h0_im, l):
    def rev_chunk(k):
        return jnp.where(k < 4, k, 11 - k)

    blk = (None, S5_CHUNK, 8, D_SSM)
    y_shape = jax.ShapeDtypeStruct((S5_NCHUNK, S5_CHUNK, 8, D_SSM), F32)
    return pl.pallas_call(
        _s5_kernel,
        grid=(S5_NCHUNK,),
        in_specs=[pl.BlockSpec(blk, lambda k: (jnp.minimum(k, 3), 0, 0, 0)),
                  pl.BlockSpec(blk, lambda k: (jnp.maximum(k - 4, 0), 0, 0, 0)),
                  pl.BlockSpec(blk, lambda k: (jnp.minimum(7 - k, 3), 0, 0, 0)),
                  pl.BlockSpec((None, 2, 2, 2, 256, 1024), lambda k: (l, 0, 0, 0, 0, 0)),
                  pl.BlockSpec((None, 2, 2, 4, 512, 128), lambda k: (l, 0, 0, 0, 0, 0)),
                  pl.BlockSpec((None, 2, 1, SSM_STATE), lambda k: (l, 0, 0, 0)),
                  pl.BlockSpec((None, 2, 1, SSM_STATE), lambda k: (l, 0, 0, 0)),
                  pl.BlockSpec((2, 8, SSM_STATE), lambda k: (0, 0, 0)),
                  pl.BlockSpec((2, 8, SSM_STATE), lambda k: (0, 0, 0))],
        out_specs=[pl.BlockSpec(blk, lambda k: (k, 0, 0, 0)),
                   pl.BlockSpec(blk, lambda k: (rev_chunk(k), 0, 0, 0)),
                   pl.BlockSpec((None, 2, 2, 8, SSM_STATE), lambda k: (jnp.minimum(k, 3), 0, 0, 0, 0))],
        out_shape=(y_shape, y_shape,
                   jax.ShapeDtypeStruct((4, 2, 2, 8, SSM_STATE), F32)),
        scratch_shapes=[pltpu.VMEM((2, 8, SSM_STATE), F32), pltpu.VMEM((2, 8, SSM_STATE), F32),
                        pltpu.VMEM((8, SSM_STATE), F32), pltpu.VMEM((8, SSM_STATE), F32),
                        pltpu.VMEM((S5_SUB * 8, SSM_STATE), F32), pltpu.VMEM((S5_SUB * 8, SSM_STATE), F32)],
        compiler_params=_cp(("arbitrary",), VMEM_LIMIT),
        name="s5_scan",
    )(u_p, u_s, u_s, wb, wc, a_re, a_im, h0_re, h0_im)


GLU_TM = 512


def _ssm_glu_kernel(u_ref, ysp_ref, yss_ref, d_ref, w_ref, b_ref, o_ref, w_b):
    i = pl.program_id(0)

    @pl.when(i == 0)
    def _():
        w_b[...] = w_ref[...].astype(BF16)

    ys = jnp.where(i < T_P // GLU_TM, ysp_ref[...], yss_ref[...])
    y = jax.nn.gelu(d_ref[...] * u_ref[...] + ys).astype(BF16)
    glu = jnp.dot(y, w_b[...], preferred_element_type=F32) + b_ref[...]
    o_ref[...] = (glu[:, :D_MODEL] * _sigmoid(glu[:, D_MODEL:])).astype(o_ref.dtype)


def _ssm_glu(proj, ys_p, ys_s, ssm_d3, w_glu, b_glu3, l):
    nbp = T_P // GLU_TM
    return pl.pallas_call(
        _ssm_glu_kernel,
        grid=(T // GLU_TM,),
        in_specs=[pl.BlockSpec((GLU_TM, D_SSM), lambda i: (i, 0)),
                  pl.BlockSpec((GLU_TM, D_SSM), lambda i: (jnp.minimum(i, nbp - 1), 0)),
                  pl.BlockSpec((GLU_TM, D_SSM), lambda i: (jnp.maximum(i - nbp, 0), 0)),
                  pl.BlockSpec((None, 1, D_SSM), lambda i: (l, 0, 0)),
                  pl.BlockSpec((None, D_SSM, 2 * D_MODEL), lambda i: (l, 0, 0)),
                  pl.BlockSpec((None, 1, 2 * D_MODEL), lambda i: (l, 0, 0))],
        out_specs=pl.BlockSpec((GLU_TM, D_MODEL), lambda i: (i, 0)),
        out_shape=jax.ShapeDtypeStruct((T, D_MODEL), BRANCH_DTYPE),
        scratch_shapes=[pltpu.VMEM((D_SSM, 2 * D_MODEL), BF16)],
        compiler_params=_cp(("arbitrary",), VMEM_LIMIT),
        name="ssm_glu",
    )(proj, ys_p, ys_s, ssm_d3, w_glu, b_glu3)


MLA_TM = DEC_SEQ


def _mla_prep_kernel(cq_ref, ckv_ref, kr_ref, qg_ref, kvg_ref, wqa_ref, wqb_ref,
                     ca_ref, sb_ref, ck_ref, sk_ref,
                     q_ref, ckvn_ref, krr_ref, wqa_b, wqb_b):
    @pl.when(pl.program_id(0) == 0)
    def _():
        wqa_b[...] = wqa_ref[...].astype(BF16)
        wqb_b[...] = wqb_ref[...].astype(BF16)

    cq = cq_ref[...]
    cqn = (cq * lax.rsqrt(jnp.mean(cq * cq, axis=-1, keepdims=True) + EPS) * qg_ref[...]).astype(BF16)
    qa = jnp.dot(cqn, wqa_b[...], preferred_element_type=F32)
    qb = jnp.dot(cqn, wqb_b[...], preferred_element_type=F32)
    ca = ca_ref[...]
    sb = sb_ref[...]
    for h in range(N_HEADS):
        cols = slice(h * 128, (h + 1) * 128)
        q_ref[:, cols] = (qa[:, cols] * ca + qb[:, cols] * sb).astype(q_ref.dtype)

    ckv = ckv_ref[...]
    ckvn_ref[...] = ckv * lax.rsqrt(jnp.mean(ckv * ckv, axis=-1, keepdims=True) + EPS) * kvg_ref[...]

    kr = kr_ref[...]
    krr_ref[...] = kr * ck_ref[...] + pltpu.roll(kr, 128 - ROPE_DIM, 1) * sk_ref[...]


def _mla_prep(proj, q_norm_g3, kv_norm_g3, wqa, wqb, tabs, l):
    ca, sb, ck, sk = tabs
    tab_spec = pl.BlockSpec((MLA_TM, 128), lambda i: (jnp.where(i < T_P // MLA_TM, 1, 0), 0))
    return pl.pallas_call(
        _mla_prep_kernel,
        grid=(T // MLA_TM,),
        in_specs=[pl.BlockSpec((MLA_TM, Q_LORA), lambda i: (i, 2)),
                  pl.BlockSpec((MLA_TM, 128), lambda i: (i, 6)),
                  pl.BlockSpec((MLA_TM, 128), lambda i: (i, 7)),
                  pl.BlockSpec((None, 1, Q_LORA), lambda i: (l, 0, 0)),
                  pl.BlockSpec((None, 1, KV_LORA), lambda i: (l, 0, 0)),
                  pl.BlockSpec((None, Q_LORA, 1024), lambda i: (l, 0, 0)),
                  pl.BlockSpec((None, Q_LORA, 1024), lambda i: (l, 0, 0)),
                  tab_spec, tab_spec, tab_spec, tab_spec],
        out_specs=[pl.BlockSpec((MLA_TM, 1024), lambda i: (i, 0)),
                   pl.BlockSpec((MLA_TM, 128), lambda i: (i, 0)),
                   pl.BlockSpec((MLA_TM, 128), lambda i: (i, 0))],
        out_shape=(jax.ShapeDtypeStruct((T, 1024), BF16),
                   jax.ShapeDtypeStruct((T, 128), F32),
                   jax.ShapeDtypeStruct((T, 128), F32)),
        scratch_shapes=[pltpu.VMEM((Q_LORA, 1024), BF16), pltpu.VMEM((Q_LORA, 1024), BF16)],
        compiler_params=_cp(("arbitrary",)),
        name="mla_prep",
    )(proj, proj, proj, q_norm_g3, kv_norm_g3, wqa, wqb, ca, sb, ck, sk)


ATT_TQ = 512


def _attn_kernel(q_ref, kvin_ref, wk_ref, wv_ref, wo_ref, o_ref,
                 k_scr, v_scr, o_scr, wk_b, wv_b, wo_b):
    s_id = pl.program_id(0)
    qb = pl.program_id(1)

    @pl.when((s_id == 0) & (qb == 0))
    def _():
        wk_b[...] = wk_ref[...].astype(BF16)
        wv_b[...] = wv_ref[...].astype(BF16)
        wo_b[...] = wo_ref[...].astype(BF16)

    @pl.when(qb == 0)
    def _():
        kv = kvin_ref[...].astype(BF16)
        k_scr[...] = jnp.dot(kv, wk_b[...], preferred_element_type=F32).astype(BF16)
        v_scr[...] = jnp.dot(kv[:, :KV_LORA], wv_b[...], preferred_element_type=F32).astype(BF16)

    scale = (NOPE_DIM + ROPE_DIM) ** -0.5
    for h in range(N_HEADS):
        qh = q_ref[:, h * 128:(h + 1) * 128]
        kh = k_scr[:, h * 128:(h + 1) * 128]
        s = lax.dot_general(qh, kh, (((1,), (1,)), ((), ())), preferred_element_type=F32) * scale
        m = jnp.max(s, axis=-1, keepdims=True)
        p = jnp.exp(s - m)
        den = jnp.sum(p, axis=-1, keepdims=True)
        oh = jnp.dot(p.astype(BF16), v_scr[:, h * V_DIM:(h + 1) * V_DIM], preferred_element_type=F32)
        o_scr[:, h * V_DIM:(h + 1) * V_DIM] = oh / den
    o_ref[...] = jnp.dot(o_scr[...].astype(BF16), wo_b[...], preferred_element_type=F32).astype(o_ref.dtype)


def _attn(q, kvin, wk2, wv, w_o, l, nseq, nqb, tq, row_blk0):
    lk = kvin.shape[1]
    return pl.pallas_call(
        _attn_kernel,
        grid=(nseq, nqb),
        in_specs=[pl.BlockSpec((tq, 1024), lambda s, b: (row_blk0 + s * nqb + b, 0)),
                  pl.BlockSpec((None, lk, 256), lambda s, b: (s, 0, 0)),
                  pl.BlockSpec((None, 256, 1024), lambda s, b: (l, 0, 0)),
                  pl.BlockSpec((None, KV_LORA, 512), lambda s, b: (l, 0, 0)),
                  pl.BlockSpec((None, 512, D_MODEL), lambda s, b: (l, 0, 0))],
        out_specs=pl.BlockSpec((tq, D_MODEL), lambda s, b: (s * nqb + b, 0)),
        out_shape=jax.ShapeDtypeStruct((nseq * nqb * tq, D_MODEL), BRANCH_DTYPE),
        scratch_shapes=[pltpu.VMEM((lk, 1024), BF16), pltpu.VMEM((lk, 512), BF16),
                        pltpu.VMEM((tq, 512), F32),
                        pltpu.VMEM((256, 1024), BF16), pltpu.VMEM((KV_LORA, 512), BF16),
                        pltpu.VMEM((512, D_MODEL), BF16)],
        compiler_params=_cp(("arbitrary", "arbitrary"), VMEM_LIMIT),
        name="attn",
    )(q, kvin, wk2, wv, w_o)


CONV_PAD = 16
CONV_CB = 128
CONV_RB = 128


def _conv_kernel(cin_ref, w_ref, b_ref, lg_ref, lb_ref, pw_ref, pb_ref, o_ref,
                 pad_scr, conv_scr, act_scr, pw_b, *, seq):
    @pl.when(pl.program_id(0) == 0)
    def _():
        pw_b[...] = pw_ref[...].astype(BF16)
        pad_scr[0:CONV_PAD, :] = jnp.zeros((CONV_PAD, D_CONV), F32)
        pad_scr[CONV_PAD + seq:, :] = jnp.zeros((CONV_PAD, D_CONV), F32)

    pad_scr[CONV_PAD:CONV_PAD + seq, :] = cin_ref[:, :D_CONV] * _sigmoid(cin_ref[:, D_CONV:])

    win_rows = CONV_CB + 2 * CONV_PAD
    for c in range(D_CONV // 128):
        lanes = slice(c * 128, (c + 1) * 128)

        def cblk(rb, carry, lanes=lanes):
            base = pl.multiple_of(rb * CONV_CB, CONV_CB)
            win = pad_scr[pl.ds(base, win_rows), lanes]
            acc = jnp.broadcast_to(b_ref[:, lanes], (CONV_CB, 128))
            for b in range(8):
                wb = win if b == 0 else pltpu.roll(win, win_rows - b, 0)
                for a in range(2 * CONV_PAD // 8):
                    k = 8 * a + b - (CONV_PAD - CONV_K // 2)
                    if 0 <= k < CONV_K:
                        acc = acc + w_ref[k:k + 1, lanes] * wb[8 * a:8 * a + CONV_CB]
            conv_scr[pl.ds(base, CONV_CB), lanes] = acc
            return carry

        lax.fori_loop(0, seq // CONV_CB, cblk, 0)

    def blk(rb, carry):
        base = pl.multiple_of(rb * CONV_RB, CONV_RB)
        acc = conv_scr[pl.ds(base, CONV_RB), :]
        mu = jnp.mean(acc, axis=-1, keepdims=True)
        xc = acc - mu
        var = jnp.mean(xc * xc, axis=-1, keepdims=True)
        y = xc * lax.rsqrt(var + EPS) * lg_ref[...] + lb_ref[...]
        act_scr[pl.ds(base, CONV_RB), :] = (y * _sigmoid(y)).astype(BF16)
        return carry

    lax.fori_loop(0, seq // CONV_RB, blk, 0)
    o_ref[...] = (jnp.dot(act_scr[...], pw_b[...], preferred_element_type=F32) + pb_ref[...]).astype(o_ref.dtype)


def _conv(proj, dw_w, dw_b3, ln_g3, ln_b3, pw, pb3, l, nseq, seq, row_blk0):
    vec = lambda n: pl.BlockSpec((None, 1, n), lambda s: (l, 0, 0))
    return pl.pallas_call(
        functools.partial(_conv_kernel, seq=seq),
        grid=(nseq,),
        in_specs=[pl.BlockSpec((seq, 2 * D_CONV), lambda s: (row_blk0 + s, 1)),
                  pl.BlockSpec((None, CONV_K, D_CONV), lambda s: (l, 0, 0)),
                  vec(D_CONV), vec(D_CONV), vec(D_CONV),
                  pl.BlockSpec((None, D_CONV, D_MODEL), lambda s: (l, 0, 0)),
                  vec(D_MODEL)],
        out_specs=pl.BlockSpec((seq, D_MODEL), lambda s: (s, 0)),
        out_shape=jax.ShapeDtypeStruct((nseq * seq, D_MODEL), BRANCH_DTYPE),
        scratch_shapes=[pltpu.VMEM((seq + 2 * CONV_PAD, D_CONV), F32),
                        pltpu.VMEM((seq, D_CONV), F32),
                        pltpu.VMEM((seq, D_CONV), BF16),
                        pltpu.VMEM((D_CONV, D_MODEL), BF16)],
        compiler_params=_cp(("arbitrary",), VMEM_LIMIT),
        name="conv",
    )(proj, dw_w, dw_b3, ln_g3, ln_b3, pw, pb3)


def _merge_kernel(xa_ref, xb_ref, os_ref, omp_ref, oms_ref, ocp_ref, ocs_ref, ga_ref, gb_ref, gc_ref,
                  wo_ref, mod_ref, ng_ref, rw_ref, rb_ref,
                  x1_ref, h2_ref, tw_ref, sc_ref, sr_ref, nb_ref, wo_b, rw_hi, rw_lo, *, n_a):
    i = pl.program_id(0)

    @pl.when(i == 0)
    def _():
        wo_b[...] = wo_ref[...].astype(BF16)
        rw = rw_ref[...]
        rw_hi[...] = rw.astype(BF16)
        rw_lo[...] = (rw - rw_hi[...].astype(F32)).astype(BF16)

    is_prompt = i < NTB_P
    o_mla = jnp.where(is_prompt, omp_ref[...], oms_ref[...])
    o_conv = jnp.where(is_prompt, ocp_ref[...], ocs_ref[...])
    merged = (_sigmoid(ga_ref[...]) * os_ref[...]
              + _sigmoid(gb_ref[...]) * o_mla
              + _sigmoid(gc_ref[...]) * o_conv)
    out = jnp.dot(merged.astype(BF16), wo_b[...], preferred_element_type=F32)
    x1 = jnp.where(i < n_a, xa_ref[...], xb_ref[...]) + mod_ref[2:3, :] * out
    x1_ref[...] = x1
    y = x1 * lax.rsqrt(jnp.mean(x1 * x1, axis=-1, keepdims=True) + EPS) * ng_ref[...]
    h2 = y * (1.0 + mod_ref[4:5, :]) + mod_ref[3:4, :]
    h2_hi = h2.astype(BF16)
    h2_ref[...] = h2_hi

    h2_lo = (h2 - h2_hi.astype(F32)).astype(BF16)
    logits = (jnp.dot(h2_hi, rw_hi[...], preferred_element_type=F32)
              + jnp.dot(h2_lo, rw_hi[...], preferred_element_type=F32)
              + jnp.dot(h2_hi, rw_lo[...], preferred_element_type=F32)) + rb_ref[...]
    lane_i = lax.broadcasted_iota(jnp.int32, logits.shape, 1)
    lane = lane_i.astype(F32)
    cur = logits
    vals, idxs = [], []
    for _ in range(TOP_K):
        m = jnp.max(cur, axis=-1, keepdims=True)
        idx = jnp.min(jnp.where(cur == m, lane, 128.0), axis=-1, keepdims=True)
        vals.append(m)
        idxs.append(idx)
        cur = jnp.where(lane == idx, -jnp.inf, cur)
    exps = [jnp.exp(v - vals[0]) for v in vals]
    tot = exps[0] + exps[1] + exps[2] + exps[3]
    tw = jnp.zeros(logits.shape, F32)
    for k in range(TOP_K):
        tw = jnp.where(lane_i == k, exps[k] / tot, tw)
    tw_ref[...] = tw

    ohs = [jnp.where(lane == idxs[k], 1.0, 0.0) for k in range(TOP_K)]
    cnts = [jnp.sum(oh, axis=0, keepdims=True) for oh in ohs]
    n_b = jnp.ceil((cnts[0] + cnts[1] + cnts[2] + cnts[3]) * (1.0 / SEG_ALIGN)) * SEG_ALIGN
    inc = jnp.broadcast_to(n_b, (8, 128))
    lane8 = lax.broadcasted_iota(jnp.int32, (8, 128), 1)
    for s in (1, 2, 4, 8, 16, 32, 64):
        inc = inc + jnp.where(lane8 >= s, pltpu.roll(inc, s, 1), 0.0)
    before = (inc - n_b)[0:1, :]
    r_i = lax.broadcasted_iota(jnp.int32, (TB, TB), 0)
    c_i = lax.broadcasted_iota(jnp.int32, (TB, TB), 1)
    earlier = jnp.where(r_i > c_i, 1.0, 0.0).astype(BF16)
    slots = jnp.zeros(logits.shape, F32)
    for k in range(TOP_K):
        pre = jnp.dot(earlier, ohs[k].astype(BF16), preferred_element_type=F32)
        slot_k = jnp.sum(ohs[k] * (pre + before), axis=-1, keepdims=True)
        slots = jnp.where(lane_i == k, slot_k, slots)
        before = before + cnts[k]
    sc_ref[...] = slots.astype(jnp.int32)
    sr_ref[...] = slots.T[0:8, :].astype(jnp.int32)
    nb_ref[...] = jnp.broadcast_to(n_b, (8, 128)).astype(jnp.int32)


def _merge(xa, xb, o_ssm, o_mla_p, o_mla_s, o_conv_p, o_conv_s, proj, w_out, modt, norm_ffn_g3, rw_p, rb_p, l):
    n_a = xa.shape[0] // TB if xb is not xa else NTB
    row = lambda i: (i, 0)
    row_p = lambda i: (jnp.minimum(i, NTB_P - 1), 0)
    row_s = lambda i: (jnp.maximum(i - NTB_P, 0), 0)
    blk = lambda m: pl.BlockSpec((TB, D_MODEL), m)
    return pl.pallas_call(
        functools.partial(_merge_kernel, n_a=n_a),
        grid=(NTB,),
        in_specs=[blk(lambda i: (jnp.minimum(i, n_a - 1), 0)), blk(lambda i: (jnp.maximum(i - n_a, 0), 0)),
                  blk(row), blk(row_p), blk(row_s), blk(row_p), blk(row_s),
                  blk(lambda i: (i, 2)), blk(lambda i: (i, 3)), blk(lambda i: (i, 4)),
                  pl.BlockSpec((None, D_MODEL, D_MODEL), lambda i: (l, 0, 0)),
                  pl.BlockSpec((None, 6, D_MODEL), lambda i: (_mod_row(i), 0, 0)),
                  pl.BlockSpec((None, 1, D_MODEL), lambda i: (l, 0, 0)),
                  pl.BlockSpec((None, D_MODEL, 128), lambda i: (l, 0, 0)),
                  pl.BlockSpec((None, 1, 128), lambda i: (l, 0, 0))],
        out_specs=[blk(row), blk(row), pl.BlockSpec((TB, 128), row), pl.BlockSpec((TB, 128), row),
                   pl.BlockSpec((None, 8, TB), lambda i: (i, 0, 0)),
                   pl.BlockSpec((None, 8, 128), lambda i: (i, 0, 0))],
        out_shape=(jax.ShapeDtypeStruct((T, D_MODEL), F32), jax.ShapeDtypeStruct((T, D_MODEL), BF16),
                   jax.ShapeDtypeStruct((T, 128), F32), jax.ShapeDtypeStruct((T, 128), jnp.int32),
                   jax.ShapeDtypeStruct((NTB, 8, TB), jnp.int32),
                   jax.ShapeDtypeStruct((NTB, 8, 128), jnp.int32)),
        scratch_shapes=[pltpu.VMEM((D_MODEL, D_MODEL), BF16),
                        pltpu.VMEM((D_MODEL, 128), BF16), pltpu.VMEM((D_MODEL, 128), BF16)],
        compiler_params=_cp(("arbitrary",), VMEM_LIMIT),
        name="merge",
    )(xa, xb, o_ssm, o_mla_p, o_mla_s, o_conv_p, o_conv_s, proj, proj, proj, w_out, modt, norm_ffn_g3, rw_p, rb_p)


PAIRS = TB * TOP_K
G_ROWS = -(-(PAIRS + (SEG_ALIGN - 1) * N_EXPERTS) // 256) * 256
SEG_BITS = 9


def _aligned(off):
    return off if isinstance(off, int) else pl.multiple_of(off, SEG_ALIGN)


def _segment_copies(src_ref, dst_ref, sem, n, src_off, dst_off):
    copies = []
    done = 0
    for bit in range(SEG_BITS - 1, SEG_ALIGN.bit_length() - 2, -1):
        size = 1 << bit
        part = n & size
        copies.append((part != 0,
                       pltpu.make_async_copy(src_ref.at[pl.ds(_aligned(src_off + done), size)],
                                             dst_ref.at[pl.ds(_aligned(dst_off + done), size)], sem)))
        done = done + part
    return copies


def _start_all(copies):
    for pred, cp in copies:
        pl.when(pred)(cp.start)


def _wait_all(copies):
    for pred, cp in copies:
        pl.when(pred)(cp.wait)


def _dispatch_kernel(seg_ref, nbe_ref, cnt_ref, pst_ref, nu_ref, h_ref, sr_ref, xs_ref, g_scr, z_scr, sem):
    b = pl.program_id(0)
    zsem = sem.at[2]

    def copies(blk, e, off):
        n = nbe_ref[blk * N_EXPERTS + e]
        dst = seg_ref[blk * N_EXPERTS + e]
        slot = blk % 2
        return n, _segment_copies(g_scr.at[slot], xs_ref, sem.at[slot], n, off, dst)

    def drain(blk):
        def wait(e, off):
            n, cps = copies(blk, e, off)
            _wait_all(cps)
            return off + n

        lax.fori_loop(0, N_EXPERTS, wait, 0)

    @pl.when(b < NTB)
    def _():
        sub = lax.broadcasted_iota(jnp.int32, (G_ROWS, TB), 0)
        sr = sr_ref[...]
        hit = sub == sr[0:1, :]
        for k in range(1, TOP_K):
            hit = hit | (sub == sr[k:k + 1, :])
        g_scr[b % 2] = jnp.dot(jnp.where(hit, 1.0, 0.0).astype(BF16), h_ref[...], preferred_element_type=F32)

        def start(e, off):
            n, cps = copies(b, e, off)
            _start_all(cps)
            return off + n

        lax.fori_loop(0, N_EXPERTS, start, 0)

    @pl.when(b > 0)
    def _():
        drain(b - 1)

    @pl.when(b == NTB)
    def _():
        sem = zsem
        z_scr[...] = jnp.zeros_like(z_scr)

        def pad(e):
            cnt = cnt_ref[e]
            padded = (cnt + MOE_BM - 1) // MOE_BM * MOE_BM
            return padded - cnt, pst_ref[e] + cnt

        def start(e, carry):
            n, dst = pad(e)
            _start_all(_segment_copies(z_scr, xs_ref, sem, n, 0, dst))
            return carry

        def wait(e, carry):
            n, dst = pad(e)
            _wait_all(_segment_copies(z_scr, xs_ref, sem, n, 0, dst))
            return carry

        lax.fori_loop(0, N_EXPERTS, start, 0)
        lax.fori_loop(0, N_EXPERTS, wait, 0)

        def tail_copy(blk):
            return pltpu.make_async_copy(z_scr, xs_ref.at[pl.ds(blk * MOE_BM, MOE_BM)], sem)

        def tail_start(blk, carry):
            tail_copy(blk).start()
            return carry

        def tail_wait(blk, carry):
            tail_copy(blk).wait()
            return carry

        lax.fori_loop(nu_ref[0], MOE_NB, tail_start, 0)
        lax.fori_loop(nu_ref[0], MOE_NB, tail_wait, 0)


def _dispatch(seg_start, n_be, counts, pad_start, n_used, h2, slot_row):
    last = lambda b, *_: (jnp.minimum(b, NTB - 1), 0)
    grid_spec = pltpu.PrefetchScalarGridSpec(
        num_scalar_prefetch=5,
        grid=(NTB + 1,),
        in_specs=[pl.BlockSpec((TB, D_MODEL), last),
                  pl.BlockSpec((None, 8, TB), lambda b, *_: (jnp.minimum(b, NTB - 1), 0, 0))],
        out_specs=pl.BlockSpec(memory_space=pl.ANY),
        scratch_shapes=[pltpu.VMEM((2, G_ROWS, D_MODEL), F32),
                        pltpu.VMEM((MOE_BM, D_MODEL), F32),
                        pltpu.SemaphoreType.DMA((3,))],
    )
    return pl.pallas_call(
        _dispatch_kernel,
        grid_spec=grid_spec,
        out_shape=jax.ShapeDtypeStruct((MOE_ROWS, D_MODEL), F32),
        compiler_params=_cp(("arbitrary",), VMEM_LIMIT),
        name="moe_dispatch",
    )(seg_start, n_be, counts, pad_start, n_used, h2, slot_row)


def _expert_kernel(be_ref, nu_ref, x_ref, w1_ref, b1_ref, w2_ref, b2_ref, y_ref, w1_b, w2_b):
    i = pl.program_id(0)
    prev = be_ref[jnp.maximum(i - 1, 0)]

    @pl.when((i == 0) | (be_ref[i] != prev))
    def _():
        w1_b[...] = w1_ref[...].astype(BF16)
        w2_b[...] = w2_ref[...].astype(BF16)

    @pl.when(i < nu_ref[0])
    def _():
        gu = jnp.dot(x_ref[...].astype(BF16), w1_b[...], preferred_element_type=F32) + b1_ref[...]
        g = jnp.minimum(gu[:, :D_FF], SWIGLU_LIMIT)
        u = jnp.clip(gu[:, D_FF:], -SWIGLU_LIMIT, SWIGLU_LIMIT)
        act = (u + 1.0) * (g * _sigmoid(SWIGLU_ALPHA * g))
        y_ref[...] = jnp.dot(act.astype(BF16), w2_b[...], preferred_element_type=F32) + b2_ref[...]

    @pl.when(i >= nu_ref[0])
    def _():
        y_ref[...] = jnp.zeros_like(y_ref)


def _experts(block_e, n_used, xs, w1, b1_4, w2, b2_4, l):
    grid_spec = pltpu.PrefetchScalarGridSpec(
        num_scalar_prefetch=2,
        grid=(MOE_NB,),
        in_specs=[pl.BlockSpec((MOE_BM, D_MODEL), lambda i, be, nu: (jnp.minimum(i, nu[0] - 1), 0)),
                  pl.BlockSpec((None, None, D_MODEL, 2 * D_FF), lambda i, be, nu: (l, be[i], 0, 0)),
                  pl.BlockSpec((None, None, 1, 2 * D_FF), lambda i, be, nu: (l, be[i], 0, 0)),
                  pl.BlockSpec((None, None, D_FF, D_MODEL), lambda i, be, nu: (l, be[i], 0, 0)),
                  pl.BlockSpec((None, None, 1, D_MODEL), lambda i, be, nu: (l, be[i], 0, 0))],
        out_specs=pl.BlockSpec((MOE_BM, D_MODEL), lambda i, be, nu: (i, 0)),
        scratch_shapes=[pltpu.VMEM((D_MODEL, 2 * D_FF), BF16), pltpu.VMEM((D_FF, D_MODEL), BF16)],
    )
    return pl.pallas_call(
        _expert_kernel,
        grid_spec=grid_spec,
        out_shape=jax.ShapeDtypeStruct((MOE_ROWS, D_MODEL), F32),
        compiler_params=_cp(("arbitrary",), VMEM_LIMIT),
        name="moe_experts",
    )(block_e, n_used, xs, w1, b1_4, w2, b2_4)


def _combine_kernel(seg_ref, nbe_ref, x1_ref, tw_ref, sc_ref, mod_ref, fg_ref, ys_ref, *rest, final):
    if final:
        yp_ref, ys_out_ref, g_scr, sem = rest
    else:
        x2_ref, g_scr, sem = rest
    b = pl.program_id(0)

    def copies(blk, e, off):
        n = nbe_ref[blk * N_EXPERTS + e]
        src = seg_ref[blk * N_EXPERTS + e]
        slot = blk % 2
        return n, _segment_copies(ys_ref, g_scr.at[slot], sem.at[slot], n, src, off)

    def fetch(blk):
        g_scr[blk % 2, PAIRS:, :] = jnp.zeros((G_ROWS - PAIRS, D_MODEL), F32)

        def start(e, off):
            n, cps = copies(blk, e, off)
            _start_all(cps)
            return off + n

        lax.fori_loop(0, N_EXPERTS, start, 0)

    @pl.when(b == 0)
    def _():
        fetch(b)

    @pl.when(b + 1 < NTB)
    def _():
        fetch(b + 1)

    def wait(e, off):
        n, cps = copies(b, e, off)
        _wait_all(cps)
        return off + n

    lax.fori_loop(0, N_EXPERTS, wait, 0)

    sc = sc_ref[...]
    tw = tw_ref[...]
    lane = lax.broadcasted_iota(jnp.int32, (TB, G_ROWS), 1)
    pw = jnp.zeros((TB, G_ROWS), F32)
    for k in range(TOP_K):
        pw = pw + jnp.where(lane == sc[:, k:k + 1], tw[:, k:k + 1], 0.0)
    moe = jnp.dot(pw.astype(BF16), g_scr[b % 2].astype(BF16), preferred_element_type=F32)
    x2 = x1_ref[...] + mod_ref[5:6, :] * moe
    if final:
        yn = x2 * lax.rsqrt(jnp.mean(x2 * x2, axis=-1, keepdims=True) + EPS) * fg_ref[...]

        @pl.when(b < NTB_P)
        def _():
            yp_ref[...] = yn

        @pl.when(b >= NTB_P)
        def _():
            ys_out_ref[...] = yn
    else:
        x2_ref[...] = x2


def _combine(seg_start, n_be, x1, topw, slot_col, modt, final_g2, ys, final):
    row = lambda b, *_: (b, 0)
    out_blk = pl.BlockSpec((TB, D_MODEL), row)
    out_sds = jax.ShapeDtypeStruct((T, D_MODEL), F32)
    fin_blks = [pl.BlockSpec((TB, D_MODEL), lambda b, *_: (jnp.minimum(b, NTB_P - 1), 0)),
                pl.BlockSpec((TB, D_MODEL), lambda b, *_: (jnp.maximum(b - NTB_P, 0), 0))]
    fin_sds = (jax.ShapeDtypeStruct((T_P, D_MODEL), F32), jax.ShapeDtypeStruct((T_S, D_MODEL), F32))
    grid_spec = pltpu.PrefetchScalarGridSpec(
        num_scalar_prefetch=2,
        grid=(NTB,),
        in_specs=[pl.BlockSpec((TB, D_MODEL), row),
                  pl.BlockSpec((TB, 128), row),
                  pl.BlockSpec((TB, 128), row),
                  pl.BlockSpec((None, 6, D_MODEL), lambda b, *_: (_mod_row(b), 0, 0)),
                  pl.BlockSpec((1, D_MODEL), lambda b, *_: (0, 0)),
                  pl.BlockSpec(memory_space=pl.ANY)],
        out_specs=fin_blks if final else [out_blk],
        scratch_shapes=[pltpu.VMEM((2, G_ROWS, D_MODEL), F32), pltpu.SemaphoreType.DMA((2,))],
    )
    return pl.pallas_call(
        functools.partial(_combine_kernel, final=final),
        grid_spec=grid_spec,
        out_shape=fin_sds if final else (out_sds,),
        compiler_params=_cp(("arbitrary",), VMEM_LIMIT),
        name="moe_combine",
    )(seg_start, n_be, x1, topw, slot_col, modt, final_g2, ys)


def _rot_cols(w):
    q = ROPE_DIM // 4
    parts = []
    for half in range(2):
        a = w[..., half * 2 * q:half * 2 * q + q]
        b = w[..., half * 2 * q + q:(half + 1) * 2 * q]
        parts += [-b, a]
    return jnp.concatenate(parts, axis=-1)


def _rope_tables():
    rows = DEC_SEQ // GRID_W
    row = jnp.repeat(jnp.arange(rows), GRID_W).astype(F32)
    col = jnp.tile(jnp.arange(GRID_W), rows).astype(F32)
    half = ROPE_DIM // 2
    freqs = ROPE_BASE ** (-jnp.arange(0, half, 2, dtype=F32) / half)
    ang_r = row[:, None] * freqs
    ang_c = col[:, None] * freqs
    ang = jnp.concatenate([ang_r, ang_r, ang_c, ang_c], axis=-1)
    cos = jnp.concatenate([jnp.cos(ang), jnp.ones((MLA_TM, ROPE_DIM), F32)], axis=0)
    sin = jnp.concatenate([jnp.sin(ang), jnp.zeros((MLA_TM, ROPE_DIM), F32)], axis=0)
    n = cos.shape[0]
    z = lambda w: jnp.zeros((n, w), F32)
    ca = jnp.concatenate([jnp.ones((n, NOPE_DIM), F32), cos, z(32)], axis=1)
    sb = jnp.concatenate([z(NOPE_DIM), sin, z(32)], axis=1)
    ck = jnp.concatenate([cos, z(96)], axis=1)
    sk = jnp.concatenate([sin, z(96)], axis=1)
    return ca, sb, ck, sk


def kernel(x_prompt, x_sample, c, cache_kv_latent, cache_k_rope, state_ssm, c_ctx, norm_mix_g, norm_ffn_g, final_norm_g, w_ada, b_ada, w_in, q_norm_g, w_uq, kv_norm_g, w_ukv, w_o_mla, ssm_lam_re, ssm_lam_im, ssm_log_dt, ssm_b_re, ssm_b_im, ssm_c_re, ssm_c_im, ssm_d, w_ssm_glu, b_ssm_glu, conv_dw_w, conv_dw_b, conv_ln_g, conv_ln_b, w_conv_pw2, b_conv_pw2, w_out, router_w, router_b, moe_w1, moe_b1, moe_w2, moe_b2):
    L = DEPTH
    v3 = lambda a: a.reshape(L, 1, a.shape[-1])

    xa, xb = x_prompt.reshape(T_P, D_MODEL), x_sample.reshape(T_S, D_MODEL)
    cc = jnp.concatenate([c, c_ctx[None, :], jnp.zeros((16 - DEC_BATCH - 1, D_MODEL), F32)], axis=0)


    wq = w_uq.reshape(L, Q_LORA, N_HEADS, NOPE_DIM + ROPE_DIM)
    zq = lambda w: jnp.zeros((L, Q_LORA, N_HEADS, w), F32)
    wqa = jnp.concatenate([wq, zq(32)], axis=-1).reshape(L, Q_LORA, 1024)
    wqb = jnp.concatenate([zq(NOPE_DIM), _rot_cols(wq[..., NOPE_DIM:]), zq(32)], axis=-1).reshape(L, Q_LORA, 1024)
    wkv = w_ukv.reshape(L, KV_LORA, N_HEADS, NOPE_DIM + V_DIM)
    wk_lat = jnp.concatenate([wkv[..., :NOPE_DIM], jnp.zeros((L, KV_LORA, N_HEADS, 64), F32)], axis=-1)
    place = jnp.concatenate([jnp.zeros((ROPE_DIM, NOPE_DIM), F32), jnp.eye(ROPE_DIM, dtype=F32),
                             jnp.zeros((ROPE_DIM, 32), F32)], axis=-1)
    wk_rope = jnp.broadcast_to(place[None, :, None, :], (L, ROPE_DIM, N_HEADS, 128))
    wk2 = jnp.concatenate([wk_lat, wk_rope, jnp.zeros((L, 96, N_HEADS, 128), F32)], axis=1).reshape(L, 256, 1024)
    wv = wkv[..., NOPE_DIM:].reshape(L, KV_LORA, N_HEADS * V_DIM)
    tabs = _rope_tables()

    a_re, a_im, f_re, f_im = [v.reshape(L, 2, N_SSM_GROUPS, N_STATE)
                              for v in _s5_disc(ssm_lam_re, ssm_lam_im, ssm_log_dt)]
    bb_re = f_re[..., None] * ssm_b_re - f_im[..., None] * ssm_b_im
    bb_im = f_re[..., None] * ssm_b_im + f_im[..., None] * ssm_b_re
    eye16 = jnp.eye(16, dtype=F32)
    eye8 = jnp.eye(8, dtype=F32)

    def bd_in(bb):
        bb = bb.reshape(L, 2, 2, 16, N_STATE, SSM_GROUP)
        return jnp.einsum('ldhgpc,gk->ldhgckp', bb, eye16).reshape(L, 2, 2, 256, 1024)

    def bd_out(cm):
        cm = cm.reshape(L, 2, 4, 8, SSM_GROUP, N_STATE)
        return jnp.einsum('ldqgcp,gk->ldqgpkc', cm, eye8).reshape(L, 2, 4, 512, 128)

    wb = jnp.stack([bd_in(bb_re), bd_in(bb_im)], axis=2).astype(BF16)
    wc = jnp.stack([bd_out(ssm_c_re), bd_out(-ssm_c_im)], axis=2).astype(BF16)
    a_re4 = a_re.reshape(L, 2, 1, SSM_STATE)
    a_im4 = a_im.reshape(L, 2, 1, SSM_STATE)

    rw_p = jnp.concatenate([router_w, jnp.zeros((L, D_MODEL, 128 - N_EXPERTS), F32)], axis=-1)
    rb_p = jnp.concatenate([router_b, jnp.full((L, 128 - N_EXPERTS), -1e30, F32)], axis=-1).reshape(L, 1, 128)

    kv_out, kr_out, ssm_out = [], [], []
    for l in range(L):
        modt = _ada(cc, w_ada, v3(b_ada), l).reshape(16, 6, D_MODEL)
        proj = _win(xa, xb, v3(norm_mix_g), modt, w_in, l)

        u_p = proj[:T_P, :D_SSM].reshape(4, 8, SEQ, D_SSM).transpose(0, 2, 1, 3)
        u_s = proj[T_P:, :D_SSM].reshape(DEC_BATCH, 4, S5_CHUNK, D_SSM).transpose(1, 2, 0, 3)
        st = state_ssm[:, l]
        h0 = st.transpose(4, 1, 0, 2, 3).reshape(2, 2, DEC_BATCH, SSM_STATE)
        y_f, y_r, fin = _s5(u_p, u_s, wb, wc, a_re4, a_im4, h0[0], h0[1], l)
        ysum = y_f + y_r
        o_ssm = _ssm_glu(proj, ysum[:4].transpose(0, 2, 1, 3).reshape(T_P, D_SSM),
                         ysum[4:].transpose(2, 0, 1, 3).reshape(T_S, D_SSM),
                         v3(ssm_d), w_ssm_glu, v3(b_ssm_glu), l)
        ssm_out.append(fin.reshape(4, 2, 2, 8, N_SSM_GROUPS, N_STATE).transpose(0, 3, 1, 4, 5, 2)
                       .reshape(BATCH, 2, N_SSM_GROUPS, N_STATE, 2))

        q, ckvn, krr = _mla_prep(proj, v3(q_norm_g), v3(kv_norm_g), wqa, wqb, tabs, l)
        kvin_p = jnp.concatenate([ckvn[:T_P], krr[:T_P]], axis=-1).reshape(BATCH, SEQ, 256)
        cache_kr = jnp.concatenate([cache_k_rope[:, l], jnp.zeros((DEC_BATCH, PAST_LEN, 96), F32)], axis=-1)
        kvin_s = jnp.concatenate([
            jnp.concatenate([cache_kv_latent[:, l], cache_kr], axis=-1),
            jnp.concatenate([ckvn[T_P:], krr[T_P:]], axis=-1).reshape(DEC_BATCH, DEC_SEQ, 256)], axis=1)
        o_mla_p = _attn(q, kvin_p, wk2, wv, w_o_mla, l, BATCH, 1, SEQ, 0)
        o_mla_s = _attn(q, kvin_s, wk2, wv, w_o_mla, l, DEC_BATCH, DEC_SEQ // ATT_TQ, ATT_TQ, T_P // ATT_TQ)
        kv_out.append(ckvn[:T_P].reshape(BATCH, SEQ, KV_LORA))
        kr_out.append(proj[:T_P, KR0:KR0 + ROPE_DIM].reshape(BATCH, SEQ, ROPE_DIM))

        conv_args = (conv_dw_w, v3(conv_dw_b), v3(conv_ln_g), v3(conv_ln_b), w_conv_pw2, v3(b_conv_pw2), l)
        o_conv_p = _conv(proj, *conv_args, BATCH, SEQ, 0)
        o_conv_s = _conv(proj, *conv_args, DEC_BATCH, DEC_SEQ, T_P // DEC_SEQ)

        x1, h2, topw, slot_col, slot_row, n_blk = _merge(
            xa, xb, o_ssm, o_mla_p, o_mla_s, o_conv_p, o_conv_s, proj, w_out, modt, v3(norm_ffn_g), rw_p, rb_p, l)

        n_be = n_blk[:, 0, :N_EXPERTS]
        counts = jnp.sum(n_be, axis=0)
        padded = (counts + MOE_BM - 1) // MOE_BM * MOE_BM
        pad_end = jnp.cumsum(padded)
        pad_start = pad_end - padded
        seg_start = (pad_start[None, :] + jnp.cumsum(n_be, axis=0) - n_be).reshape(-1).astype(jnp.int32)
        blk_row = jnp.arange(MOE_NB, dtype=jnp.int32) * MOE_BM
        block_e = jnp.minimum(jnp.sum((pad_end[None, :] <= blk_row[:, None]).astype(jnp.int32), axis=1),
                              N_EXPERTS - 1).astype(jnp.int32)
        n_used = (pad_end[-1:] // MOE_BM).astype(jnp.int32)
        n_be = n_be.reshape(-1)

        xs = _dispatch(seg_start, n_be, counts.astype(jnp.int32), pad_start.astype(jnp.int32), n_used, h2, slot_row)
        ys = _experts(block_e, n_used, xs, moe_w1, moe_b1.reshape(L, N_EXPERTS, 1, 2 * D_FF),
                      moe_w2, moe_b2.reshape(L, N_EXPERTS, 1, D_MODEL), l)
        outs = _combine(seg_start, n_be, x1, topw, slot_col, modt, final_norm_g.reshape(1, D_MODEL), ys,
                        final=(l == L - 1))
        if l < L - 1:
            xa = xb = outs[0]

    y_prompt = outs[0].reshape(BATCH, SEQ, D_MODEL)
    y_sample = outs[1].reshape(DEC_BATCH, DEC_SEQ, D_MODEL)
    return (y_prompt, y_sample, jnp.stack(kv_out, axis=1), jnp.stack(kr_out, axis=1), jnp.stack(ssm_out, axis=1))
```

```python
import functools
import math

import numpy as np
import jax
import jax.numpy as jnp
from jax import lax
from jax.experimental import pallas as pl
from jax.experimental.pallas import tpu as pltpu

F32 = jnp.float32
BF16 = jnp.bfloat16

D_MODEL = 1024
BATCH = 32
SEQ = 256
DEPTH = 2
DEC_BATCH = 8
DEC_SEQ = 1024
PAST_LEN = 256
GRID_W = 64
EPS = 1e-6
D_SSM = 512
SSM_GROUP = 16
N_SSM_GROUPS = 32
N_STATE = 64
N_HEADS = 8
NOPE_DIM = 64
ROPE_DIM = 32
V_DIM = 64
Q_LORA = 256
KV_LORA = 128
ROPE_BASE = 10000.0
D_CONV = 512
CONV_K = 31
N_EXPERTS = 32
TOP_K = 4
D_FF = 1024
SWIGLU_ALPHA = 1.702
SWIGLU_LIMIT = 7.0

T_P = BATCH * SEQ
T_S = DEC_BATCH * DEC_SEQ
T = T_P + T_S
TB = 256
NTB = T // TB
NTB_P = T_P // TB
TB_PER_S = DEC_SEQ // TB
SSM_STATE = N_SSM_GROUPS * N_STATE
PROJ_W = 5120
MOE_BM = 512
SEG_ALIGN = 8
MOE_NB = (T * TOP_K + (SEG_ALIGN - 1) * NTB * N_EXPERTS) // MOE_BM + N_EXPERTS
MOE_ROWS = MOE_NB * MOE_BM
VMEM_LIMIT = 56 * 1024 * 1024
BRANCH_DTYPE = BF16


def _cp(sem, vmem=None):
    return pltpu.CompilerParams(dimension_semantics=sem, vmem_limit_bytes=vmem)


def _sigmoid(x):
    return 0.5 * jnp.tanh(0.5 * x) + 0.5


def _mod_row(i):
    return jnp.where(i < NTB_P, DEC_BATCH, (i - NTB_P) // TB_PER_S)


def _ada_kernel(c_ref, w_ref, b_ref, o_ref):
    c = c_ref[...]
    s = c * jax.nn.sigmoid(c)
    o_ref[...] = jnp.dot(s, w_ref[...], preferred_element_type=F32,
                         precision=lax.Precision.HIGHEST) + b_ref[...]


def _ada(cc, w_ada, b_ada3, l):
    tn = 1024
    return pl.pallas_call(
        _ada_kernel,
        grid=(6 * D_MODEL // tn,),
        in_specs=[pl.BlockSpec((16, D_MODEL), lambda j: (0, 0)),
                  pl.BlockSpec((None, D_MODEL, tn), lambda j: (l, 0, j)),
                  pl.BlockSpec((None, 1, tn), lambda j: (l, 0, j))],
        out_specs=pl.BlockSpec((16, tn), lambda j: (0, j)),
        out_shape=jax.ShapeDtypeStruct((16, 6 * D_MODEL), F32),
        compiler_params=_cp(("arbitrary",)),
        name="ada",
    )(cc, w_ada, b_ada3)


WIN_TM = 1024
WIN_TN = 1024
KR0 = D_SSM + Q_LORA + KV_LORA
IN_W = KR0 + ROPE_DIM + 2 * D_CONV + 3 * D_MODEL
PROJ_SHIFT = PROJ_W - IN_W


def _win_kernel(xa_ref, xb_ref, g_ref, mod_ref, wp_ref, wc_ref, o_ref, wb_ref, *, n_a):
    j = pl.program_id(0)
    i = pl.program_id(1)

    @pl.when((i == 0) & (j == 0))
    def _():
        w = wc_ref[...]
        kr = w[:, KR0:KR0 + ROPE_DIM]
        wb_ref[...] = jnp.concatenate(
            [w[:, :KR0 + ROPE_DIM], _rot_cols(kr), jnp.zeros((D_MODEL, PROJ_SHIFT - ROPE_DIM), F32)],
            axis=1).astype(BF16)

    @pl.when((i == 0) & (j > 0))
    def _():
        wb_ref[...] = jnp.concatenate([wp_ref[:, WIN_TN - PROJ_SHIFT:], wc_ref[:, :WIN_TN - PROJ_SHIFT]],
                                      axis=1).astype(BF16)

    x = jnp.where(i < n_a, xa_ref[...], xb_ref[...])
    y = x * lax.rsqrt(jnp.mean(x * x, axis=-1, keepdims=True) + EPS) * g_ref[...]
    h = (y * (1.0 + mod_ref[1:2, :]) + mod_ref[0:1, :]).astype(BF16)
    o_ref[...] = jnp.dot(h, wb_ref[...], preferred_element_type=F32)


def _win(xa, xb, g3, modt, w_in, l):
    n_a = xa.shape[0] // WIN_TM if xb is not xa else T // WIN_TM
    ni = T // WIN_TM
    nbp = T_P // WIN_TM
    return pl.pallas_call(
        functools.partial(_win_kernel, n_a=n_a),
        grid=(PROJ_W // WIN_TN, ni),
        in_specs=[pl.BlockSpec((WIN_TM, D_MODEL), lambda j, i: (jnp.minimum(i, n_a - 1), 0)),
                  pl.BlockSpec((WIN_TM, D_MODEL), lambda j, i: (jnp.maximum(i - n_a, 0), 0)),
                  pl.BlockSpec((None, 1, D_MODEL), lambda j, i: (l, 0, 0)),
                  pl.BlockSpec((None, 6, D_MODEL),
                               lambda j, i: (jnp.where(i < nbp, DEC_BATCH, i - nbp), 0, 0)),
                  pl.BlockSpec((None, D_MODEL, WIN_TN), lambda j, i: (l, 0, jnp.maximum(j - 1, 0))),
                  pl.BlockSpec((None, D_MODEL, WIN_TN), lambda j, i: (l, 0, j))],
        out_specs=pl.BlockSpec((WIN_TM, WIN_TN), lambda j, i: (i, j)),
        out_shape=jax.ShapeDtypeStruct((T, PROJ_W), F32),
        scratch_shapes=[pltpu.VMEM((D_MODEL, WIN_TN), BF16)],
        compiler_params=_cp(("arbitrary", "arbitrary"), VMEM_LIMIT),
        name="w_in",
    )(xa, xb, g3, modt, w_in, w_in)


def _s5_disc_kernel(lr_ref, li_ref, ldt_ref, ar_ref, ai_ref, fr_ref, fi_ref):
    dt = jnp.exp(ldt_ref[...])
    lr = lr_ref[...]
    li = li_ref[...]
    mag = jnp.exp(lr * dt)
    a_re = mag * jnp.cos(li * dt)
    a_im = mag * jnp.sin(li * dt)
    den = lr * lr + li * li
    ar_ref[...] = a_re
    ai_ref[...] = a_im
    fr_ref[...] = ((a_re - 1.0) * lr + a_im * li) / den
    fi_ref[...] = (a_im * lr - (a_re - 1.0) * li) / den


def _s5_disc(lam_re, lam_im, log_dt):
    rows = DEPTH * 2 * N_SSM_GROUPS
    shp = jax.ShapeDtypeStruct((rows, N_STATE), F32)
    return pl.pallas_call(
        _s5_disc_kernel,
        out_shape=(shp, shp, shp, shp),
        name="s5_disc",
    )(lam_re.reshape(rows, N_STATE), lam_im.reshape(rows, N_STATE), log_dt.reshape(rows, 1))


S5_CHUNK = 256
S5_SUB = 64
S5_NSUB = S5_CHUNK // S5_SUB
S5_NCHUNK = 8


def _s5_kernel(up_ref, usf_ref, usr_ref, wb_ref, wc_ref, are_ref, aim_ref, h0r_ref, h0i_ref,
               yf_ref, yr_ref, fin_ref,
               hre_ref, him_ref, ab_re, ab_im, xre_ref, xim_ref):
    k = pl.program_id(0)

    @pl.when(k < 4)
    def _():
        hre_ref[...] = jnp.zeros_like(hre_ref)
        him_ref[...] = jnp.zeros_like(him_ref)

    @pl.when(k == 4)
    def _():
        hre_ref[...] = h0r_ref[...]
        him_ref[...] = h0i_ref[...]

    for d, (u_ref, y_ref) in enumerate(((usf_ref, yf_ref), (usr_ref, yr_ref))):
        reverse = d == 1
        ab_re[...] = jnp.broadcast_to(are_ref[d], (8, SSM_STATE))
        ab_im[...] = jnp.broadcast_to(aim_ref[d], (8, SSM_STATE))

        def sub(s, carry, d=d, reverse=reverse, u_ref=u_ref, y_ref=y_ref):
            ss = (S5_NSUB - 1 - s) if reverse else s
            rows = pl.ds(pl.multiple_of(ss * S5_SUB, S5_SUB), S5_SUB)
            u2 = jnp.where(k < 4, up_ref[rows], u_ref[rows]).reshape(S5_SUB * 8, D_SSM).astype(BF16)
            for half in range(2):
                uh = u2[:, half * 256:(half + 1) * 256]
                cols = slice(half * 1024, (half + 1) * 1024)
                xre_ref[:, cols] = jnp.dot(uh, wb_ref[d, 0, half], preferred_element_type=F32)
                xim_ref[:, cols] = jnp.dot(uh, wb_ref[d, 1, half], preferred_element_type=F32)

            def step(t, h, reverse=reverse):
                tt = (S5_SUB - 1 - t) if reverse else t
                r = pl.ds(pl.multiple_of(tt * 8, 8), 8)
                h_re, h_im = h
                a_re = ab_re[...]
                a_im = ab_im[...]
                n_re = a_re * h_re - a_im * h_im + xre_ref[r, :]
                n_im = a_re * h_im + a_im * h_re + xim_ref[r, :]
                xre_ref[r, :] = n_re
                xim_ref[r, :] = n_im
                return n_re, n_im

            h_re, h_im = lax.fori_loop(0, S5_SUB, step, (hre_ref[d], him_ref[d]))
            hre_ref[d] = h_re
            him_ref[d] = h_im

            @pl.when((s == 0) & (k < 4))
            def _(d=d, reverse=reverse):
                r0 = (S5_SUB - 1) * 8 if reverse else 0
                fin_ref[d, 0] = xre_ref[r0:r0 + 8, :]
                fin_ref[d, 1] = xim_ref[r0:r0 + 8, :]

            x_re = xre_ref[...].astype(BF16)
            x_im = xim_ref[...].astype(BF16)
            for q in range(4):
                ks = slice(q * 512, (q + 1) * 512)
                yq = (jnp.dot(x_re[:, ks], wc_ref[d, 0, q], preferred_element_type=F32)
                      + jnp.dot(x_im[:, ks], wc_ref[d, 1, q], preferred_element_type=F32))
                y_ref[rows, :, q * 128:(q + 1) * 128] = yq.reshape(S5_SUB, 8, 128)
            return carry

        lax.fori_loop(0, S5_NSUB, sub, 0)


def _s5(u_p, u_s, wb, wc, a_re, a_im, h0_re, h0_im, l):
    def rev_chunk(k):
        return jnp.where(k < 4, k, 11 - k)

    blk = (None, S5_CHUNK, 8, D_SSM)
    y_shape = jax.ShapeDtypeStruct((S5_NCHUNK, S5_CHUNK, 8, D_SSM), F32)
    return pl.pallas_call(
        _s5_kernel,
        grid=(S5_NCHUNK,),
        in_specs=[pl.BlockSpec(blk, lambda k: (jnp.minimum(k, 3), 0, 0, 0)),
                  pl.BlockSpec(blk, lambda k: (jnp.maximum(k - 4, 0), 0, 0, 0)),
                  pl.BlockSpec(blk, lambda k: (jnp.minimum(7 - k, 3), 0, 0, 0)),
                  pl.BlockSpec((None, 2, 2, 2, 256, 1024), lambda k: (l, 0, 0, 0, 0, 0)),
                  pl.BlockSpec((None, 2, 2, 4, 512, 128), lambda k: (l, 0, 0, 0, 0, 0)),
                  pl.BlockSpec((None, 2, 1, SSM_STATE), lambda k: (l, 0, 0, 0)),
                  pl.BlockSpec((None, 2, 1, SSM_STATE), lambda k: (l, 0, 0, 0)),
                  pl.BlockSpec((2, 8, SSM_STATE), lambda k: (0, 0, 0)),
                  pl.BlockSpec((2, 8, SSM_STATE), lambda k: (0, 0, 0))],
        out_specs=[pl.BlockSpec(blk, lambda k: (k, 0, 0, 0)),
                   pl.BlockSpec(blk, lambda k: (rev_chunk(k), 0, 0, 0)),
                   pl.BlockSpec((None, 2, 2, 8, SSM_STATE), lambda k: (jnp.minimum(k, 3), 0, 0, 0, 0))],
        out_shape=(y_shape, y_shape,
                   jax.ShapeDtypeStruct((4, 2, 2, 8, SSM_STATE), F32)),
        scratch_shapes=[pltpu.VMEM((2, 8, SSM_STATE), F32), pltpu.VMEM((2, 8, SSM_STATE), F32),
                        pltpu.VMEM((8, SSM_STATE), F32), pltpu.VMEM((8, SSM_STATE), F32),
                        pltpu.VMEM((S5_SUB * 8, SSM_STATE), F32), pltpu.VMEM((S5_SUB * 8, SSM_STATE), F32)],
        compiler_params=_cp(("arbitrary",), VMEM_LIMIT),
        name="s5_scan",
    )(u_p, u_s, u_s, wb, wc, a_re, a_im, h0_re, h0_im)


GLU_TM = 512


def _ssm_glu_kernel(u_ref, ysp_ref, yss_ref, d_ref, w_ref, b_ref, o_ref, w_b):
    i = pl.program_id(0)

    @pl.when(i == 0)
    def _():
        w_b[...] = w_ref[...].astype(BF16)

    ys = jnp.where(i < T_P // GLU_TM, ysp_ref[...], yss_ref[...])
    y = jax.nn.gelu(d_ref[...] * u_ref[...] + ys).astype(BF16)
    glu = jnp.dot(y, w_b[...], preferred_element_type=F32) + b_ref[...]
    o_ref[...] = (glu[:, :D_MODEL] * _sigmoid(glu[:, D_MODEL:])).astype(o_ref.dtype)


def _ssm_glu(proj, ys_p, ys_s, ssm_d3, w_glu, b_glu3, l):
    nbp = T_P // GLU_TM
    return pl.pallas_call(
        _ssm_glu_kernel,
        grid=(T // GLU_TM,),
        in_specs=[pl.BlockSpec((GLU_TM, D_SSM), lambda i: (i, 0)),
                  pl.BlockSpec((GLU_TM, D_SSM), lambda i: (jnp.minimum(i, nbp - 1), 0)),
                  pl.BlockSpec((GLU_TM, D_SSM), lambda i: (jnp.maximum(i - nbp, 0), 0)),
                  pl.BlockSpec((None, 1, D_SSM), lambda i: (l, 0, 0)),
                  pl.BlockSpec((None, D_SSM, 2 * D_MODEL), lambda i: (l, 0, 0)),
                  pl.BlockSpec((None, 1, 2 * D_MODEL), lambda i: (l, 0, 0))],
        out_specs=pl.BlockSpec((GLU_TM, D_MODEL), lambda i: (i, 0)),
        out_shape=jax.ShapeDtypeStruct((T, D_MODEL), BRANCH_DTYPE),
        scratch_shapes=[pltpu.VMEM((D_SSM, 2 * D_MODEL), BF16)],
        compiler_params=_cp(("arbitrary",), VMEM_LIMIT),
        name="ssm_glu",
    )(proj, ys_p, ys_s, ssm_d3, w_glu, b_glu3)


MLA_TM = DEC_SEQ


def _mla_prep_kernel(cq_ref, ckv_ref, kr_ref, qg_ref, kvg_ref, wqa_ref, wqb_ref,
                     ca_ref, sb_ref, ck_ref, sk_ref,
                     q_ref, ckvn_ref, krr_ref, wqa_b, wqb_b):
    @pl.when(pl.program_id(0) == 0)
    def _():
        wqa_b[...] = wqa_ref[...].astype(BF16)
        wqb_b[...] = wqb_ref[...].astype(BF16)

    cq = cq_ref[...]
    cqn = (cq * lax.rsqrt(jnp.mean(cq * cq, axis=-1, keepdims=True) + EPS) * qg_ref[...]).astype(BF16)
    qa = jnp.dot(cqn, wqa_b[...], preferred_element_type=F32)
    qb = jnp.dot(cqn, wqb_b[...], preferred_element_type=F32)
    ca = ca_ref[...]
    sb = sb_ref[...]
    for h in range(N_HEADS):
        cols = slice(h * 128, (h + 1) * 128)
        q_ref[:, cols] = (qa[:, cols] * ca + qb[:, cols] * sb).astype(q_ref.dtype)

    ckv = ckv_ref[...]
    ckvn_ref[...] = ckv * lax.rsqrt(jnp.mean(ckv * ckv, axis=-1, keepdims=True) + EPS) * kvg_ref[...]

    kr = kr_ref[...]
    krr_ref[...] = kr * ck_ref[...] + pltpu.roll(kr, 128 - ROPE_DIM, 1) * sk_ref[...]


def _mla_prep(proj, q_norm_g3, kv_norm_g3, wqa, wqb, tabs, l):
    ca, sb, ck, sk = tabs
    tab_spec = pl.BlockSpec((MLA_TM, 128), lambda i: (jnp.where(i < T_P // MLA_TM, 1, 0), 0))
    return pl.pallas_call(
        _mla_prep_kernel,
        grid=(T // MLA_TM,),
        in_specs=[pl.BlockSpec((MLA_TM, Q_LORA), lambda i: (i, 2)),
                  pl.BlockSpec((MLA_TM, 128), lambda i: (i, 6)),
                  pl.BlockSpec((MLA_TM, 128), lambda i: (i, 7)),
                  pl.BlockSpec((None, 1, Q_LORA), lambda i: (l, 0, 0)),
                  pl.BlockSpec((None, 1, KV_LORA), lambda i: (l, 0, 0)),
                  pl.BlockSpec((None, Q_LORA, 1024), lambda i: (l, 0, 0)),
                  pl.BlockSpec((None, Q_LORA, 1024), lambda i: (l, 0, 0)),
                  tab_spec, tab_spec, tab_spec, tab_spec],
        out_specs=[pl.BlockSpec((MLA_TM, 1024), lambda i: (i, 0)),
                   pl.BlockSpec((MLA_TM, 128), lambda i: (i, 0)),
                   pl.BlockSpec((MLA_TM, 128), lambda i: (i, 0))],
        out_shape=(jax.ShapeDtypeStruct((T, 1024), BF16),
                   jax.ShapeDtypeStruct((T, 128), F32),
                   jax.ShapeDtypeStruct((T, 128), F32)),
        scratch_shapes=[pltpu.VMEM((Q_LORA, 1024), BF16), pltpu.VMEM((Q_LORA, 1024), BF16)],
        compiler_params=_cp(("arbitrary",)),
        name="mla_prep",
    )(proj, proj, proj, q_norm_g3, kv_norm_g3, wqa, wqb, ca, sb, ck, sk)


ATT_TQ = 512


def _attn_kernel(q_ref, kvin_ref, wk_ref, wv_ref, wo_ref, o_ref,
                 k_scr, v_scr, o_scr, wk_b, wv_b, wo_b):
    s_id = pl.program_id(0)
    qb = pl.program_id(1)

    @pl.when((s_id == 0) & (qb == 0))
    def _():
        wk_b[...] = wk_ref[...].astype(BF16)
        wv_b[...] = wv_ref[...].astype(BF16)
        wo_b[...] = wo_ref[...].astype(BF16)

    @pl.when(qb == 0)
    def _():
        kv = kvin_ref[...].astype(BF16)
        k_scr[...] = jnp.dot(kv, wk_b[...], preferred_element_type=F32).astype(BF16)
        v_scr[...] = jnp.dot(kv[:, :KV_LORA], wv_b[...], preferred_element_type=F32).astype(BF16)

    scale = (NOPE_DIM + ROPE_DIM) ** -0.5
    for h in range(N_HEADS):
        qh = q_ref[:, h * 128:(h + 1) * 128]
        kh = k_scr[:, h * 128:(h + 1) * 128]
        s = lax.dot_general(qh, kh, (((1,), (1,)), ((), ())), preferred_element_type=F32) * scale
        m = jnp.max(s, axis=-1, keepdims=True)
        p = jnp.exp(s - m)
        den = jnp.sum(p, axis=-1, keepdims=True)
        oh = jnp.dot(p.astype(BF16), v_scr[:, h * V_DIM:(h + 1) * V_DIM], preferred_element_type=F32)
        o_scr[:, h * V_DIM:(h + 1) * V_DIM] = oh / den
    o_ref[...] = jnp.dot(o_scr[...].astype(BF16), wo_b[...], preferred_element_type=F32).astype(o_ref.dtype)


def _attn(q, kvin, wk2, wv, w_o, l, nseq, nqb, tq, row_blk0):
    lk = kvin.shape[1]
    return pl.pallas_call(
        _attn_kernel,
        grid=(nseq, nqb),
        in_specs=[pl.BlockSpec((tq, 1024), lambda s, b: (row_blk0 + s * nqb + b, 0)),
                  pl.BlockSpec((None, lk, 256), lambda s, b: (s, 0, 0)),
                  pl.BlockSpec((None, 256, 1024), lambda s, b: (l, 0, 0)),
                  pl.BlockSpec((None, KV_LORA, 512), lambda s, b: (l, 0, 0)),
                  pl.BlockSpec((None, 512, D_MODEL), lambda s, b: (l, 0, 0))],
        out_specs=pl.BlockSpec((tq, D_MODEL), lambda s, b: (s * nqb + b, 0)),
        out_shape=jax.ShapeDtypeStruct((nseq * nqb * tq, D_MODEL), BRANCH_DTYPE),
        scratch_shapes=[pltpu.VMEM((lk, 1024), BF16), pltpu.VMEM((lk, 512), BF16),
                        pltpu.VMEM((tq, 512), F32),
                        pltpu.VMEM((256, 1024), BF16), pltpu.VMEM((KV_LORA, 512), BF16),
                        pltpu.VMEM((512, D_MODEL), BF16)],
        compiler_params=_cp(("arbitrary", "arbitrary"), VMEM_LIMIT),
        name="attn",
    )(q, kvin, wk2, wv, w_o)


CONV_PAD = 16
CONV_CB = 128
CONV_RB = 128


def _conv_kernel(cin_ref, w_ref, b_ref, lg_ref, lb_ref, pw_ref, pb_ref, o_ref,
                 pad_scr, conv_scr, act_scr, pw_b, *, seq):
    @pl.when(pl.program_id(0) == 0)
    def _():
        pw_b[...] = pw_ref[...].astype(BF16)
        pad_scr[0:CONV_PAD, :] = jnp.zeros((CONV_PAD, D_CONV), F32)
        pad_scr[CONV_PAD + seq:, :] = jnp.zeros((CONV_PAD, D_CONV), F32)

    pad_scr[CONV_PAD:CONV_PAD + seq, :] = cin_ref[:, :D_CONV] * _sigmoid(cin_ref[:, D_CONV:])

    win_rows = CONV_CB + 2 * CONV_PAD
    for c in range(D_CONV // 128):
        lanes = slice(c * 128, (c + 1) * 128)

        def cblk(rb, carry, lanes=lanes):
            base = pl.multiple_of(rb * CONV_CB, CONV_CB)
            win = pad_scr[pl.ds(base, win_rows), lanes]
            acc = jnp.broadcast_to(b_ref[:, lanes], (CONV_CB, 128))
            for b in range(8):
                wb = win if b == 0 else pltpu.roll(win, win_rows - b, 0)
                for a in range(2 * CONV_PAD // 8):
                    k = 8 * a + b - (CONV_PAD - CONV_K // 2)
                    if 0 <= k < CONV_K:
                        acc = acc + w_ref[k:k + 1, lanes] * wb[8 * a:8 * a + CONV_CB]
            conv_scr[pl.ds(base, CONV_CB), lanes] = acc
            return carry

        lax.fori_loop(0, seq // CONV_CB, cblk, 0)

    def blk(rb, carry):
        base = pl.multiple_of(rb * CONV_RB, CONV_RB)
        acc = conv_scr[pl.ds(base, CONV_RB), :]
        mu = jnp.mean(acc, axis=-1, keepdims=True)
        xc = acc - mu
        var = jnp.mean(xc * xc, axis=-1, keepdims=True)
        y = xc * lax.rsqrt(var + EPS) * lg_ref[...] + lb_ref[...]
        act_scr[pl.ds(base, CONV_RB), :] = (y * _sigmoid(y)).astype(BF16)
        return carry

    lax.fori_loop(0, seq // CONV_RB, blk, 0)
    o_ref[...] = (jnp.dot(act_scr[...], pw_b[...], preferred_element_type=F32) + pb_ref[...]).astype(o_ref.dtype)


def _conv(proj, dw_w, dw_b3, ln_g3, ln_b3, pw, pb3, l, nseq, seq, row_blk0):
    vec = lambda n: pl.BlockSpec((None, 1, n), lambda s: (l, 0, 0))
    return pl.pallas_call(
        functools.partial(_conv_kernel, seq=seq),
        grid=(nseq,),
        in_specs=[pl.BlockSpec((seq, 2 * D_CONV), lambda s: (row_blk0 + s, 1)),
                  pl.BlockSpec((None, CONV_K, D_CONV), lambda s: (l, 0, 0)),
                  vec(D_CONV), vec(D_CONV), vec(D_CONV),
                  pl.BlockSpec((None, D_CONV, D_MODEL), lambda s: (l, 0, 0)),
                  vec(D_MODEL)],
        out_specs=pl.BlockSpec((seq, D_MODEL), lambda s: (s, 0)),
        out_shape=jax.ShapeDtypeStruct((nseq * seq, D_MODEL), BRANCH_DTYPE),
        scratch_shapes=[pltpu.VMEM((seq + 2 * CONV_PAD, D_CONV), F32),
                        pltpu.VMEM((seq, D_CONV), F32),
                        pltpu.VMEM((seq, D_CONV), BF16),
                        pltpu.VMEM((D_CONV, D_MODEL), BF16)],
        compiler_params=_cp(("arbitrary",), VMEM_LIMIT),
        name="conv",
    )(proj, dw_w, dw_b3, ln_g3, ln_b3, pw, pb3)


def _merge_kernel(xa_ref, xb_ref, os_ref, omp_ref, oms_ref, ocp_ref, ocs_ref, ga_ref, gb_ref, gc_ref,
                  wo_ref, mod_ref, ng_ref, rw_ref, rb_ref,
                  x1_ref, h2_ref, tw_ref, sc_ref, sr_ref, nb_ref, wo_b, rw_hi, rw_lo, *, n_a):
    i = pl.program_id(0)

    @pl.when(i == 0)
    def _():
        wo_b[...] = wo_ref[...].astype(BF16)
        rw = rw_ref[...]
        rw_hi[...] = rw.astype(BF16)
        rw_lo[...] = (rw - rw_hi[...].astype(F32)).astype(BF16)

    is_prompt = i < NTB_P
    o_mla = jnp.where(is_prompt, omp_ref[...], oms_ref[...])
    o_conv = jnp.where(is_prompt, ocp_ref[...], ocs_ref[...])
    merged = (_sigmoid(ga_ref[...]) * os_ref[...]
              + _sigmoid(gb_ref[...]) * o_mla
              + _sigmoid(gc_ref[...]) * o_conv)
    out = jnp.dot(merged.astype(BF16), wo_b[...], preferred_element_type=F32)
    x1 = jnp.where(i < n_a, xa_ref[...], xb_ref[...]) + mod_ref[2:3, :] * out
    x1_ref[...] = x1
    y = x1 * lax.rsqrt(jnp.mean(x1 * x1, axis=-1, keepdims=True) + EPS) * ng_ref[...]
    h2 = y * (1.0 + mod_ref[4:5, :]) + mod_ref[3:4, :]
    h2_hi = h2.astype(BF16)
    h2_ref[...] = h2_hi

    h2_lo = (h2 - h2_hi.astype(F32)).astype(BF16)
    logits = (jnp.dot(h2_hi, rw_hi[...], preferred_element_type=F32)
              + jnp.dot(h2_lo, rw_hi[...], preferred_element_type=F32)
              + jnp.dot(h2_hi, rw_lo[...], preferred_element_type=F32)) + rb_ref[...]
    lane_i = lax.broadcasted_iota(jnp.int32, logits.shape, 1)
    lane = lane_i.astype(F32)
    cur = logits
    vals, idxs = [], []
    for _ in range(TOP_K):
        m = jnp.max(cur, axis=-1, keepdims=True)
        idx = jnp.min(jnp.where(cur == m, lane, 128.0), axis=-1, keepdims=True)
        vals.append(m)
        idxs.append(idx)
        cur = jnp.where(lane == idx, -jnp.inf, cur)
    exps = [jnp.exp(v - vals[0]) for v in vals]
    tot = exps[0] + exps[1] + exps[2] + exps[3]
    tw = jnp.zeros(logits.shape, F32)
    for k in range(TOP_K):
        tw = jnp.where(lane_i == k, exps[k] / tot, tw)
    tw_ref[...] = tw

    ohs = [jnp.where(lane == idxs[k], 1.0, 0.0) for k in range(TOP_K)]
    cnts = [jnp.sum(oh, axis=0, keepdims=True) for oh in ohs]
    n_b = jnp.ceil((cnts[0] + cnts[1] + cnts[2] + cnts[3]) * (1.0 / SEG_ALIGN)) * SEG_ALIGN
    inc = jnp.broadcast_to(n_b, (8, 128))
    lane8 = lax.broadcasted_iota(jnp.int32, (8, 128), 1)
    for s in (1, 2, 4, 8, 16, 32, 64):
        inc = inc + jnp.where(lane8 >= s, pltpu.roll(inc, s, 1), 0.0)
    before = (inc - n_b)[0:1, :]
    r_i = lax.broadcasted_iota(jnp.int32, (TB, TB), 0)
    c_i = lax.broadcasted_iota(jnp.int32, (TB, TB), 1)
    earlier = jnp.where(r_i > c_i, 1.0, 0.0).astype(BF16)
    slots = jnp.zeros(logits.shape, F32)
    for k in range(TOP_K):
        pre = jnp.dot(earlier, ohs[k].astype(BF16), preferred_element_type=F32)
        slot_k = jnp.sum(ohs[k] * (pre + before), axis=-1, keepdims=True)
        slots = jnp.where(lane_i == k, slot_k, slots)
        before = before + cnts[k]
    sc_ref[...] = slots.astype(jnp.int32)
    sr_ref[...] = slots.T[0:8, :].astype(jnp.int32)
    nb_ref[...] = jnp.broadcast_to(n_b, (8, 128)).astype(jnp.int32)


def _merge(xa, xb, o_ssm, o_mla_p, o_mla_s, o_conv_p, o_conv_s, proj, w_out, modt, norm_ffn_g3, rw_p, rb_p, l):
    n_a = xa.shape[0] // TB if xb is not xa else NTB
    row = lambda i: (i, 0)
    row_p = lambda i: (jnp.minimum(i, NTB_P - 1), 0)
    row_s = lambda i: (jnp.maximum(i - NTB_P, 0), 0)
    blk = lambda m: pl.BlockSpec((TB, D_MODEL), m)
    return pl.pallas_call(
        functools.partial(_merge_kernel, n_a=n_a),
        grid=(NTB,),
        in_specs=[blk(lambda i: (jnp.minimum(i, n_a - 1), 0)), blk(lambda i: (jnp.maximum(i - n_a, 0), 0)),
                  blk(row), blk(row_p), blk(row_s), blk(row_p), blk(row_s),
                  blk(lambda i: (i, 2)), blk(lambda i: (i, 3)), blk(lambda i: (i, 4)),
                  pl.BlockSpec((None, D_MODEL, D_MODEL), lambda i: (l, 0, 0)),
                  pl.BlockSpec((None, 6, D_MODEL), lambda i: (_mod_row(i), 0, 0)),
                  pl.BlockSpec((None, 1, D_MODEL), lambda i: (l, 0, 0)),
                  pl.BlockSpec((None, D_MODEL, 128), lambda i: (l, 0, 0)),
                  pl.BlockSpec((None, 1, 128), lambda i: (l, 0, 0))],
        out_specs=[blk(row), blk(row), pl.BlockSpec((TB, 128), row), pl.BlockSpec((TB, 128), row),
                   pl.BlockSpec((None, 8, TB), lambda i: (i, 0, 0)),
                   pl.BlockSpec((None, 8, 128), lambda i: (i, 0, 0))],
        out_shape=(jax.ShapeDtypeStruct((T, D_MODEL), F32), jax.ShapeDtypeStruct((T, D_MODEL), BF16),
                   jax.ShapeDtypeStruct((T, 128), F32), jax.ShapeDtypeStruct((T, 128), jnp.int32),
                   jax.ShapeDtypeStruct((NTB, 8, TB), jnp.int32),
                   jax.ShapeDtypeStruct((NTB, 8, 128), jnp.int32)),
        scratch_shapes=[pltpu.VMEM((D_MODEL, D_MODEL), BF16),
                        pltpu.VMEM((D_MODEL, 128), BF16), pltpu.VMEM((D_MODEL, 128), BF16)],
        compiler_params=_cp(("arbitrary",), VMEM_LIMIT),
        name="merge",
    )(xa, xb, o_ssm, o_mla_p, o_mla_s, o_conv_p, o_conv_s, proj, proj, proj, w_out, modt, norm_ffn_g3, rw_p, rb_p)


PAIRS = TB * TOP_K
G_ROWS = -(-(PAIRS + (SEG_ALIGN - 1) * N_EXPERTS) // 256) * 256
SEG_BITS = 9


def _aligned(off):
    return off if isinstance(off, int) else pl.multiple_of(off, SEG_ALIGN)


def _segment_copies(src_ref, dst_ref, sem, n, src_off, dst_off):
    copies = []
    done = 0
    for bit in range(SEG_BITS - 1, SEG_ALIGN.bit_length() - 2, -1):
        size = 1 << bit
        part = n & size
        copies.append((part != 0,
                       pltpu.make_async_copy(src_ref.at[pl.ds(_aligned(src_off + done), size)],
                                             dst_ref.at[pl.ds(_aligned(dst_off + done), size)], sem)))
        done = done + part
    return copies


def _start_all(copies):
    for pred, cp in copies:
        pl.when(pred)(cp.start)


def _wait_all(copies):
    for pred, cp in copies:
        pl.when(pred)(cp.wait)


def _dispatch_kernel(seg_ref, nbe_ref, cnt_ref, pst_ref, nu_ref, h_ref, sr_ref, xs_ref, g_scr, z_scr, sem):
    b = pl.program_id(0)
    zsem = sem.at[2]

    def copies(blk, e, off):
        n = nbe_ref[blk * N_EXPERTS + e]
        dst = seg_ref[blk * N_EXPERTS + e]
        slot = blk % 2
        return n, _segment_copies(g_scr.at[slot], xs_ref, sem.at[slot], n, off, dst)

    def drain(blk):
        def wait(e, off):
            n, cps = copies(blk, e, off)
            _wait_all(cps)
            return off + n

        lax.fori_loop(0, N_EXPERTS, wait, 0)

    @pl.when(b < NTB)
    def _():
        sub = lax.broadcasted_iota(jnp.int32, (G_ROWS, TB), 0)
        sr = sr_ref[...]
        hit = sub == sr[0:1, :]
        for k in range(1, TOP_K):
            hit = hit | (sub == sr[k:k + 1, :])
        g_scr[b % 2] = jnp.dot(jnp.where(hit, 1.0, 0.0).astype(BF16), h_ref[...], preferred_element_type=F32)

        def start(e, off):
            n, cps = copies(b, e, off)
            _start_all(cps)
            return off + n

        lax.fori_loop(0, N_EXPERTS, start, 0)

    @pl.when(b > 0)
    def _():
        drain(b - 1)

    @pl.when(b == NTB)
    def _():
        sem = zsem
        z_scr[...] = jnp.zeros_like(z_scr)

        def pad(e):
            cnt = cnt_ref[e]
            padded = (cnt + MOE_BM - 1) // MOE_BM * MOE_BM
            return padded - cnt, pst_ref[e] + cnt

        def start(e, carry):
            n, dst = pad(e)
            _start_all(_segment_copies(z_scr, xs_ref, sem, n, 0, dst))
            return carry

        def wait(e, carry):
            n, dst = pad(e)
            _wait_all(_segment_copies(z_scr, xs_ref, sem, n, 0, dst))
            return carry

        lax.fori_loop(0, N_EXPERTS, start, 0)
        lax.fori_loop(0, N_EXPERTS, wait, 0)

        def tail_copy(blk):
            return pltpu.make_async_copy(z_scr, xs_ref.at[pl.ds(blk * MOE_BM, MOE_BM)], sem)

        def tail_start(blk, carry):
            tail_copy(blk).start()
            return carry

        def tail_wait(blk, carry):
            tail_copy(blk).wait()
            return carry

        lax.fori_loop(nu_ref[0], MOE_NB, tail_start, 0)
        lax.fori_loop(nu_ref[0], MOE_NB, tail_wait, 0)


def _dispatch(seg_start, n_be, counts, pad_start, n_used, h2, slot_row):
    last = lambda b, *_: (jnp.minimum(b, NTB - 1), 0)
    grid_spec = pltpu.PrefetchScalarGridSpec(
        num_scalar_prefetch=5,
        grid=(NTB + 1,),
        in_specs=[pl.BlockSpec((TB, D_MODEL), last),
                  pl.BlockSpec((None, 8, TB), lambda b, *_: (jnp.minimum(b, NTB - 1), 0, 0))],
        out_specs=pl.BlockSpec(memory_space=pl.ANY),
        scratch_shapes=[pltpu.VMEM((2, G_ROWS, D_MODEL), F32),
                        pltpu.VMEM((MOE_BM, D_MODEL), F32),
                        pltpu.SemaphoreType.DMA((3,))],
    )
    return pl.pallas_call(
        _dispatch_kernel,
        grid_spec=grid_spec,
        out_shape=jax.ShapeDtypeStruct((MOE_ROWS, D_MODEL), F32),
        compiler_params=_cp(("arbitrary",), VMEM_LIMIT),
        name="moe_dispatch",
    )(seg_start, n_be, counts, pad_start, n_used, h2, slot_row)


def _expert_kernel(be_ref, slot_ref, nu_ref, x_ref, w1_ref, b1_ref, w2_ref, b2_ref, y_ref, w1_b, w2_b):
    i = pl.program_id(0)
    blk = i - 1
    cur = be_ref[jnp.maximum(blk, 0)]
    nxt = be_ref[jnp.minimum(i, MOE_NB - 1)]

    @pl.when((i == 0) | (nxt != cur))
    def _():
        w1_b[slot_ref[jnp.minimum(i, MOE_NB - 1)]] = w1_ref[...].astype(BF16)

    @pl.when((blk == 0) | ((blk > 0) & (cur != be_ref[jnp.maximum(blk - 1, 0)])))
    def _():
        w2_b[...] = w2_ref[...].astype(BF16)

    @pl.when((blk >= 0) & (blk < nu_ref[0]))
    def _():
        w1 = w1_b[slot_ref[jnp.maximum(blk, 0)]]
        gu = jnp.dot(x_ref[...].astype(BF16), w1, preferred_element_type=F32) + b1_ref[...]
        g = jnp.minimum(gu[:, :D_FF], SWIGLU_LIMIT)
        u = jnp.clip(gu[:, D_FF:], -SWIGLU_LIMIT, SWIGLU_LIMIT)
        act = (u + 1.0) * (g * _sigmoid(SWIGLU_ALPHA * g))
        y_ref[...] = jnp.dot(act.astype(BF16), w2_b[...], preferred_element_type=F32) + b2_ref[...]

    @pl.when(blk >= nu_ref[0])
    def _():
        y_ref[...] = jnp.zeros_like(y_ref)


def _experts(block_e, w1_slot, n_used, xs, w1, b1_4, w2, b2_4, l):
    prev = lambda i: jnp.maximum(i - 1, 0)
    grid_spec = pltpu.PrefetchScalarGridSpec(
        num_scalar_prefetch=3,
        grid=(MOE_NB + 1,),
        in_specs=[pl.BlockSpec((MOE_BM, D_MODEL), lambda i, be, sl, nu: (jnp.minimum(prev(i), nu[0] - 1), 0)),
                  pl.BlockSpec((None, None, D_MODEL, 2 * D_FF),
                               lambda i, be, sl, nu: (l, be[jnp.minimum(i, MOE_NB - 1)], 0, 0)),
                  pl.BlockSpec((None, None, 1, 2 * D_FF), lambda i, be, sl, nu: (l, be[prev(i)], 0, 0)),
                  pl.BlockSpec((None, None, D_FF, D_MODEL), lambda i, be, sl, nu: (l, be[prev(i)], 0, 0)),
                  pl.BlockSpec((None, None, 1, D_MODEL), lambda i, be, sl, nu: (l, be[prev(i)], 0, 0))],
        out_specs=pl.BlockSpec((MOE_BM, D_MODEL), lambda i, be, sl, nu: (prev(i), 0)),
        scratch_shapes=[pltpu.VMEM((2, D_MODEL, 2 * D_FF), BF16), pltpu.VMEM((D_FF, D_MODEL), BF16)],
    )
    return pl.pallas_call(
        _expert_kernel,
        grid_spec=grid_spec,
        out_shape=jax.ShapeDtypeStruct((MOE_ROWS, D_MODEL), F32),
        compiler_params=_cp(("arbitrary",), VMEM_LIMIT),
        name="moe_experts",
    )(block_e, w1_slot, n_used, xs, w1, b1_4, w2, b2_4)


def _combine_kernel(seg_ref, nbe_ref, x1_ref, tw_ref, sc_ref, mod_ref, fg_ref, ys_ref, *rest, final):
    if final:
        yp_ref, ys_out_ref, g_scr, sem = rest
    else:
        x2_ref, g_scr, sem = rest
    b = pl.program_id(0)

    def copies(blk, e, off):
        n = nbe_ref[blk * N_EXPERTS + e]
        src = seg_ref[blk * N_EXPERTS + e]
        slot = blk % 2
        return n, _segment_copies(ys_ref, g_scr.at[slot], sem.at[slot], n, src, off)

    def fetch(blk):
        g_scr[blk % 2, PAIRS:, :] = jnp.zeros((G_ROWS - PAIRS, D_MODEL), F32)

        def start(e, off):
            n, cps = copies(blk, e, off)
            _start_all(cps)
            return off + n

        lax.fori_loop(0, N_EXPERTS, start, 0)

    @pl.when(b == 0)
    def _():
        fetch(b)

    @pl.when(b + 1 < NTB)
    def _():
        fetch(b + 1)

    def wait(e, off):
        n, cps = copies(b, e, off)
        _wait_all(cps)
        return off + n

    lax.fori_loop(0, N_EXPERTS, wait, 0)

    sc = sc_ref[...]
    tw = tw_ref[...]
    lane = lax.broadcasted_iota(jnp.int32, (TB, G_ROWS), 1)
    pw = jnp.zeros((TB, G_ROWS), F32)
    for k in range(TOP_K):
        pw = pw + jnp.where(lane == sc[:, k:k + 1], tw[:, k:k + 1], 0.0)
    moe = jnp.dot(pw.astype(BF16), g_scr[b % 2].astype(BF16), preferred_element_type=F32)
    x2 = x1_ref[...] + mod_ref[5:6, :] * moe
    if final:
        yn = x2 * lax.rsqrt(jnp.mean(x2 * x2, axis=-1, keepdims=True) + EPS) * fg_ref[...]

        @pl.when(b < NTB_P)
        def _():
            yp_ref[...] = yn

        @pl.when(b >= NTB_P)
        def _():
            ys_out_ref[...] = yn
    else:
        x2_ref[...] = x2


def _combine(seg_start, n_be, x1, topw, slot_col, modt, final_g2, ys, final):
    row = lambda b, *_: (b, 0)
    out_blk = pl.BlockSpec((TB, D_MODEL), row)
    out_sds = jax.ShapeDtypeStruct((T, D_MODEL), F32)
    fin_blks = [pl.BlockSpec((TB, D_MODEL), lambda b, *_: (jnp.minimum(b, NTB_P - 1), 0)),
                pl.BlockSpec((TB, D_MODEL), lambda b, *_: (jnp.maximum(b - NTB_P, 0), 0))]
    fin_sds = (jax.ShapeDtypeStruct((T_P, D_MODEL), F32), jax.ShapeDtypeStruct((T_S, D_MODEL), F32))
    grid_spec = pltpu.PrefetchScalarGridSpec(
        num_scalar_prefetch=2,
        grid=(NTB,),
        in_specs=[pl.BlockSpec((TB, D_MODEL), row),
                  pl.BlockSpec((TB, 128), row),
                  pl.BlockSpec((TB, 128), row),
                  pl.BlockSpec((None, 6, D_MODEL), lambda b, *_: (_mod_row(b), 0, 0)),
                  pl.BlockSpec((1, D_MODEL), lambda b, *_: (0, 0)),
                  pl.BlockSpec(memory_space=pl.ANY)],
        out_specs=fin_blks if final else [out_blk],
        scratch_shapes=[pltpu.VMEM((2, G_ROWS, D_MODEL), F32), pltpu.SemaphoreType.DMA((2,))],
    )
    return pl.pallas_call(
        functools.partial(_combine_kernel, final=final),
        grid_spec=grid_spec,
        out_shape=fin_sds if final else (out_sds,),
        compiler_params=_cp(("arbitrary",), VMEM_LIMIT),
        name="moe_combine",
    )(seg_start, n_be, x1, topw, slot_col, modt, final_g2, ys)


def _rot_cols(w):
    q = ROPE_DIM // 4
    parts = []
    for half in range(2):
        a = w[..., half * 2 * q:half * 2 * q + q]
        b = w[..., half * 2 * q + q:(half + 1) * 2 * q]
        parts += [-b, a]
    return jnp.concatenate(parts, axis=-1)


def _rope_tables():
    rows = DEC_SEQ // GRID_W
    row = jnp.repeat(jnp.arange(rows), GRID_W).astype(F32)
    col = jnp.tile(jnp.arange(GRID_W), rows).astype(F32)
    half = ROPE_DIM // 2
    freqs = ROPE_BASE ** (-jnp.arange(0, half, 2, dtype=F32) / half)
    ang_r = row[:, None] * freqs
    ang_c = col[:, None] * freqs
    ang = jnp.concatenate([ang_r, ang_r, ang_c, ang_c], axis=-1)
    cos = jnp.concatenate([jnp.cos(ang), jnp.ones((MLA_TM, ROPE_DIM), F32)], axis=0)
    sin = jnp.concatenate([jnp.sin(ang), jnp.zeros((MLA_TM, ROPE_DIM), F32)], axis=0)
    n = cos.shape[0]
    z = lambda w: jnp.zeros((n, w), F32)
    ca = jnp.concatenate([jnp.ones((n, NOPE_DIM), F32), cos, z(32)], axis=1)
    sb = jnp.concatenate([z(NOPE_DIM), sin, z(32)], axis=1)
    ck = jnp.concatenate([cos, z(96)], axis=1)
    sk = jnp.concatenate([sin, z(96)], axis=1)
    return ca, sb, ck, sk


def kernel(x_prompt, x_sample, c, cache_kv_latent, cache_k_rope, state_ssm, c_ctx, norm_mix_g, norm_ffn_g, final_norm_g, w_ada, b_ada, w_in, q_norm_g, w_uq, kv_norm_g, w_ukv, w_o_mla, ssm_lam_re, ssm_lam_im, ssm_log_dt, ssm_b_re, ssm_b_im, ssm_c_re, ssm_c_im, ssm_d, w_ssm_glu, b_ssm_glu, conv_dw_w, conv_dw_b, conv_ln_g, conv_ln_b, w_conv_pw2, b_conv_pw2, w_out, router_w, router_b, moe_w1, moe_b1, moe_w2, moe_b2):
    L = DEPTH
    v3 = lambda a: a.reshape(L, 1, a.shape[-1])

    xa, xb = x_prompt.reshape(T_P, D_MODEL), x_sample.reshape(T_S, D_MODEL)
    cc = jnp.concatenate([c, c_ctx[None, :], jnp.zeros((16 - DEC_BATCH - 1, D_MODEL), F32)], axis=0)


    wq = w_uq.reshape(L, Q_LORA, N_HEADS, NOPE_DIM + ROPE_DIM)
    zq = lambda w: jnp.zeros((L, Q_LORA, N_HEADS, w), F32)
    wqa = jnp.concatenate([wq, zq(32)], axis=-1).reshape(L, Q_LORA, 1024)
    wqb = jnp.concatenate([zq(NOPE_DIM), _rot_cols(wq[..., NOPE_DIM:]), zq(32)], axis=-1).reshape(L, Q_LORA, 1024)
    wkv = w_ukv.reshape(L, KV_LORA, N_HEADS, NOPE_DIM + V_DIM)
    wk_lat = jnp.concatenate([wkv[..., :NOPE_DIM], jnp.zeros((L, KV_LORA, N_HEADS, 64), F32)], axis=-1)
    place = jnp.concatenate([jnp.zeros((ROPE_DIM, NOPE_DIM), F32), jnp.eye(ROPE_DIM, dtype=F32),
                             jnp.zeros((ROPE_DIM, 32), F32)], axis=-1)
    wk_rope = jnp.broadcast_to(place[None, :, None, :], (L, ROPE_DIM, N_HEADS, 128))
    wk2 = jnp.concatenate([wk_lat, wk_rope, jnp.zeros((L, 96, N_HEADS, 128), F32)], axis=1).reshape(L, 256, 1024)
    wv = wkv[..., NOPE_DIM:].reshape(L, KV_LORA, N_HEADS * V_DIM)
    tabs = _rope_tables()

    a_re, a_im, f_re, f_im = [v.reshape(L, 2, N_SSM_GROUPS, N_STATE)
                              for v in _s5_disc(ssm_lam_re, ssm_lam_im, ssm_log_dt)]
    bb_re = f_re[..., None] * ssm_b_re - f_im[..., None] * ssm_b_im
    bb_im = f_re[..., None] * ssm_b_im + f_im[..., None] * ssm_b_re
    eye16 = jnp.eye(16, dtype=F32)
    eye8 = jnp.eye(8, dtype=F32)

    def bd_in(bb):
        bb = bb.reshape(L, 2, 2, 16, N_STATE, SSM_GROUP)
        return jnp.einsum('ldhgpc,gk->ldhgckp', bb, eye16).reshape(L, 2, 2, 256, 1024)

    def bd_out(cm):
        cm = cm.reshape(L, 2, 4, 8, SSM_GROUP, N_STATE)
        return jnp.einsum('ldqgcp,gk->ldqgpkc', cm, eye8).reshape(L, 2, 4, 512, 128)

    wb = jnp.stack([bd_in(bb_re), bd_in(bb_im)], axis=2).astype(BF16)
    wc = jnp.stack([bd_out(ssm_c_re), bd_out(-ssm_c_im)], axis=2).astype(BF16)
    a_re4 = a_re.reshape(L, 2, 1, SSM_STATE)
    a_im4 = a_im.reshape(L, 2, 1, SSM_STATE)

    rw_p = jnp.concatenate([router_w, jnp.zeros((L, D_MODEL, 128 - N_EXPERTS), F32)], axis=-1)
    rb_p = jnp.concatenate([router_b, jnp.full((L, 128 - N_EXPERTS), -1e30, F32)], axis=-1).reshape(L, 1, 128)

    kv_out, kr_out, ssm_out = [], [], []
    for l in range(L):
        modt = _ada(cc, w_ada, v3(b_ada), l).reshape(16, 6, D_MODEL)
        proj = _win(xa, xb, v3(norm_mix_g), modt, w_in, l)

        u_p = proj[:T_P, :D_SSM].reshape(4, 8, SEQ, D_SSM).transpose(0, 2, 1, 3)
        u_s = proj[T_P:, :D_SSM].reshape(DEC_BATCH, 4, S5_CHUNK, D_SSM).transpose(1, 2, 0, 3)
        st = state_ssm[:, l]
        h0 = st.transpose(4, 1, 0, 2, 3).reshape(2, 2, DEC_BATCH, SSM_STATE)
        y_f, y_r, fin = _s5(u_p, u_s, wb, wc, a_re4, a_im4, h0[0], h0[1], l)
        ysum = y_f + y_r
        o_ssm = _ssm_glu(proj, ysum[:4].transpose(0, 2, 1, 3).reshape(T_P, D_SSM),
                         ysum[4:].transpose(2, 0, 1, 3).reshape(T_S, D_SSM),
                         v3(ssm_d), w_ssm_glu, v3(b_ssm_glu), l)
        ssm_out.append(fin.reshape(4, 2, 2, 8, N_SSM_GROUPS, N_STATE).transpose(0, 3, 1, 4, 5, 2)
                       .reshape(BATCH, 2, N_SSM_GROUPS, N_STATE, 2))

        q, ckvn, krr = _mla_prep(proj, v3(q_norm_g), v3(kv_norm_g), wqa, wqb, tabs, l)
        kvin_p = jnp.concatenate([ckvn[:T_P], krr[:T_P]], axis=-1).reshape(BATCH, SEQ, 256)
        cache_kr = jnp.concatenate([cache_k_rope[:, l], jnp.zeros((DEC_BATCH, PAST_LEN, 96), F32)], axis=-1)
        kvin_s = jnp.concatenate([
            jnp.concatenate([cache_kv_latent[:, l], cache_kr], axis=-1),
            jnp.concatenate([ckvn[T_P:], krr[T_P:]], axis=-1).reshape(DEC_BATCH, DEC_SEQ, 256)], axis=1)
        o_mla_p = _attn(q, kvin_p, wk2, wv, w_o_mla, l, BATCH, 1, SEQ, 0)
        o_mla_s = _attn(q, kvin_s, wk2, wv, w_o_mla, l, DEC_BATCH, DEC_SEQ // ATT_TQ, ATT_TQ, T_P // ATT_TQ)
        kv_out.append(ckvn[:T_P].reshape(BATCH, SEQ, KV_LORA))
        kr_out.append(proj[:T_P, KR0:KR0 + ROPE_DIM].reshape(BATCH, SEQ, ROPE_DIM))

        conv_args = (conv_dw_w, v3(conv_dw_b), v3(conv_ln_g), v3(conv_ln_b), w_conv_pw2, v3(b_conv_pw2), l)
        o_conv_p = _conv(proj, *conv_args, BATCH, SEQ, 0)
        o_conv_s = _conv(proj, *conv_args, DEC_BATCH, DEC_SEQ, T_P // DEC_SEQ)

        x1, h2, topw, slot_col, slot_row, n_blk = _merge(
            xa, xb, o_ssm, o_mla_p, o_mla_s, o_conv_p, o_conv_s, proj, w_out, modt, v3(norm_ffn_g), rw_p, rb_p, l)

        n_be = n_blk[:, 0, :N_EXPERTS]
        counts = jnp.sum(n_be, axis=0)
        padded = (counts + MOE_BM - 1) // MOE_BM * MOE_BM
        pad_end = jnp.cumsum(padded)
        pad_start = pad_end - padded
        seg_start = (pad_start[None, :] + jnp.cumsum(n_be, axis=0) - n_be).reshape(-1).astype(jnp.int32)
        blk_row = jnp.arange(MOE_NB, dtype=jnp.int32) * MOE_BM
        block_e = jnp.minimum(jnp.sum((pad_end[None, :] <= blk_row[:, None]).astype(jnp.int32), axis=1),
                              N_EXPERTS - 1).astype(jnp.int32)
        n_used = (pad_end[-1:] // MOE_BM).astype(jnp.int32)
        blk_id = jnp.arange(MOE_NB, dtype=jnp.int32)
        block_e = jnp.where(blk_id < n_used[0], block_e, jnp.max(jnp.where(blk_id < n_used[0], block_e, 0)))
        switches = jnp.concatenate([jnp.zeros((1,), jnp.int32), (block_e[1:] != block_e[:-1]).astype(jnp.int32)])
        w1_slot = (jnp.cumsum(switches) % 2).astype(jnp.int32)
        n_be = n_be.reshape(-1)

        xs = _dispatch(seg_start, n_be, counts.astype(jnp.int32), pad_start.astype(jnp.int32), n_used, h2, slot_row)
        ys = _experts(block_e, w1_slot, n_used, xs, moe_w1, moe_b1.reshape(L, N_EXPERTS, 1, 2 * D_FF),
                      moe_w2, moe_b2.reshape(L, N_EXPERTS, 1, D_MODEL), l)
        outs = _combine(seg_start, n_be, x1, topw, slot_col, modt, final_norm_g.reshape(1, D_MODEL), ys,
                        final=(l == L - 1))
        if l < L - 1:
            xa = xb = outs[0]

    y_prompt = outs[0].reshape(BATCH, SEQ, D_MODEL)
    y_sample = outs[1].reshape(DEC_BATCH, DEC_SEQ, D_MODEL)
    return (y_prompt, y_sample, jnp.stack(kv_out, axis=1), jnp.stack(kr_out, axis=1), jnp.stack(ssm_out, axis=1))
```

```python
import functools
import math

import numpy as np
import jax
import jax.numpy as jnp
from jax import lax
from jax.experimental import pallas as pl
from jax.experimental.pallas import tpu as pltpu

F32 = jnp.float32
BF16 = jnp.bfloat16

D_MODEL = 1024
BATCH = 32
SEQ = 256
DEPTH = 2
DEC_BATCH = 8
DEC_SEQ = 1024
PAST_LEN = 256
GRID_W = 64
EPS = 1e-6
D_SSM = 512
SSM_GROUP = 16
N_SSM_GROUPS = 32
N_STATE = 64
N_HEADS = 8
NOPE_DIM = 64
ROPE_DIM = 32
V_DIM = 64
Q_LORA = 256
KV_LORA = 128
ROPE_BASE = 10000.0
D_CONV = 512
CONV_K = 31
N_EXPERTS = 32
TOP_K = 4
D_FF = 1024
SWIGLU_ALPHA = 1.702
SWIGLU_LIMIT = 7.0

T_P = BATCH * SEQ
T_S = DEC_BATCH * DEC_SEQ
T = T_P + T_S
TB = 256
NTB = T // TB
NTB_P = T_P // TB
TB_PER_S = DEC_SEQ // TB
SSM_STATE = N_SSM_GROUPS * N_STATE
PROJ_W = 5120
MOE_BM = 512
SEG_ALIGN = 8
MOE_NB = (T * TOP_K + (SEG_ALIGN - 1) * NTB * N_EXPERTS) // MOE_BM + N_EXPERTS
MOE_ROWS = MOE_NB * MOE_BM
VMEM_LIMIT = 56 * 1024 * 1024
BRANCH_DTYPE = BF16


def _cp(sem, vmem=None):
    return pltpu.CompilerParams(dimension_semantics=sem, vmem_limit_bytes=vmem)


def _sigmoid(x):
    return 0.5 * jnp.tanh(0.5 * x) + 0.5


def _mod_row(i):
    return jnp.where(i < NTB_P, DEC_BATCH, (i - NTB_P) // TB_PER_S)


def _ada_kernel(c_ref, w_ref, b_ref, o_ref):
    c = c_ref[...]
    s = c * _sigmoid(c)
    w = w_ref[...]
    s_hi = s.astype(BF16)
    s_lo = (s - s_hi.astype(F32)).astype(BF16)
    w_hi = w.astype(BF16)
    w_lo = (w - w_hi.astype(F32)).astype(BF16)
    o_ref[...] = (jnp.dot(s_hi, w_hi, preferred_element_type=F32)
                  + jnp.dot(s_lo, w_hi, preferred_element_type=F32)
                  + jnp.dot(s_hi, w_lo, preferred_element_type=F32)) + b_ref[...]


def _ada(cc, w_ada, b_ada3, l):
    tn = 1024
    return pl.pallas_call(
        _ada_kernel,
        grid=(6 * D_MODEL // tn,),
        in_specs=[pl.BlockSpec((16, D_MODEL), lambda j: (0, 0)),
                  pl.BlockSpec((None, D_MODEL, tn), lambda j: (l, 0, j)),
                  pl.BlockSpec((None, 1, tn), lambda j: (l, 0, j))],
        out_specs=pl.BlockSpec((16, tn), lambda j: (0, j)),
        out_shape=jax.ShapeDtypeStruct((16, 6 * D_MODEL), F32),
        compiler_params=_cp(("arbitrary",)),
        name="ada",
    )(cc, w_ada, b_ada3)


WIN_TM = 1024
WIN_TN = 1024
KR0 = D_SSM + Q_LORA + KV_LORA
IN_W = KR0 + ROPE_DIM + 2 * D_CONV + 3 * D_MODEL
PROJ_SHIFT = PROJ_W - IN_W


def _win_kernel(xa_ref, xb_ref, g_ref, mod_ref, wp_ref, wc_ref, o_ref, wb_ref, *, n_a):
    j = pl.program_id(0)
    i = pl.program_id(1)

    @pl.when((i == 0) & (j == 0))
    def _():
        w = wc_ref[...]
        kr = w[:, KR0:KR0 + ROPE_DIM]
        wb_ref[...] = jnp.concatenate(
            [w[:, :KR0 + ROPE_DIM], _rot_cols(kr), jnp.zeros((D_MODEL, PROJ_SHIFT - ROPE_DIM), F32)],
            axis=1).astype(BF16)

    @pl.when((i == 0) & (j > 0))
    def _():
        wb_ref[...] = jnp.concatenate([wp_ref[:, WIN_TN - PROJ_SHIFT:], wc_ref[:, :WIN_TN - PROJ_SHIFT]],
                                      axis=1).astype(BF16)

    x = jnp.where(i < n_a, xa_ref[...], xb_ref[...])
    y = x * lax.rsqrt(jnp.mean(x * x, axis=-1, keepdims=True) + EPS) * g_ref[...]
    h = (y * (1.0 + mod_ref[1:2, :]) + mod_ref[0:1, :]).astype(BF16)
    o_ref[...] = jnp.dot(h, wb_ref[...], preferred_element_type=F32)


def _win(xa, xb, g3, modt, w_in, l):
    n_a = xa.shape[0] // WIN_TM if xb is not xa else T // WIN_TM
    ni = T // WIN_TM
    nbp = T_P // WIN_TM
    return pl.pallas_call(
        functools.partial(_win_kernel, n_a=n_a),
        grid=(PROJ_W // WIN_TN, ni),
        in_specs=[pl.BlockSpec((WIN_TM, D_MODEL), lambda j, i: (jnp.minimum(i, n_a - 1), 0)),
                  pl.BlockSpec((WIN_TM, D_MODEL), lambda j, i: (jnp.maximum(i - n_a, 0), 0)),
                  pl.BlockSpec((None, 1, D_MODEL), lambda j, i: (l, 0, 0)),
                  pl.BlockSpec((None, 6, D_MODEL),
                               lambda j, i: (jnp.where(i < nbp, DEC_BATCH, i - nbp), 0, 0)),
                  pl.BlockSpec((None, D_MODEL, WIN_TN), lambda j, i: (l, 0, jnp.maximum(j - 1, 0))),
                  pl.BlockSpec((None, D_MODEL, WIN_TN), lambda j, i: (l, 0, j))],
        out_specs=pl.BlockSpec((WIN_TM, WIN_TN), lambda j, i: (i, j)),
        out_shape=jax.ShapeDtypeStruct((T, PROJ_W), F32),
        scratch_shapes=[pltpu.VMEM((D_MODEL, WIN_TN), BF16)],
        compiler_params=_cp(("arbitrary", "arbitrary"), VMEM_LIMIT),
        name="w_in",
    )(xa, xb, g3, modt, w_in, w_in)


def _s5_disc_kernel(lr_ref, li_ref, ldt_ref, ar_ref, ai_ref, fr_ref, fi_ref):
    dt = jnp.exp(ldt_ref[...])
    lr = lr_ref[...]
    li = li_ref[...]
    mag = jnp.exp(lr * dt)
    a_re = mag * jnp.cos(li * dt)
    a_im = mag * jnp.sin(li * dt)
    den = lr * lr + li * li
    ar_ref[...] = a_re
    ai_ref[...] = a_im
    fr_ref[...] = ((a_re - 1.0) * lr + a_im * li) / den
    fi_ref[...] = (a_im * lr - (a_re - 1.0) * li) / den


def _s5_disc(lam_re, lam_im, log_dt):
    rows = DEPTH * 2 * N_SSM_GROUPS
    shp = jax.ShapeDtypeStruct((rows, N_STATE), F32)
    return pl.pallas_call(
        _s5_disc_kernel,
        out_shape=(shp, shp, shp, shp),
        name="s5_disc",
    )(lam_re.reshape(rows, N_STATE), lam_im.reshape(rows, N_STATE), log_dt.reshape(rows, 1))


S5_CHUNK = 256
S5_SUB = 64
S5_NSUB = S5_CHUNK // S5_SUB
S5_NCHUNK = 8


def _s5_kernel(up_ref, usf_ref, usr_ref, wb_ref, wc_ref, are_ref, aim_ref, h0r_ref, h0i_ref,
               yf_ref, yr_ref, fin_ref,
               hre_ref, him_ref, ab_re, ab_im, xre_ref, xim_ref):
    k = pl.program_id(0)

    @pl.when(k < 4)
    def _():
        hre_ref[...] = jnp.zeros_like(hre_ref)
        him_ref[...] = jnp.zeros_like(him_ref)

    @pl.when(k == 4)
    def _():
        hre_ref[...] = h0r_ref[...]
        him_ref[...] = h0i_ref[...]

    for d, (u_ref, y_ref) in enumerate(((usf_ref, yf_ref), (usr_ref, yr_ref))):
        reverse = d == 1
        ab_re[...] = jnp.broadcast_to(are_ref[d], (8, SSM_STATE))
        ab_im[...] = jnp.broadcast_to(aim_ref[d], (8, SSM_STATE))

        def sub(s, carry, d=d, reverse=reverse, u_ref=u_ref, y_ref=y_ref):
            ss = (S5_NSUB - 1 - s) if reverse else s
            rows = pl.ds(pl.multiple_of(ss * S5_SUB, S5_SUB), S5_SUB)
            u2 = jnp.where(k < 4, up_ref[rows], u_ref[rows]).reshape(S5_SUB * 8, D_SSM).astype(BF16)
            for half in range(2):
                uh = u2[:, half * 256:(half + 1) * 256]
                cols = slice(half * 1024, (half + 1) * 1024)
                xre_ref[:, cols] = jnp.dot(uh, wb_ref[d, 0, half], preferred_element_type=F32)
                xim_ref[:, cols] = jnp.dot(uh, wb_ref[d, 1, half], preferred_element_type=F32)

            def step(t, h, reverse=reverse):
                tt = (S5_SUB - 1 - t) if reverse else t
                r = pl.ds(pl.multiple_of(tt * 8, 8), 8)
                h_re, h_im = h
                a_re = ab_re[...]
                a_im = ab_im[...]
                n_re = a_re * h_re - a_im * h_im + xre_ref[r, :]
                n_im = a_re * h_im + a_im * h_re + xim_ref[r, :]
                xre_ref[r, :] = n_re
                xim_ref[r, :] = n_im
                return n_re, n_im

            h_re, h_im = lax.fori_loop(0, S5_SUB, step, (hre_ref[d], him_ref[d]))
            hre_ref[d] = h_re
            him_ref[d] = h_im

            @pl.when((s == 0) & (k < 4))
            def _(d=d, reverse=reverse):
                r0 = (S5_SUB - 1) * 8 if reverse else 0
                fin_ref[d, 0] = xre_ref[r0:r0 + 8, :]
                fin_ref[d, 1] = xim_ref[r0:r0 + 8, :]

            x_re = xre_ref[...].astype(BF16)
            x_im = xim_ref[...].astype(BF16)
            for q in range(4):
                ks = slice(q * 512, (q + 1) * 512)
                yq = (jnp.dot(x_re[:, ks], wc_ref[d, 0, q], preferred_element_type=F32)
                      + jnp.dot(x_im[:, ks], wc_ref[d, 1, q], preferred_element_type=F32))
                y_ref[rows, :, q * 128:(q + 1) * 128] = yq.reshape(S5_SUB, 8, 128)
            return carry

        lax.fori_loop(0, S5_NSUB, sub, 0)


def _s5(u_p, u_s, wb, wc, a_re, a_im, h0_re, h0_im, l):
    def rev_chunk(k):
        return jnp.where(k < 4, k, 11 - k)

    blk = (None, S5_CHUNK, 8, D_SSM)
    y_shape = jax.ShapeDtypeStruct((S5_NCHUNK, S5_CHUNK, 8, D_SSM), F32)
    return pl.pallas_call(
        _s5_kernel,
        grid=(S5_NCHUNK,),
        in_specs=[pl.BlockSpec(blk, lambda k: (jnp.minimum(k, 3), 0, 0, 0)),
                  pl.BlockSpec(blk, lambda k: (jnp.maximum(k - 4, 0), 0, 0, 0)),
                  pl.BlockSpec(blk, lambda k: (jnp.minimum(7 - k, 3), 0, 0, 0)),
                  pl.BlockSpec((None, 2, 2, 2, 256, 1024), lambda k: (l, 0, 0, 0, 0, 0)),
                  pl.BlockSpec((None, 2, 2, 4, 512, 128), lambda k: (l, 0, 0, 0, 0, 0)),
                  pl.BlockSpec((None, 2, 1, SSM_STATE), lambda k: (l, 0, 0, 0)),
                  pl.BlockSpec((None, 2, 1, SSM_STATE), lambda k: (l, 0, 0, 0)),
                  pl.BlockSpec((2, 8, SSM_STATE), lambda k: (0, 0, 0)),
                  pl.BlockSpec((2, 8, SSM_STATE), lambda k: (0, 0, 0))],
        out_specs=[pl.BlockSpec(blk, lambda k: (k, 0, 0, 0)),
                   pl.BlockSpec(blk, lambda k: (rev_chunk(k), 0, 0, 0)),
                   pl.BlockSpec((None, 2, 2, 8, SSM_STATE), lambda k: (jnp.minimum(k, 3), 0, 0, 0, 0))],
        out_shape=(y_shape, y_shape,
                   jax.ShapeDtypeStruct((4, 2, 2, 8, SSM_STATE), F32)),
        scratch_shapes=[pltpu.VMEM((2, 8, SSM_STATE), F32), pltpu.VMEM((2, 8, SSM_STATE), F32),
                        pltpu.VMEM((8, SSM_STATE), F32), pltpu.VMEM((8, SSM_STATE), F32),
                        pltpu.VMEM((S5_SUB * 8, SSM_STATE), F32), pltpu.VMEM((S5_SUB * 8, SSM_STATE), F32)],
        compiler_params=_cp(("arbitrary",), VMEM_LIMIT),
        name="s5_scan",
    )(u_p, u_s, u_s, wb, wc, a_re, a_im, h0_re, h0_im)


GLU_TM = 1024


def _ssm_glu_kernel(u_ref, ysp_ref, yss_ref, d_ref, w_ref, b_ref, o_ref, w_b):
    i = pl.program_id(0)

    @pl.when(i == 0)
    def _():
        w_b[...] = w_ref[...].astype(BF16)

    ys = jnp.where(i < T_P // GLU_TM, ysp_ref[...], yss_ref[...])
    y = jax.nn.gelu(d_ref[...] * u_ref[...] + ys).astype(BF16)
    glu = jnp.dot(y, w_b[...], preferred_element_type=F32) + b_ref[...]
    o_ref[...] = (glu[:, :D_MODEL] * _sigmoid(glu[:, D_MODEL:])).astype(o_ref.dtype)


def _ssm_glu(proj, ys_p, ys_s, ssm_d3, w_glu, b_glu3, l):
    nbp = T_P // GLU_TM
    return pl.pallas_call(
        _ssm_glu_kernel,
        grid=(T // GLU_TM,),
        in_specs=[pl.BlockSpec((GLU_TM, D_SSM), lambda i: (i, 0)),
                  pl.BlockSpec((GLU_TM, D_SSM), lambda i: (jnp.minimum(i, nbp - 1), 0)),
                  pl.BlockSpec((GLU_TM, D_SSM), lambda i: (jnp.maximum(i - nbp, 0), 0)),
                  pl.BlockSpec((None, 1, D_SSM), lambda i: (l, 0, 0)),
                  pl.BlockSpec((None, D_SSM, 2 * D_MODEL), lambda i: (l, 0, 0)),
                  pl.BlockSpec((None, 1, 2 * D_MODEL), lambda i: (l, 0, 0))],
        out_specs=pl.BlockSpec((GLU_TM, D_MODEL), lambda i: (i, 0)),
        out_shape=jax.ShapeDtypeStruct((T, D_MODEL), BRANCH_DTYPE),
        scratch_shapes=[pltpu.VMEM((D_SSM, 2 * D_MODEL), BF16)],
        compiler_params=_cp(("arbitrary",), VMEM_LIMIT),
        name="ssm_glu",
    )(proj, ys_p, ys_s, ssm_d3, w_glu, b_glu3)


MLA_TM = DEC_SEQ


def _mla_prep_kernel(cq_ref, ckv_ref, kr_ref, qg_ref, kvg_ref, wqa_ref, wqb_ref,
                     ca_ref, sb_ref, ck_ref, sk_ref,
                     q_ref, ckvn_ref, krr_ref, wqa_b, wqb_b):
    @pl.when(pl.program_id(0) == 0)
    def _():
        wqa_b[...] = wqa_ref[...].astype(BF16)
        wqb_b[...] = wqb_ref[...].astype(BF16)

    cq = cq_ref[...]
    cqn = (cq * lax.rsqrt(jnp.mean(cq * cq, axis=-1, keepdims=True) + EPS) * qg_ref[...]).astype(BF16)
    qa = jnp.dot(cqn, wqa_b[...], preferred_element_type=F32)
    qb = jnp.dot(cqn, wqb_b[...], preferred_element_type=F32)
    ca = ca_ref[...]
    sb = sb_ref[...]
    for h in range(N_HEADS):
        cols = slice(h * 128, (h + 1) * 128)
        q_ref[:, cols] = (qa[:, cols] * ca + qb[:, cols] * sb).astype(q_ref.dtype)

    ckv = ckv_ref[...]
    ckvn_ref[...] = ckv * lax.rsqrt(jnp.mean(ckv * ckv, axis=-1, keepdims=True) + EPS) * kvg_ref[...]

    kr = kr_ref[...]
    krr_ref[...] = kr * ck_ref[...] + pltpu.roll(kr, 128 - ROPE_DIM, 1) * sk_ref[...]


def _mla_prep(proj, q_norm_g3, kv_norm_g3, wqa, wqb, tabs, l):
    ca, sb, ck, sk = tabs
    tab_spec = pl.BlockSpec((MLA_TM, 128), lambda i: (jnp.where(i < T_P // MLA_TM, 1, 0), 0))
    return pl.pallas_call(
        _mla_prep_kernel,
        grid=(T // MLA_TM,),
        in_specs=[pl.BlockSpec((MLA_TM, Q_LORA), lambda i: (i, 2)),
                  pl.BlockSpec((MLA_TM, 128), lambda i: (i, 6)),
                  pl.BlockSpec((MLA_TM, 128), lambda i: (i, 7)),
                  pl.BlockSpec((None, 1, Q_LORA), lambda i: (l, 0, 0)),
                  pl.BlockSpec((None, 1, KV_LORA), lambda i: (l, 0, 0)),
                  pl.BlockSpec((None, Q_LORA, 1024), lambda i: (l, 0, 0)),
                  pl.BlockSpec((None, Q_LORA, 1024), lambda i: (l, 0, 0)),
                  tab_spec, tab_spec, tab_spec, tab_spec],
        out_specs=[pl.BlockSpec((MLA_TM, 1024), lambda i: (i, 0)),
                   pl.BlockSpec((MLA_TM, 128), lambda i: (i, 0)),
                   pl.BlockSpec((MLA_TM, 128), lambda i: (i, 0))],
        out_shape=(jax.ShapeDtypeStruct((T, 1024), BF16),
                   jax.ShapeDtypeStruct((T, 128), F32),
                   jax.ShapeDtypeStruct((T, 128), F32)),
        scratch_shapes=[pltpu.VMEM((Q_LORA, 1024), BF16), pltpu.VMEM((Q_LORA, 1024), BF16)],
        compiler_params=_cp(("arbitrary",)),
        name="mla_prep",
    )(proj, proj, proj, q_norm_g3, kv_norm_g3, wqa, wqb, ca, sb, ck, sk)


ATT_TQ = 1024


def _attn_kernel(q_ref, kvin_ref, wk_ref, wv_ref, wo_ref, o_ref,
                 k_scr, v_scr, o_scr, wk_b, wv_b, wo_b):
    s_id = pl.program_id(0)
    qb = pl.program_id(1)

    @pl.when((s_id == 0) & (qb == 0))
    def _():
        wk_b[...] = wk_ref[...].astype(BF16)
        wv_b[...] = wv_ref[...].astype(BF16)
        wo_b[...] = wo_ref[...].astype(BF16)

    @pl.when(qb == 0)
    def _():
        kv = kvin_ref[...].astype(BF16)
        k_scr[...] = jnp.dot(kv, wk_b[...], preferred_element_type=F32).astype(BF16)
        v_scr[...] = jnp.dot(kv[:, :KV_LORA], wv_b[...], preferred_element_type=F32).astype(BF16)

    scale = (NOPE_DIM + ROPE_DIM) ** -0.5
    for h in range(N_HEADS):
        qh = q_ref[:, h * 128:(h + 1) * 128]
        kh = k_scr[:, h * 128:(h + 1) * 128]
        s = lax.dot_general(qh, kh, (((1,), (1,)), ((), ())), preferred_element_type=F32) * scale
        m = jnp.max(s, axis=-1, keepdims=True)
        p = jnp.exp(s - m)
        den = jnp.sum(p, axis=-1, keepdims=True)
        oh = jnp.dot(p.astype(BF16), v_scr[:, h * V_DIM:(h + 1) * V_DIM], preferred_element_type=F32)
        o_scr[:, h * V_DIM:(h + 1) * V_DIM] = oh / den
    o_ref[...] = jnp.dot(o_scr[...].astype(BF16), wo_b[...], preferred_element_type=F32).astype(o_ref.dtype)


def _attn(q, kvin, wk2, wv, w_o, l, nseq, nqb, tq, row_blk0):
    lk = kvin.shape[1]
    return pl.pallas_call(
        _attn_kernel,
        grid=(nseq, nqb),
        in_specs=[pl.BlockSpec((tq, 1024), lambda s, b: (row_blk0 + s * nqb + b, 0)),
                  pl.BlockSpec((None, lk, 256), lambda s, b: (s, 0, 0)),
                  pl.BlockSpec((None, 256, 1024), lambda s, b: (l, 0, 0)),
                  pl.BlockSpec((None, KV_LORA, 512), lambda s, b: (l, 0, 0)),
                  pl.BlockSpec((None, 512, D_MODEL), lambda s, b: (l, 0, 0))],
        out_specs=pl.BlockSpec((tq, D_MODEL), lambda s, b: (s * nqb + b, 0)),
        out_shape=jax.ShapeDtypeStruct((nseq * nqb * tq, D_MODEL), BRANCH_DTYPE),
        scratch_shapes=[pltpu.VMEM((lk, 1024), BF16), pltpu.VMEM((lk, 512), BF16),
                        pltpu.VMEM((tq, 512), F32),
                        pltpu.VMEM((256, 1024), BF16), pltpu.VMEM((KV_LORA, 512), BF16),
                        pltpu.VMEM((512, D_MODEL), BF16)],
        compiler_params=_cp(("arbitrary", "arbitrary"), VMEM_LIMIT),
        name="attn",
    )(q, kvin, wk2, wv, w_o)


CONV_PAD = 16
CONV_CB = 128
CONV_RB = 128


def _conv_kernel(cin_ref, w_ref, b_ref, lg_ref, lb_ref, pw_ref, pb_ref, o_ref,
                 pad_scr, conv_scr, act_scr, pw_b, *, seq):
    @pl.when(pl.program_id(0) == 0)
    def _():
        pw_b[...] = pw_ref[...].astype(BF16)
        pad_scr[0:CONV_PAD, :] = jnp.zeros((CONV_PAD, D_CONV), F32)
        pad_scr[CONV_PAD + seq:, :] = jnp.zeros((CONV_PAD, D_CONV), F32)

    pad_scr[CONV_PAD:CONV_PAD + seq, :] = cin_ref[:, :D_CONV] * _sigmoid(cin_ref[:, D_CONV:])

    win_rows = CONV_CB + 2 * CONV_PAD
    for c in range(D_CONV // 128):
        lanes = slice(c * 128, (c + 1) * 128)

        def cblk(rb, carry, lanes=lanes):
            base = pl.multiple_of(rb * CONV_CB, CONV_CB)
            win = pad_scr[pl.ds(base, win_rows), lanes]
            acc = jnp.broadcast_to(b_ref[:, lanes], (CONV_CB, 128))
            for b in range(8):
                wb = win if b == 0 else pltpu.roll(win, win_rows - b, 0)
                for a in range(2 * CONV_PAD // 8):
                    k = 8 * a + b - (CONV_PAD - CONV_K // 2)
                    if 0 <= k < CONV_K:
                        acc = acc + w_ref[k:k + 1, lanes] * wb[8 * a:8 * a + CONV_CB]
            conv_scr[pl.ds(base, CONV_CB), lanes] = acc
            return carry

        lax.fori_loop(0, seq // CONV_CB, cblk, 0)

    def blk(rb, carry):
        base = pl.multiple_of(rb * CONV_RB, CONV_RB)
        acc = conv_scr[pl.ds(base, CONV_RB), :]
        mu = jnp.mean(acc, axis=-1, keepdims=True)
        xc = acc - mu
        var = jnp.mean(xc * xc, axis=-1, keepdims=True)
        y = xc * lax.rsqrt(var + EPS) * lg_ref[...] + lb_ref[...]
        act_scr[pl.ds(base, CONV_RB), :] = (y * _sigmoid(y)).astype(BF16)
        return carry

    lax.fori_loop(0, seq // CONV_RB, blk, 0)
    o_ref[...] = (jnp.dot(act_scr[...], pw_b[...], preferred_element_type=F32) + pb_ref[...]).astype(o_ref.dtype)


def _conv(proj, dw_w, dw_b3, ln_g3, ln_b3, pw, pb3, l, nseq, seq, row_blk0):
    vec = lambda n: pl.BlockSpec((None, 1, n), lambda s: (l, 0, 0))
    return pl.pallas_call(
        functools.partial(_conv_kernel, seq=seq),
        grid=(nseq,),
        in_specs=[pl.BlockSpec((seq, 2 * D_CONV), lambda s: (row_blk0 + s, 1)),
                  pl.BlockSpec((None, CONV_K, D_CONV), lambda s: (l, 0, 0)),
                  vec(D_CONV), vec(D_CONV), vec(D_CONV),
                  pl.BlockSpec((None, D_CONV, D_MODEL), lambda s: (l, 0, 0)),
                  vec(D_MODEL)],
        out_specs=pl.BlockSpec((seq, D_MODEL), lambda s: (s, 0)),
        out_shape=jax.ShapeDtypeStruct((nseq * seq, D_MODEL), BRANCH_DTYPE),
        scratch_shapes=[pltpu.VMEM((seq + 2 * CONV_PAD, D_CONV), F32),
                        pltpu.VMEM((seq, D_CONV), F32),
                        pltpu.VMEM((seq, D_CONV), BF16),
                        pltpu.VMEM((D_CONV, D_MODEL), BF16)],
        compiler_params=_cp(("arbitrary",), VMEM_LIMIT),
        name="conv",
    )(proj, dw_w, dw_b3, ln_g3, ln_b3, pw, pb3)


def _merge_kernel(xa_ref, xb_ref, os_ref, omp_ref, oms_ref, ocp_ref, ocs_ref, ga_ref, gb_ref, gc_ref,
                  wo_ref, mod_ref, ng_ref, rw_ref, rb_ref,
                  x1_ref, h2_ref, tw_ref, sc_ref, sr_ref, nb_ref, wo_b, rw_hi, rw_lo, *, n_a):
    i = pl.program_id(0)

    @pl.when(i == 0)
    def _():
        wo_b[...] = wo_ref[...].astype(BF16)
        rw = rw_ref[...]
        rw_hi[...] = rw.astype(BF16)
        rw_lo[...] = (rw - rw_hi[...].astype(F32)).astype(BF16)

    is_prompt = i < NTB_P
    o_mla = jnp.where(is_prompt, omp_ref[...], oms_ref[...])
    o_conv = jnp.where(is_prompt, ocp_ref[...], ocs_ref[...])
    merged = (_sigmoid(ga_ref[...]) * os_ref[...]
              + _sigmoid(gb_ref[...]) * o_mla
              + _sigmoid(gc_ref[...]) * o_conv)
    out = jnp.dot(merged.astype(BF16), wo_b[...], preferred_element_type=F32)
    x1 = jnp.where(i < n_a, xa_ref[...], xb_ref[...]) + mod_ref[2:3, :] * out
    x1_ref[...] = x1
    y = x1 * lax.rsqrt(jnp.mean(x1 * x1, axis=-1, keepdims=True) + EPS) * ng_ref[...]
    h2 = y * (1.0 + mod_ref[4:5, :]) + mod_ref[3:4, :]
    h2_hi = h2.astype(BF16)
    h2_ref[...] = h2_hi

    h2_lo = (h2 - h2_hi.astype(F32)).astype(BF16)
    logits = (jnp.dot(h2_hi, rw_hi[...], preferred_element_type=F32)
              + jnp.dot(h2_lo, rw_hi[...], preferred_element_type=F32)
              + jnp.dot(h2_hi, rw_lo[...], preferred_element_type=F32)) + rb_ref[...]
    lane_i = lax.broadcasted_iota(jnp.int32, logits.shape, 1)
    lane = lane_i.astype(F32)
    cur = logits
    vals, idxs = [], []
    for _ in range(TOP_K):
        m = jnp.max(cur, axis=-1, keepdims=True)
        idx = jnp.min(jnp.where(cur == m, lane, 128.0), axis=-1, keepdims=True)
        vals.append(m)
        idxs.append(idx)
        cur = jnp.where(lane == idx, -jnp.inf, cur)
    exps = [jnp.exp(v - vals[0]) for v in vals]
    tot = exps[0] + exps[1] + exps[2] + exps[3]
    tw = jnp.zeros(logits.shape, F32)
    for k in range(TOP_K):
        tw = jnp.where(lane_i == k, exps[k] / tot, tw)
    tw_ref[...] = tw

    ohs = [jnp.where(lane == idxs[k], 1.0, 0.0) for k in range(TOP_K)]
    cnts = [jnp.sum(oh, axis=0, keepdims=True) for oh in ohs]
    n_b = jnp.ceil((cnts[0] + cnts[1] + cnts[2] + cnts[3]) * (1.0 / SEG_ALIGN)) * SEG_ALIGN
    inc = jnp.broadcast_to(n_b, (8, 128))
    lane8 = lax.broadcasted_iota(jnp.int32, (8, 128), 1)
    for s in (1, 2, 4, 8, 16, 32, 64):
        inc = inc + jnp.where(lane8 >= s, pltpu.roll(inc, s, 1), 0.0)
    before = (inc - n_b)[0:1, :]
    r_i = lax.broadcasted_iota(jnp.int32, (TB, TB), 0)
    c_i = lax.broadcasted_iota(jnp.int32, (TB, TB), 1)
    earlier = jnp.where(r_i > c_i, 1.0, 0.0).astype(BF16)
    slots = jnp.zeros(logits.shape, F32)
    for k in range(TOP_K):
        pre = jnp.dot(earlier, ohs[k].astype(BF16), preferred_element_type=F32)
        slot_k = jnp.sum(ohs[k] * (pre + before), axis=-1, keepdims=True)
        slots = jnp.where(lane_i == k, slot_k, slots)
        before = before + cnts[k]
    sc_ref[...] = slots.astype(jnp.int32)
    sr_ref[...] = slots.T[0:8, :].astype(jnp.int32)
    nb_ref[...] = jnp.broadcast_to(n_b, (8, 128)).astype(jnp.int32)


def _merge(xa, xb, o_ssm, o_mla_p, o_mla_s, o_conv_p, o_conv_s, proj, w_out, modt, norm_ffn_g3, rw_p, rb_p, l):
    n_a = xa.shape[0] // TB if xb is not xa else NTB
    row = lambda i: (i, 0)
    row_p = lambda i: (jnp.minimum(i, NTB_P - 1), 0)
    row_s = lambda i: (jnp.maximum(i - NTB_P, 0), 0)
    blk = lambda m: pl.BlockSpec((TB, D_MODEL), m)
    return pl.pallas_call(
        functools.partial(_merge_kernel, n_a=n_a),
        grid=(NTB,),
        in_specs=[blk(lambda i: (jnp.minimum(i, n_a - 1), 0)), blk(lambda i: (jnp.maximum(i - n_a, 0), 0)),
                  blk(row), blk(row_p), blk(row_s), blk(row_p), blk(row_s),
                  blk(lambda i: (i, 2)), blk(lambda i: (i, 3)), blk(lambda i: (i, 4)),
                  pl.BlockSpec((None, D_MODEL, D_MODEL), lambda i: (l, 0, 0)),
                  pl.BlockSpec((None, 6, D_MODEL), lambda i: (_mod_row(i), 0, 0)),
                  pl.BlockSpec((None, 1, D_MODEL), lambda i: (l, 0, 0)),
                  pl.BlockSpec((None, D_MODEL, 128), lambda i: (l, 0, 0)),
                  pl.BlockSpec((None, 1, 128), lambda i: (l, 0, 0))],
        out_specs=[blk(row), blk(row), pl.BlockSpec((TB, 128), row), pl.BlockSpec((TB, 128), row),
                   pl.BlockSpec((None, 8, TB), lambda i: (i, 0, 0)),
                   pl.BlockSpec((None, 8, 128), lambda i: (i, 0, 0))],
        out_shape=(jax.ShapeDtypeStruct((T, D_MODEL), F32), jax.ShapeDtypeStruct((T, D_MODEL), BF16),
                   jax.ShapeDtypeStruct((T, 128), F32), jax.ShapeDtypeStruct((T, 128), jnp.int32),
                   jax.ShapeDtypeStruct((NTB, 8, TB), jnp.int32),
                   jax.ShapeDtypeStruct((NTB, 8, 128), jnp.int32)),
        scratch_shapes=[pltpu.VMEM((D_MODEL, D_MODEL), BF16),
                        pltpu.VMEM((D_MODEL, 128), BF16), pltpu.VMEM((D_MODEL, 128), BF16)],
        compiler_params=_cp(("arbitrary",), VMEM_LIMIT),
        name="merge",
    )(xa, xb, o_ssm, o_mla_p, o_mla_s, o_conv_p, o_conv_s, proj, proj, proj, w_out, modt, norm_ffn_g3, rw_p, rb_p)


PAIRS = TB * TOP_K
G_ROWS = -(-(PAIRS + (SEG_ALIGN - 1) * N_EXPERTS) // 256) * 256
SEG_BITS = 9


def _aligned(off):
    return off if isinstance(off, int) else pl.multiple_of(off, SEG_ALIGN)


def _segment_copies(src_ref, dst_ref, sem, n, src_off, dst_off):
    copies = []
    done = 0
    for bit in range(SEG_BITS - 1, SEG_ALIGN.bit_length() - 2, -1):
        size = 1 << bit
        part = n & size
        copies.append((part != 0,
                       pltpu.make_async_copy(src_ref.at[pl.ds(_aligned(src_off + done), size)],
                                             dst_ref.at[pl.ds(_aligned(dst_off + done), size)], sem)))
        done = done + part
    return copies


def _start_all(copies):
    for pred, cp in copies:
        pl.when(pred)(cp.start)


def _wait_all(copies):
    for pred, cp in copies:
        pl.when(pred)(cp.wait)


def _dispatch_kernel(seg_ref, nbe_ref, cnt_ref, pst_ref, nu_ref, h_ref, sr_ref, xs_ref, g_scr, z_scr, sem):
    b = pl.program_id(0)
    zsem = sem.at[2]

    def copies(blk, e, off):
        n = nbe_ref[blk * N_EXPERTS + e]
        dst = seg_ref[blk * N_EXPERTS + e]
        slot = blk % 2
        return n, _segment_copies(g_scr.at[slot], xs_ref, sem.at[slot], n, off, dst)

    def drain(blk):
        def wait(e, off):
            n, cps = copies(blk, e, off)
            _wait_all(cps)
            return off + n

        lax.fori_loop(0, N_EXPERTS, wait, 0)

    @pl.when(b < NTB)
    def _():
        sub = lax.broadcasted_iota(jnp.int32, (G_ROWS, TB), 0)
        sr = sr_ref[...]
        hit = sub == sr[0:1, :]
        for k in range(1, TOP_K):
            hit = hit | (sub == sr[k:k + 1, :])
        g_scr[b % 2] = jnp.dot(jnp.where(hit, 1.0, 0.0).astype(BF16), h_ref[...], preferred_element_type=F32)

        def start(e, off):
            n, cps = copies(b, e, off)
            _start_all(cps)
            return off + n

        lax.fori_loop(0, N_EXPERTS, start, 0)

    @pl.when(b > 0)
    def _():
        drain(b - 1)

    @pl.when(b == NTB)
    def _():
        sem = zsem
        z_scr[...] = jnp.zeros_like(z_scr)

        def pad(e):
            cnt = cnt_ref[e]
            padded = (cnt + MOE_BM - 1) // MOE_BM * MOE_BM
            return padded - cnt, pst_ref[e] + cnt

        def start(e, carry):
            n, dst = pad(e)
            _start_all(_segment_copies(z_scr, xs_ref, sem, n, 0, dst))
            return carry

        def wait(e, carry):
            n, dst = pad(e)
            _wait_all(_segment_copies(z_scr, xs_ref, sem, n, 0, dst))
            return carry

        lax.fori_loop(0, N_EXPERTS, start, 0)
        lax.fori_loop(0, N_EXPERTS, wait, 0)

        def tail_copy(blk):
            return pltpu.make_async_copy(z_scr, xs_ref.at[pl.ds(blk * MOE_BM, MOE_BM)], sem)

        def tail_start(blk, carry):
            tail_copy(blk).start()
            return carry

        def tail_wait(blk, carry):
            tail_copy(blk).wait()
            return carry

        lax.fori_loop(nu_ref[0], MOE_NB, tail_start, 0)
        lax.fori_loop(nu_ref[0], MOE_NB, tail_wait, 0)


def _dispatch(seg_start, n_be, counts, pad_start, n_used, h2, slot_row):
    last = lambda b, *_: (jnp.minimum(b, NTB - 1), 0)
    grid_spec = pltpu.PrefetchScalarGridSpec(
        num_scalar_prefetch=5,
        grid=(NTB + 1,),
        in_specs=[pl.BlockSpec((TB, D_MODEL), last),
                  pl.BlockSpec((None, 8, TB), lambda b, *_: (jnp.minimum(b, NTB - 1), 0, 0))],
        out_specs=pl.BlockSpec(memory_space=pl.ANY),
        scratch_shapes=[pltpu.VMEM((2, G_ROWS, D_MODEL), F32),
                        pltpu.VMEM((MOE_BM, D_MODEL), F32),
                        pltpu.SemaphoreType.DMA((3,))],
    )
    return pl.pallas_call(
        _dispatch_kernel,
        grid_spec=grid_spec,
        out_shape=jax.ShapeDtypeStruct((MOE_ROWS, D_MODEL), F32),
        compiler_params=_cp(("arbitrary",), VMEM_LIMIT),
        name="moe_dispatch",
    )(seg_start, n_be, counts, pad_start, n_used, h2, slot_row)


def _expert_kernel(be_ref, slot_ref, nu_ref, x_ref, w1_ref, b1_ref, w2_ref, b2_ref, y_ref, w1_b, w2_b):
    i = pl.program_id(0)
    blk = i - 1
    cur = be_ref[jnp.maximum(blk, 0)]
    nxt = be_ref[jnp.minimum(i, MOE_NB - 1)]

    @pl.when((i == 0) | (nxt != cur))
    def _():
        w1_b[slot_ref[jnp.minimum(i, MOE_NB - 1)]] = w1_ref[...].astype(BF16)

    @pl.when((blk == 0) | ((blk > 0) & (cur != be_ref[jnp.maximum(blk - 1, 0)])))
    def _():
        w2_b[...] = w2_ref[...].astype(BF16)

    @pl.when((blk >= 0) & (blk < nu_ref[0]))
    def _():
        w1 = w1_b[slot_ref[jnp.maximum(blk, 0)]]
        gu = jnp.dot(x_ref[...].astype(BF16), w1, preferred_element_type=F32) + b1_ref[...]
        g = jnp.minimum(gu[:, :D_FF], SWIGLU_LIMIT)
        u = jnp.clip(gu[:, D_FF:], -SWIGLU_LIMIT, SWIGLU_LIMIT)
        act = (u + 1.0) * (g * _sigmoid(SWIGLU_ALPHA * g))
        y_ref[...] = jnp.dot(act.astype(BF16), w2_b[...], preferred_element_type=F32) + b2_ref[...]

    @pl.when(blk >= nu_ref[0])
    def _():
        y_ref[...] = jnp.zeros_like(y_ref)


def _experts(block_e, w1_slot, n_used, xs, w1, b1_4, w2, b2_4, l):
    prev = lambda i: jnp.maximum(i - 1, 0)
    grid_spec = pltpu.PrefetchScalarGridSpec(
        num_scalar_prefetch=3,
        grid=(MOE_NB + 1,),
        in_specs=[pl.BlockSpec((MOE_BM, D_MODEL), lambda i, be, sl, nu: (jnp.minimum(prev(i), nu[0] - 1), 0)),
                  pl.BlockSpec((None, None, D_MODEL, 2 * D_FF),
                               lambda i, be, sl, nu: (l, be[jnp.minimum(i, MOE_NB - 1)], 0, 0)),
                  pl.BlockSpec((None, None, 1, 2 * D_FF), lambda i, be, sl, nu: (l, be[prev(i)], 0, 0)),
                  pl.BlockSpec((None, None, D_FF, D_MODEL), lambda i, be, sl, nu: (l, be[prev(i)], 0, 0)),
                  pl.BlockSpec((None, None, 1, D_MODEL), lambda i, be, sl, nu: (l, be[prev(i)], 0, 0))],
        out_specs=pl.BlockSpec((MOE_BM, D_MODEL), lambda i, be, sl, nu: (prev(i), 0)),
        scratch_shapes=[pltpu.VMEM((2, D_MODEL, 2 * D_FF), BF16), pltpu.VMEM((D_FF, D_MODEL), BF16)],
    )
    return pl.pallas_call(
        _expert_kernel,
        grid_spec=grid_spec,
        out_shape=jax.ShapeDtypeStruct((MOE_ROWS, D_MODEL), F32),
        compiler_params=_cp(("arbitrary",), VMEM_LIMIT),
        name="moe_experts",
    )(block_e, w1_slot, n_used, xs, w1, b1_4, w2, b2_4)


def _combine_kernel(seg_ref, nbe_ref, x1_ref, tw_ref, sc_ref, mod_ref, fg_ref, ys_ref, *rest, final):
    if final:
        yp_ref, ys_out_ref, g_scr, sem = rest
    else:
        x2_ref, g_scr, sem = rest
    b = pl.program_id(0)

    def copies(blk, e, off):
        n = nbe_ref[blk * N_EXPERTS + e]
        src = seg_ref[blk * N_EXPERTS + e]
        slot = blk % 2
        return n, _segment_copies(ys_ref, g_scr.at[slot], sem.at[slot], n, src, off)

    def fetch(blk):
        g_scr[blk % 2, PAIRS:, :] = jnp.zeros((G_ROWS - PAIRS, D_MODEL), F32)

        def start(e, off):
            n, cps = copies(blk, e, off)
            _start_all(cps)
            return off + n

        lax.fori_loop(0, N_EXPERTS, start, 0)

    @pl.when(b == 0)
    def _():
        fetch(b)

    @pl.when(b + 1 < NTB)
    def _():
        fetch(b + 1)

    def wait(e, off):
        n, cps = copies(b, e, off)
        _wait_all(cps)
        return off + n

    lax.fori_loop(0, N_EXPERTS, wait, 0)

    sc = sc_ref[...]
    tw = tw_ref[...]
    lane = lax.broadcasted_iota(jnp.int32, (TB, G_ROWS), 1)
    pw = jnp.zeros((TB, G_ROWS), F32)
    for k in range(TOP_K):
        pw = pw + jnp.where(lane == sc[:, k:k + 1], tw[:, k:k + 1], 0.0)
    moe = jnp.dot(pw.astype(BF16), g_scr[b % 2].astype(BF16), preferred_element_type=F32)
    x2 = x1_ref[...] + mod_ref[5:6, :] * moe
    if final:
        yn = x2 * lax.rsqrt(jnp.mean(x2 * x2, axis=-1, keepdims=True) + EPS) * fg_ref[...]

        @pl.when(b < NTB_P)
        def _():
            yp_ref[...] = yn

        @pl.when(b >= NTB_P)
        def _():
            ys_out_ref[...] = yn
    else:
        x2_ref[...] = x2


def _combine(seg_start, n_be, x1, topw, slot_col, modt, final_g2, ys, final):
    row = lambda b, *_: (b, 0)
    out_blk = pl.BlockSpec((TB, D_MODEL), row)
    out_sds = jax.ShapeDtypeStruct((T, D_MODEL), F32)
    fin_blks = [pl.BlockSpec((TB, D_MODEL), lambda b, *_: (jnp.minimum(b, NTB_P - 1), 0)),
                pl.BlockSpec((TB, D_MODEL), lambda b, *_: (jnp.maximum(b - NTB_P, 0), 0))]
    fin_sds = (jax.ShapeDtypeStruct((T_P, D_MODEL), F32), jax.ShapeDtypeStruct((T_S, D_MODEL), F32))
    grid_spec = pltpu.PrefetchScalarGridSpec(
        num_scalar_prefetch=2,
        grid=(NTB,),
        in_specs=[pl.BlockSpec((TB, D_MODEL), row),
                  pl.BlockSpec((TB, 128), row),
                  pl.BlockSpec((TB, 128), row),
                  pl.BlockSpec((None, 6, D_MODEL), lambda b, *_: (_mod_row(b), 0, 0)),
                  pl.BlockSpec((1, D_MODEL), lambda b, *_: (0, 0)),
                  pl.BlockSpec(memory_space=pl.ANY)],
        out_specs=fin_blks if final else [out_blk],
        scratch_shapes=[pltpu.VMEM((2, G_ROWS, D_MODEL), F32), pltpu.SemaphoreType.DMA((2,))],
    )
    return pl.pallas_call(
        functools.partial(_combine_kernel, final=final),
        grid_spec=grid_spec,
        out_shape=fin_sds if final else (out_sds,),
        compiler_params=_cp(("arbitrary",), VMEM_LIMIT),
        name="moe_combine",
    )(seg_start, n_be, x1, topw, slot_col, modt, final_g2, ys)


def _rot_cols(w):
    q = ROPE_DIM // 4
    parts = []
    for half in range(2):
        a = w[..., half * 2 * q:half * 2 * q + q]
        b = w[..., half * 2 * q + q:(half + 1) * 2 * q]
        parts += [-b, a]
    return jnp.concatenate(parts, axis=-1)


def _rope_tables():
    rows = DEC_SEQ // GRID_W
    row = jnp.repeat(jnp.arange(rows), GRID_W).astype(F32)
    col = jnp.tile(jnp.arange(GRID_W), rows).astype(F32)
    half = ROPE_DIM // 2
    freqs = ROPE_BASE ** (-jnp.arange(0, half, 2, dtype=F32) / half)
    ang_r = row[:, None] * freqs
    ang_c = col[:, None] * freqs
    ang = jnp.concatenate([ang_r, ang_r, ang_c, ang_c], axis=-1)
    cos = jnp.concatenate([jnp.cos(ang), jnp.ones((MLA_TM, ROPE_DIM), F32)], axis=0)
    sin = jnp.concatenate([jnp.sin(ang), jnp.zeros((MLA_TM, ROPE_DIM), F32)], axis=0)
    n = cos.shape[0]
    z = lambda w: jnp.zeros((n, w), F32)
    ca = jnp.concatenate([jnp.ones((n, NOPE_DIM), F32), cos, z(32)], axis=1)
    sb = jnp.concatenate([z(NOPE_DIM), sin, z(32)], axis=1)
    ck = jnp.concatenate([cos, z(96)], axis=1)
    sk = jnp.concatenate([sin, z(96)], axis=1)
    return ca, sb, ck, sk


def kernel(x_prompt, x_sample, c, cache_kv_latent, cache_k_rope, state_ssm, c_ctx, norm_mix_g, norm_ffn_g, final_norm_g, w_ada, b_ada, w_in, q_norm_g, w_uq, kv_norm_g, w_ukv, w_o_mla, ssm_lam_re, ssm_lam_im, ssm_log_dt, ssm_b_re, ssm_b_im, ssm_c_re, ssm_c_im, ssm_d, w_ssm_glu, b_ssm_glu, conv_dw_w, conv_dw_b, conv_ln_g, conv_ln_b, w_conv_pw2, b_conv_pw2, w_out, router_w, router_b, moe_w1, moe_b1, moe_w2, moe_b2):
    L = DEPTH
    v3 = lambda a: a.reshape(L, 1, a.shape[-1])

    xa, xb = x_prompt.reshape(T_P, D_MODEL), x_sample.reshape(T_S, D_MODEL)
    cc = jnp.concatenate([c, c_ctx[None, :], jnp.zeros((16 - DEC_BATCH - 1, D_MODEL), F32)], axis=0)


    wq = w_uq.reshape(L, Q_LORA, N_HEADS, NOPE_DIM + ROPE_DIM)
    zq = lambda w: jnp.zeros((L, Q_LORA, N_HEADS, w), F32)
    wqa = jnp.concatenate([wq, zq(32)], axis=-1).reshape(L, Q_LORA, 1024)
    wqb = jnp.concatenate([zq(NOPE_DIM), _rot_cols(wq[..., NOPE_DIM:]), zq(32)], axis=-1).reshape(L, Q_LORA, 1024)
    wkv = w_ukv.reshape(L, KV_LORA, N_HEADS, NOPE_DIM + V_DIM)
    wk_lat = jnp.concatenate([wkv[..., :NOPE_DIM], jnp.zeros((L, KV_LORA, N_HEADS, 64), F32)], axis=-1)
    place = jnp.concatenate([jnp.zeros((ROPE_DIM, NOPE_DIM), F32), jnp.eye(ROPE_DIM, dtype=F32),
                             jnp.zeros((ROPE_DIM, 32), F32)], axis=-1)
    wk_rope = jnp.broadcast_to(place[None, :, None, :], (L, ROPE_DIM, N_HEADS, 128))
    wk2 = jnp.concatenate([wk_lat, wk_rope, jnp.zeros((L, 96, N_HEADS, 128), F32)], axis=1).reshape(L, 256, 1024)
    wv = wkv[..., NOPE_DIM:].reshape(L, KV_LORA, N_HEADS * V_DIM)
    tabs = _rope_tables()

    a_re, a_im, f_re, f_im = [v.reshape(L, 2, N_SSM_GROUPS, N_STATE)
                              for v in _s5_disc(ssm_lam_re, ssm_lam_im, ssm_log_dt)]
    bb_re = f_re[..., None] * ssm_b_re - f_im[..., None] * ssm_b_im
    bb_im = f_re[..., None] * ssm_b_im + f_im[..., None] * ssm_b_re
    eye16 = jnp.eye(16, dtype=F32)
    eye8 = jnp.eye(8, dtype=F32)

    def bd_in(bb):
        bb = bb.reshape(L, 2, 2, 16, N_STATE, SSM_GROUP)
        return jnp.einsum('ldhgpc,gk->ldhgckp', bb, eye16).reshape(L, 2, 2, 256, 1024)

    def bd_out(cm):
        cm = cm.reshape(L, 2, 4, 8, SSM_GROUP, N_STATE)
        return jnp.einsum('ldqgcp,gk->ldqgpkc', cm, eye8).reshape(L, 2, 4, 512, 128)

    wb = jnp.stack([bd_in(bb_re), bd_in(bb_im)], axis=2).astype(BF16)
    wc = jnp.stack([bd_out(ssm_c_re), bd_out(-ssm_c_im)], axis=2).astype(BF16)
    a_re4 = a_re.reshape(L, 2, 1, SSM_STATE)
    a_im4 = a_im.reshape(L, 2, 1, SSM_STATE)

    rw_p = jnp.concatenate([router_w, jnp.zeros((L, D_MODEL, 128 - N_EXPERTS), F32)], axis=-1)
    rb_p = jnp.concatenate([router_b, jnp.full((L, 128 - N_EXPERTS), -1e30, F32)], axis=-1).reshape(L, 1, 128)

    kv_out, kr_out, ssm_out = [], [], []
    for l in range(L):
        modt = _ada(cc, w_ada, v3(b_ada), l).reshape(16, 6, D_MODEL)
        proj = _win(xa, xb, v3(norm_mix_g), modt, w_in, l)

        u_p = proj[:T_P, :D_SSM].reshape(4, 8, SEQ, D_SSM).transpose(0, 2, 1, 3)
        u_s = proj[T_P:, :D_SSM].reshape(DEC_BATCH, 4, S5_CHUNK, D_SSM).transpose(1, 2, 0, 3)
        st = state_ssm[:, l]
        h0 = st.transpose(4, 1, 0, 2, 3).reshape(2, 2, DEC_BATCH, SSM_STATE)
        y_f, y_r, fin = _s5(u_p, u_s, wb, wc, a_re4, a_im4, h0[0], h0[1], l)
        ysum = y_f + y_r
        o_ssm = _ssm_glu(proj, ysum[:4].transpose(0, 2, 1, 3).reshape(T_P, D_SSM),
                         ysum[4:].transpose(2, 0, 1, 3).reshape(T_S, D_SSM),
                         v3(ssm_d), w_ssm_glu, v3(b_ssm_glu), l)
        ssm_out.append(fin.reshape(4, 2, 2, 8, N_SSM_GROUPS, N_STATE).transpose(0, 3, 1, 4, 5, 2)
                       .reshape(BATCH, 2, N_SSM_GROUPS, N_STATE, 2))

        q, ckvn, krr = _mla_prep(proj, v3(q_norm_g), v3(kv_norm_g), wqa, wqb, tabs, l)
        kvin_p = jnp.concatenate([ckvn[:T_P], krr[:T_P]], axis=-1).reshape(BATCH, SEQ, 256)
        cache_kr = jnp.concatenate([cache_k_rope[:, l], jnp.zeros((DEC_BATCH, PAST_LEN, 96), F32)], axis=-1)
        kvin_s = jnp.concatenate([
            jnp.concatenate([cache_kv_latent[:, l], cache_kr], axis=-1),
            jnp.concatenate([ckvn[T_P:], krr[T_P:]], axis=-1).reshape(DEC_BATCH, DEC_SEQ, 256)], axis=1)
        o_mla_p = _attn(q, kvin_p, wk2, wv, w_o_mla, l, BATCH, 1, SEQ, 0)
        o_mla_s = _attn(q, kvin_s, wk2, wv, w_o_mla, l, DEC_BATCH, DEC_SEQ // ATT_TQ, ATT_TQ, T_P // ATT_TQ)
        kv_out.append(ckvn[:T_P].reshape(BATCH, SEQ, KV_LORA))
        kr_out.append(proj[:T_P, KR0:KR0 + ROPE_DIM].reshape(BATCH, SEQ, ROPE_DIM))

        conv_args = (conv_dw_w, v3(conv_dw_b), v3(conv_ln_g), v3(conv_ln_b), w_conv_pw2, v3(b_conv_pw2), l)
        o_conv_p = _conv(proj, *conv_args, BATCH, SEQ, 0)
        o_conv_s = _conv(proj, *conv_args, DEC_BATCH, DEC_SEQ, T_P // DEC_SEQ)

        x1, h2, topw, slot_col, slot_row, n_blk = _merge(
            xa, xb, o_ssm, o_mla_p, o_mla_s, o_conv_p, o_conv_s, proj, w_out, modt, v3(norm_ffn_g), rw_p, rb_p, l)

        n_be = n_blk[:, 0, :N_EXPERTS]
        counts = jnp.sum(n_be, axis=0)
        padded = (counts + MOE_BM - 1) // MOE_BM * MOE_BM
        pad_end = jnp.cumsum(padded)
        pad_start = pad_end - padded
        seg_start = (pad_start[None, :] + jnp.cumsum(n_be, axis=0) - n_be).reshape(-1).astype(jnp.int32)
        blk_row = jnp.arange(MOE_NB, dtype=jnp.int32) * MOE_BM
        block_e = jnp.minimum(jnp.sum((pad_end[None, :] <= blk_row[:, None]).astype(jnp.int32), axis=1),
                              N_EXPERTS - 1).astype(jnp.int32)
        n_used = (pad_end[-1:] // MOE_BM).astype(jnp.int32)
        blk_id = jnp.arange(MOE_NB, dtype=jnp.int32)
        block_e = jnp.where(blk_id < n_used[0], block_e, jnp.max(jnp.where(blk_id < n_used[0], block_e, 0)))
        switches = jnp.concatenate([jnp.zeros((1,), jnp.int32), (block_e[1:] != block_e[:-1]).astype(jnp.int32)])
        w1_slot = (jnp.cumsum(switches) % 2).astype(jnp.int32)
        n_be = n_be.reshape(-1)

        xs = _dispatch(seg_start, n_be, counts.astype(jnp.int32), pad_start.astype(jnp.int32), n_used, h2, slot_row)
        ys = _experts(block_e, w1_slot, n_used, xs, moe_w1, moe_b1.reshape(L, N_EXPERTS, 1, 2 * D_FF),
                      moe_w2, moe_b2.reshape(L, N_EXPERTS, 1, D_MODEL), l)
        outs = _combine(seg_start, n_be, x1, topw, slot_col, modt, final_norm_g.reshape(1, D_MODEL), ys,
                        final=(l == L - 1))
        if l < L - 1:
            xa = xb = outs[0]

    y_prompt = outs[0].reshape(BATCH, SEQ, D_MODEL)
    y_sample = outs[1].reshape(DEC_BATCH, DEC_SEQ, D_MODEL)
    return (y_prompt, y_sample, jnp.stack(kv_out, axis=1), jnp.stack(kr_out, axis=1), jnp.stack(ssm_out, axis=1))
```

```python
import functools
import math

import numpy as np
import jax
import jax.numpy as jnp
from jax import lax
from jax.experimental import pallas as pl
from jax.experimental.pallas import tpu as pltpu

F32 = jnp.float32
BF16 = jnp.bfloat16

D_MODEL = 1024
BATCH = 32
SEQ = 256
DEPTH = 2
DEC_BATCH = 8
DEC_SEQ = 1024
PAST_LEN = 256
GRID_W = 64
EPS = 1e-6
D_SSM = 512
SSM_GROUP = 16
N_SSM_GROUPS = 32
N_STATE = 64
N_HEADS = 8
NOPE_DIM = 64
ROPE_DIM = 32
V_DIM = 64
Q_LORA = 256
KV_LORA = 128
ROPE_BASE = 10000.0
D_CONV = 512
CONV_K = 31
N_EXPERTS = 32
TOP_K = 4
D_FF = 1024
SWIGLU_ALPHA = 1.702
SWIGLU_LIMIT = 7.0

T_P = BATCH * SEQ
T_S = DEC_BATCH * DEC_SEQ
T = T_P + T_S
TB = 256
NTB = T // TB
NTB_P = T_P // TB
TB_PER_S = DEC_SEQ // TB
SSM_STATE = N_SSM_GROUPS * N_STATE
PROJ_W = 5120
MOE_BM = 512
SEG_ALIGN = 8
MOE_NB = (T * TOP_K + (SEG_ALIGN - 1) * NTB * N_EXPERTS) // MOE_BM + N_EXPERTS
MOE_ROWS = MOE_NB * MOE_BM
VMEM_LIMIT = 56 * 1024 * 1024
BRANCH_DTYPE = BF16


def _cp(sem, vmem=None):
    return pltpu.CompilerParams(dimension_semantics=sem, vmem_limit_bytes=vmem)


def _sigmoid(x):
    return 0.5 * jnp.tanh(0.5 * x) + 0.5


def _mod_row(i):
    return jnp.where(i < NTB_P, DEC_BATCH, (i - NTB_P) // TB_PER_S)


def _ada_kernel(c_ref, w_ref, b_ref, o_ref):
    c = c_ref[...]
    s = c * _sigmoid(c)
    w = w_ref[...]
    s_hi = s.astype(BF16)
    s_lo = (s - s_hi.astype(F32)).astype(BF16)
    w_hi = w.astype(BF16)
    w_lo = (w - w_hi.astype(F32)).astype(BF16)
    o_ref[...] = (jnp.dot(s_hi, w_hi, preferred_element_type=F32)
                  + jnp.dot(s_lo, w_hi, preferred_element_type=F32)
                  + jnp.dot(s_hi, w_lo, preferred_element_type=F32)) + b_ref[...]


def _ada(cc, w_ada, b_ada3, l):
    tn = 1024
    return pl.pallas_call(
        _ada_kernel,
        grid=(6 * D_MODEL // tn,),
        in_specs=[pl.BlockSpec((16, D_MODEL), lambda j: (0, 0)),
                  pl.BlockSpec((None, D_MODEL, tn), lambda j: (l, 0, j)),
                  pl.BlockSpec((None, 1, tn), lambda j: (l, 0, j))],
        out_specs=pl.BlockSpec((16, tn), lambda j: (0, j)),
        out_shape=jax.ShapeDtypeStruct((16, 6 * D_MODEL), F32),
        compiler_params=_cp(("arbitrary",)),
        name="ada",
    )(cc, w_ada, b_ada3)


WIN_TM = 1024
WIN_TN = 1024
KR0 = D_SSM + Q_LORA + KV_LORA
IN_W = KR0 + ROPE_DIM + 2 * D_CONV + 3 * D_MODEL
PROJ_SHIFT = PROJ_W - IN_W


def _win_kernel(xa_ref, xb_ref, g_ref, mod_ref, w_ref, o_ref, wb_ref, carry_ref, h_ref, *, n_a):
    i = pl.program_id(0)
    j = pl.program_id(1)

    @pl.when((i == 0) & (j == 0))
    def _():
        w = w_ref[...]
        kr = w[:, KR0:KR0 + ROPE_DIM]
        wb_ref[0] = jnp.concatenate(
            [w[:, :KR0 + ROPE_DIM], _rot_cols(kr), jnp.zeros((D_MODEL, PROJ_SHIFT - ROPE_DIM), F32)],
            axis=1).astype(BF16)

    @pl.when((i == 0) & (j > 0))
    def _():
        wb_ref[j] = jnp.concatenate([carry_ref[:, :PROJ_SHIFT], w_ref[:, :WIN_TN - PROJ_SHIFT]],
                                    axis=1).astype(BF16)

    @pl.when((i == 0) & (j < PROJ_W // WIN_TN - 1))
    def _():
        carry_ref[:, :PROJ_SHIFT] = w_ref[:, WIN_TN - PROJ_SHIFT:]

    @pl.when(j == 0)
    def _():
        x = jnp.where(i < n_a, xa_ref[...], xb_ref[...])
        y = x * lax.rsqrt(jnp.mean(x * x, axis=-1, keepdims=True) + EPS) * g_ref[...]
        h_ref[...] = (y * (1.0 + mod_ref[1:2, :]) + mod_ref[0:1, :]).astype(BF16)

    o_ref[...] = jnp.dot(h_ref[...], wb_ref[j], preferred_element_type=F32)


def _win(xa, xb, g3, modt, w_in, l):
    n_a = xa.shape[0] // WIN_TM if xb is not xa else T // WIN_TM
    ni = T // WIN_TM
    nj = PROJ_W // WIN_TN
    nbp = T_P // WIN_TM
    return pl.pallas_call(
        functools.partial(_win_kernel, n_a=n_a),
        grid=(ni, nj),
        in_specs=[pl.BlockSpec((WIN_TM, D_MODEL), lambda i, j: (jnp.minimum(i, n_a - 1), 0)),
                  pl.BlockSpec((WIN_TM, D_MODEL), lambda i, j: (jnp.maximum(i - n_a, 0), 0)),
                  pl.BlockSpec((None, 1, D_MODEL), lambda i, j: (l, 0, 0)),
                  pl.BlockSpec((None, 6, D_MODEL),
                               lambda i, j: (jnp.where(i < nbp, DEC_BATCH, i - nbp), 0, 0)),
                  pl.BlockSpec((None, D_MODEL, WIN_TN), lambda i, j: (l, 0, jnp.where(i == 0, j, nj - 1)))],
        out_specs=pl.BlockSpec((WIN_TM, WIN_TN), lambda i, j: (i, j)),
        out_shape=jax.ShapeDtypeStruct((T, PROJ_W), F32),
        scratch_shapes=[pltpu.VMEM((nj, D_MODEL, WIN_TN), BF16), pltpu.VMEM((D_MODEL, 128), F32),
                        pltpu.VMEM((WIN_TM, D_MODEL), BF16)],
        compiler_params=_cp(("arbitrary", "arbitrary"), VMEM_LIMIT),
        name="w_in",
    )(xa, xb, g3, modt, w_in)


def _s5_disc_kernel(lr_ref, li_ref, ldt_ref, ar_ref, ai_ref, fr_ref, fi_ref):
    dt = jnp.exp(ldt_ref[...])
    lr = lr_ref[...]
    li = li_ref[...]
    mag = jnp.exp(lr * dt)
    a_re = mag * jnp.cos(li * dt)
    a_im = mag * jnp.sin(li * dt)
    den = lr * lr + li * li
    ar_ref[...] = a_re
    ai_ref[...] = a_im
    fr_ref[...] = ((a_re - 1.0) * lr + a_im * li) / den
    fi_ref[...] = (a_im * lr - (a_re - 1.0) * li) / den


def _s5_disc(lam_re, lam_im, log_dt):
    rows = DEPTH * 2 * N_SSM_GROUPS
    shp = jax.ShapeDtypeStruct((rows, N_STATE), F32)
    return pl.pallas_call(
        _s5_disc_kernel,
        out_shape=(shp, shp, shp, shp),
        name="s5_disc",
    )(lam_re.reshape(rows, N_STATE), lam_im.reshape(rows, N_STATE), log_dt.reshape(rows, 1))


S5_CHUNK = 256
S5_SUB = 64
S5_NSUB = S5_CHUNK // S5_SUB
S5_NCHUNK = 8


def _s5_kernel(up_ref, usf_ref, usr_ref, wb_ref, wc_ref, are_ref, aim_ref, h0r_ref, h0i_ref,
               yf_ref, yr_ref, fin_ref,
               hre_ref, him_ref, ab_re, ab_im, xre_ref, xim_ref):
    k = pl.program_id(0)

    @pl.when(k < 4)
    def _():
        hre_ref[...] = jnp.zeros_like(hre_ref)
        him_ref[...] = jnp.zeros_like(him_ref)

    @pl.when(k == 4)
    def _():
        hre_ref[...] = h0r_ref[...]
        him_ref[...] = h0i_ref[...]

    for d, (u_ref, y_ref) in enumerate(((usf_ref, yf_ref), (usr_ref, yr_ref))):
        reverse = d == 1
        ab_re[...] = jnp.broadcast_to(are_ref[d], (8, SSM_STATE))
        ab_im[...] = jnp.broadcast_to(aim_ref[d], (8, SSM_STATE))

        def sub(s, carry, d=d, reverse=reverse, u_ref=u_ref, y_ref=y_ref):
            ss = (S5_NSUB - 1 - s) if reverse else s
            rows = pl.ds(pl.multiple_of(ss * S5_SUB, S5_SUB), S5_SUB)
            u2 = jnp.where(k < 4, up_ref[rows], u_ref[rows]).reshape(S5_SUB * 8, D_SSM).astype(BF16)
            for half in range(2):
                uh = u2[:, half * 256:(half + 1) * 256]
                cols = slice(half * 1024, (half + 1) * 1024)
                xre_ref[:, cols] = jnp.dot(uh, wb_ref[d, 0, half], preferred_element_type=F32)
                xim_ref[:, cols] = jnp.dot(uh, wb_ref[d, 1, half], preferred_element_type=F32)

            def step(t, h, reverse=reverse):
                tt = (S5_SUB - 1 - t) if reverse else t
                r = pl.ds(pl.multiple_of(tt * 8, 8), 8)
                h_re, h_im = h
                a_re = ab_re[...]
                a_im = ab_im[...]
                n_re = a_re * h_re - a_im * h_im + xre_ref[r, :]
                n_im = a_re * h_im + a_im * h_re + xim_ref[r, :]
                xre_ref[r, :] = n_re
                xim_ref[r, :] = n_im
                return n_re, n_im

            h_re, h_im = lax.fori_loop(0, S5_SUB, step, (hre_ref[d], him_ref[d]))
            hre_ref[d] = h_re
            him_ref[d] = h_im

            @pl.when((s == 0) & (k < 4))
            def _(d=d, reverse=reverse):
                r0 = (S5_SUB - 1) * 8 if reverse else 0
                fin_ref[d, 0] = xre_ref[r0:r0 + 8, :]
                fin_ref[d, 1] = xim_ref[r0:r0 + 8, :]

            x_re = xre_ref[...].astype(BF16)
            x_im = xim_ref[...].astype(BF16)
            for q in range(4):
                ks = slice(q * 512, (q + 1) * 512)
                yq = (jnp.dot(x_re[:, ks], wc_ref[d, 0, q], preferred_element_type=F32)
                      + jnp.dot(x_im[:, ks], wc_ref[d, 1, q], preferred_element_type=F32))
                y_ref[rows, :, q * 128:(q + 1) * 128] = yq.reshape(S5_SUB, 8, 128)
            return carry

        lax.fori_loop(0, S5_NSUB, sub, 0)


def _s5(u_p, u_s, wb, wc, a_re, a_im, h0_re, h0_im, l):
    def rev_chunk(k):
        return jnp.where(k < 4, k, 11 - k)

    blk = (None, S5_CHUNK, 8, D_SSM)
    y_shape = jax.ShapeDtypeStruct((S5_NCHUNK, S5_CHUNK, 8, D_SSM), F32)
    return pl.pallas_call(
        _s5_kernel,
        grid=(S5_NCHUNK,),
        in_specs=[pl.BlockSpec(blk, lambda k: (jnp.minimum(k, 3), 0, 0, 0)),
                  pl.BlockSpec(blk, lambda k: (jnp.maximum(k - 4, 0), 0, 0, 0)),
                  pl.BlockSpec(blk, lambda k: (jnp.minimum(7 - k, 3), 0, 0, 0)),
                  pl.BlockSpec((None, 2, 2, 2, 256, 1024), lambda k: (l, 0, 0, 0, 0, 0)),
                  pl.BlockSpec((None, 2, 2, 4, 512, 128), lambda k: (l, 0, 0, 0, 0, 0)),
                  pl.BlockSpec((None, 2, 1, SSM_STATE), lambda k: (l, 0, 0, 0)),
                  pl.BlockSpec((None, 2, 1, SSM_STATE), lambda k: (l, 0, 0, 0)),
                  pl.BlockSpec((2, 8, SSM_STATE), lambda k: (0, 0, 0)),
                  pl.BlockSpec((2, 8, SSM_STATE), lambda k: (0, 0, 0))],
        out_specs=[pl.BlockSpec(blk, lambda k: (k, 0, 0, 0)),
                   pl.BlockSpec(blk, lambda k: (rev_chunk(k), 0, 0, 0)),
                   pl.BlockSpec((None, 2, 2, 8, SSM_STATE), lambda k: (jnp.minimum(k, 3), 0, 0, 0, 0))],
        out_shape=(y_shape, y_shape,
                   jax.ShapeDtypeStruct((4, 2, 2, 8, SSM_STATE), F32)),
        scratch_shapes=[pltpu.VMEM((2, 8, SSM_STATE), F32), pltpu.VMEM((2, 8, SSM_STATE), F32),
                        pltpu.VMEM((8, SSM_STATE), F32), pltpu.VMEM((8, SSM_STATE), F32),
                        pltpu.VMEM((S5_SUB * 8, SSM_STATE), F32), pltpu.VMEM((S5_SUB * 8, SSM_STATE), F32)],
        compiler_params=_cp(("arbitrary",), VMEM_LIMIT),
        name="s5_scan",
    )(u_p, u_s, u_s, wb, wc, a_re, a_im, h0_re, h0_im)


GLU_TM = 1024


def _ssm_glu_kernel(u_ref, ysp_ref, yss_ref, d_ref, w_ref, b_ref, o_ref, w_b):
    i = pl.program_id(0)

    @pl.when(i == 0)
    def _():
        w_b[...] = w_ref[...].astype(BF16)

    ys = jnp.where(i < T_P // GLU_TM, ysp_ref[...], yss_ref[...])
    y = jax.nn.gelu(d_ref[...] * u_ref[...] + ys).astype(BF16)
    glu = jnp.dot(y, w_b[...], preferred_element_type=F32) + b_ref[...]
    o_ref[...] = (glu[:, :D_MODEL] * _sigmoid(glu[:, D_MODEL:])).astype(o_ref.dtype)


def _ssm_glu(proj, ys_p, ys_s, ssm_d3, w_glu, b_glu3, l):
    nbp = T_P // GLU_TM
    return pl.pallas_call(
        _ssm_glu_kernel,
        grid=(T // GLU_TM,),
        in_specs=[pl.BlockSpec((GLU_TM, D_SSM), lambda i: (i, 0)),
                  pl.BlockSpec((GLU_TM, D_SSM), lambda i: (jnp.minimum(i, nbp - 1), 0)),
                  pl.BlockSpec((GLU_TM, D_SSM), lambda i: (jnp.maximum(i - nbp, 0), 0)),
                  pl.BlockSpec((None, 1, D_SSM), lambda i: (l, 0, 0)),
                  pl.BlockSpec((None, D_SSM, 2 * D_MODEL), lambda i: (l, 0, 0)),
                  pl.BlockSpec((None, 1, 2 * D_MODEL), lambda i: (l, 0, 0))],
        out_specs=pl.BlockSpec((GLU_TM, D_MODEL), lambda i: (i, 0)),
        out_shape=jax.ShapeDtypeStruct((T, D_MODEL), BRANCH_DTYPE),
        scratch_shapes=[pltpu.VMEM((D_SSM, 2 * D_MODEL), BF16)],
        compiler_params=_cp(("arbitrary",), VMEM_LIMIT),
        name="ssm_glu",
    )(proj, ys_p, ys_s, ssm_d3, w_glu, b_glu3)


MLA_TM = DEC_SEQ


def _mla_prep_kernel(cq_ref, ckv_ref, kr_ref, qg_ref, kvg_ref, wqa_ref, wqb_ref,
                     ca_ref, sb_ref, ck_ref, sk_ref,
                     q_ref, ckvn_ref, krr_ref, wqa_b, wqb_b):
    @pl.when(pl.program_id(0) == 0)
    def _():
        wqa_b[...] = wqa_ref[...].astype(BF16)
        wqb_b[...] = wqb_ref[...].astype(BF16)

    cq = cq_ref[...]
    cqn = (cq * lax.rsqrt(jnp.mean(cq * cq, axis=-1, keepdims=True) + EPS) * qg_ref[...]).astype(BF16)
    qa = jnp.dot(cqn, wqa_b[...], preferred_element_type=F32)
    qb = jnp.dot(cqn, wqb_b[...], preferred_element_type=F32)
    ca = ca_ref[...]
    sb = sb_ref[...]
    for h in range(N_HEADS):
        cols = slice(h * 128, (h + 1) * 128)
        q_ref[:, cols] = (qa[:, cols] * ca + qb[:, cols] * sb).astype(q_ref.dtype)

    ckv = ckv_ref[...]
    ckvn_ref[...] = ckv * lax.rsqrt(jnp.mean(ckv * ckv, axis=-1, keepdims=True) + EPS) * kvg_ref[...]

    kr = kr_ref[...]
    krr_ref[...] = kr * ck_ref[...] + pltpu.roll(kr, 128 - ROPE_DIM, 1) * sk_ref[...]


def _mla_prep(proj, q_norm_g3, kv_norm_g3, wqa, wqb, tabs, l):
    ca, sb, ck, sk = tabs
    tab_spec = pl.BlockSpec((MLA_TM, 128), lambda i: (jnp.where(i < T_P // MLA_TM, 1, 0), 0))
    return pl.pallas_call(
        _mla_prep_kernel,
        grid=(T // MLA_TM,),
        in_specs=[pl.BlockSpec((MLA_TM, Q_LORA), lambda i: (i, 2)),
                  pl.BlockSpec((MLA_TM, 128), lambda i: (i, 6)),
                  pl.BlockSpec((MLA_TM, 128), lambda i: (i, 7)),
                  pl.BlockSpec((None, 1, Q_LORA), lambda i: (l, 0, 0)),
                  pl.BlockSpec((None, 1, KV_LORA), lambda i: (l, 0, 0)),
                  pl.BlockSpec((None, Q_LORA, 1024), lambda i: (l, 0, 0)),
                  pl.BlockSpec((None, Q_LORA, 1024), lambda i: (l, 0, 0)),
                  tab_spec, tab_spec, tab_spec, tab_spec],
        out_specs=[pl.BlockSpec((MLA_TM, 1024), lambda i: (i, 0)),
                   pl.BlockSpec((MLA_TM, 128), lambda i: (i, 0)),
                   pl.BlockSpec((MLA_TM, 128), lambda i: (i, 0))],
        out_shape=(jax.ShapeDtypeStruct((T, 1024), BF16),
                   jax.ShapeDtypeStruct((T, 128), F32),
                   jax.ShapeDtypeStruct((T, 128), F32)),
        scratch_shapes=[pltpu.VMEM((Q_LORA, 1024), BF16), pltpu.VMEM((Q_LORA, 1024), BF16)],
        compiler_params=_cp(("arbitrary",)),
        name="mla_prep",
    )(proj, proj, proj, q_norm_g3, kv_norm_g3, wqa, wqb, ca, sb, ck, sk)


ATT_TQ = 1024


def _attn_kernel(q_ref, kvin_ref, wk_ref, wv_ref, wo_ref, o_ref,
                 k_scr, v_scr, o_scr, wk_b, wv_b, wo_b):
    s_id = pl.program_id(0)
    qb = pl.program_id(1)

    @pl.when((s_id == 0) & (qb == 0))
    def _():
        wk_b[...] = wk_ref[...].astype(BF16)
        wv_b[...] = wv_ref[...].astype(BF16)
        wo_b[...] = wo_ref[...].astype(BF16)

    @pl.when(qb == 0)
    def _():
        kv = kvin_ref[...].astype(BF16)
        k_scr[...] = jnp.dot(kv, wk_b[...], preferred_element_type=F32).astype(BF16)
        v_scr[...] = jnp.dot(kv[:, :KV_LORA], wv_b[...], preferred_element_type=F32).astype(BF16)

    scale = (NOPE_DIM + ROPE_DIM) ** -0.5
    for h in range(N_HEADS):
        qh = q_ref[:, h * 128:(h + 1) * 128]
        kh = k_scr[:, h * 128:(h + 1) * 128]
        s = lax.dot_general(qh, kh, (((1,), (1,)), ((), ())), preferred_element_type=F32) * scale
        m = jnp.max(s, axis=-1, keepdims=True)
        p = jnp.exp(s - m)
        den = jnp.sum(p, axis=-1, keepdims=True)
        oh = jnp.dot(p.astype(BF16), v_scr[:, h * V_DIM:(h + 1) * V_DIM], preferred_element_type=F32)
        o_scr[:, h * V_DIM:(h + 1) * V_DIM] = oh / den
    o_ref[...] = jnp.dot(o_scr[...].astype(BF16), wo_b[...], preferred_element_type=F32).astype(o_ref.dtype)


def _attn(q, kvin, wk2, wv, w_o, l, nseq, nqb, tq, row_blk0):
    lk = kvin.shape[1]
    return pl.pallas_call(
        _attn_kernel,
        grid=(nseq, nqb),
        in_specs=[pl.BlockSpec((tq, 1024), lambda s, b: (row_blk0 + s * nqb + b, 0)),
                  pl.BlockSpec((None, lk, 256), lambda s, b: (s, 0, 0)),
                  pl.BlockSpec((None, 256, 1024), lambda s, b: (l, 0, 0)),
                  pl.BlockSpec((None, KV_LORA, 512), lambda s, b: (l, 0, 0)),
                  pl.BlockSpec((None, 512, D_MODEL), lambda s, b: (l, 0, 0))],
        out_specs=pl.BlockSpec((tq, D_MODEL), lambda s, b: (s * nqb + b, 0)),
        out_shape=jax.ShapeDtypeStruct((nseq * nqb * tq, D_MODEL), BRANCH_DTYPE),
        scratch_shapes=[pltpu.VMEM((lk, 1024), BF16), pltpu.VMEM((lk, 512), BF16),
                        pltpu.VMEM((tq, 512), F32),
                        pltpu.VMEM((256, 1024), BF16), pltpu.VMEM((KV_LORA, 512), BF16),
                        pltpu.VMEM((512, D_MODEL), BF16)],
        compiler_params=_cp(("arbitrary", "arbitrary"), VMEM_LIMIT),
        name="attn",
    )(q, kvin, wk2, wv, w_o)


CONV_PAD = 16
CONV_CB = 128
CONV_RB = 128


def _conv_kernel(cin_ref, w_ref, b_ref, lg_ref, lb_ref, pw_ref, pb_ref, o_ref,
                 pad_scr, conv_scr, act_scr, pw_b, *, seq):
    @pl.when(pl.program_id(0) == 0)
    def _():
        pw_b[...] = pw_ref[...].astype(BF16)
        pad_scr[0:CONV_PAD, :] = jnp.zeros((CONV_PAD, D_CONV), F32)
        pad_scr[CONV_PAD + seq:, :] = jnp.zeros((CONV_PAD, D_CONV), F32)

    pad_scr[CONV_PAD:CONV_PAD + seq, :] = cin_ref[:, :D_CONV] * _sigmoid(cin_ref[:, D_CONV:])

    win_rows = CONV_CB + 2 * CONV_PAD
    for c in range(D_CONV // 128):
        lanes = slice(c * 128, (c + 1) * 128)

        def cblk(rb, carry, lanes=lanes):
            base = pl.multiple_of(rb * CONV_CB, CONV_CB)
            win = pad_scr[pl.ds(base, win_rows), lanes]
            acc = jnp.broadcast_to(b_ref[:, lanes], (CONV_CB, 128))
            for b in range(8):
                wb = win if b == 0 else pltpu.roll(win, win_rows - b, 0)
                for a in range(2 * CONV_PAD // 8):
                    k = 8 * a + b - (CONV_PAD - CONV_K // 2)
                    if 0 <= k < CONV_K:
                        acc = acc + w_ref[k:k + 1, lanes] * wb[8 * a:8 * a + CONV_CB]
            conv_scr[pl.ds(base, CONV_CB), lanes] = acc
            return carry

        lax.fori_loop(0, seq // CONV_CB, cblk, 0)

    def blk(rb, carry):
        base = pl.multiple_of(rb * CONV_RB, CONV_RB)
        acc = conv_scr[pl.ds(base, CONV_RB), :]
        mu = jnp.mean(acc, axis=-1, keepdims=True)
        xc = acc - mu
        var = jnp.mean(xc * xc, axis=-1, keepdims=True)
        y = xc * lax.rsqrt(var + EPS) * lg_ref[...] + lb_ref[...]
        act_scr[pl.ds(base, CONV_RB), :] = (y * _sigmoid(y)).astype(BF16)
        return carry

    lax.fori_loop(0, seq // CONV_RB, blk, 0)
    o_ref[...] = (jnp.dot(act_scr[...], pw_b[...], preferred_element_type=F32) + pb_ref[...]).astype(o_ref.dtype)


def _conv(proj, dw_w, dw_b3, ln_g3, ln_b3, pw, pb3, l, nseq, seq, row_blk0):
    vec = lambda n: pl.BlockSpec((None, 1, n), lambda s: (l, 0, 0))
    return pl.pallas_call(
        functools.partial(_conv_kernel, seq=seq),
        grid=(nseq,),
        in_specs=[pl.BlockSpec((seq, 2 * D_CONV), lambda s: (row_blk0 + s, 1)),
                  pl.BlockSpec((None, CONV_K, D_CONV), lambda s: (l, 0, 0)),
                  vec(D_CONV), vec(D_CONV), vec(D_CONV),
                  pl.BlockSpec((None, D_CONV, D_MODEL), lambda s: (l, 0, 0)),
                  vec(D_MODEL)],
        out_specs=pl.BlockSpec((seq, D_MODEL), lambda s: (s, 0)),
        out_shape=jax.ShapeDtypeStruct((nseq * seq, D_MODEL), BRANCH_DTYPE),
        scratch_shapes=[pltpu.VMEM((seq + 2 * CONV_PAD, D_CONV), F32),
                        pltpu.VMEM((seq, D_CONV), F32),
                        pltpu.VMEM((seq, D_CONV), BF16),
                        pltpu.VMEM((D_CONV, D_MODEL), BF16)],
        compiler_params=_cp(("arbitrary",), VMEM_LIMIT),
        name="conv",
    )(proj, dw_w, dw_b3, ln_g3, ln_b3, pw, pb3)


def _merge_kernel(xa_ref, xb_ref, os_ref, omp_ref, oms_ref, ocp_ref, ocs_ref, ga_ref, gb_ref, gc_ref,
                  wo_ref, mod_ref, ng_ref, rw_ref, rb_ref,
                  x1_ref, h2_ref, tw_ref, sc_ref, sr_ref, nb_ref, wo_b, rw_hi, rw_lo, *, n_a):
    i = pl.program_id(0)

    @pl.when(i == 0)
    def _():
        wo_b[...] = wo_ref[...].astype(BF16)
        rw = rw_ref[...]
        rw_hi[...] = rw.astype(BF16)
        rw_lo[...] = (rw - rw_hi[...].astype(F32)).astype(BF16)

    is_prompt = i < NTB_P
    o_mla = jnp.where(is_prompt, omp_ref[...], oms_ref[...])
    o_conv = jnp.where(is_prompt, ocp_ref[...], ocs_ref[...])
    merged = (_sigmoid(ga_ref[...]) * os_ref[...]
              + _sigmoid(gb_ref[...]) * o_mla
              + _sigmoid(gc_ref[...]) * o_conv)
    out = jnp.dot(merged.astype(BF16), wo_b[...], preferred_element_type=F32)
    x1 = jnp.where(i < n_a, xa_ref[...], xb_ref[...]) + mod_ref[2:3, :] * out
    x1_ref[...] = x1
    y = x1 * lax.rsqrt(jnp.mean(x1 * x1, axis=-1, keepdims=True) + EPS) * ng_ref[...]
    h2 = y * (1.0 + mod_ref[4:5, :]) + mod_ref[3:4, :]
    h2_hi = h2.astype(BF16)
    h2_ref[...] = h2_hi

    h2_lo = (h2 - h2_hi.astype(F32)).astype(BF16)
    logits = (jnp.dot(h2_hi, rw_hi[...], preferred_element_type=F32)
              + jnp.dot(h2_lo, rw_hi[...], preferred_element_type=F32)
              + jnp.dot(h2_hi, rw_lo[...], preferred_element_type=F32)) + rb_ref[...]
    lane_i = lax.broadcasted_iota(jnp.int32, logits.shape, 1)
    lane = lane_i.astype(F32)
    cur = logits
    vals, idxs = [], []
    for _ in range(TOP_K):
        m = jnp.max(cur, axis=-1, keepdims=True)
        idx = jnp.min(jnp.where(cur == m, lane, 128.0), axis=-1, keepdims=True)
        vals.append(m)
        idxs.append(idx)
        cur = jnp.where(lane == idx, -jnp.inf, cur)
    exps = [jnp.exp(v - vals[0]) for v in vals]
    tot = exps[0] + exps[1] + exps[2] + exps[3]
    tw = jnp.zeros(logits.shape, F32)
    for k in range(TOP_K):
        tw = jnp.where(lane_i == k, exps[k] / tot, tw)
    tw_ref[...] = tw

    ohs = [jnp.where(lane == idxs[k], 1.0, 0.0) for k in range(TOP_K)]
    cnts = [jnp.sum(oh, axis=0, keepdims=True) for oh in ohs]
    n_b = jnp.ceil((cnts[0] + cnts[1] + cnts[2] + cnts[3]) * (1.0 / SEG_ALIGN)) * SEG_ALIGN
    inc = jnp.broadcast_to(n_b, (8, 128))
    lane8 = lax.broadcasted_iota(jnp.int32, (8, 128), 1)
    for s in (1, 2, 4, 8, 16, 32, 64):
        inc = inc + jnp.where(lane8 >= s, pltpu.roll(inc, s, 1), 0.0)
    before = (inc - n_b)[0:1, :]
    r_i = lax.broadcasted_iota(jnp.int32, (TB, TB), 0)
    c_i = lax.broadcasted_iota(jnp.int32, (TB, TB), 1)
    earlier = jnp.where(r_i > c_i, 1.0, 0.0).astype(BF16)
    slots = jnp.zeros(logits.shape, F32)
    for k in range(TOP_K):
        pre = jnp.dot(earlier, ohs[k].astype(BF16), preferred_element_type=F32)
        slot_k = jnp.sum(ohs[k] * (pre + before), axis=-1, keepdims=True)
        slots = jnp.where(lane_i == k, slot_k, slots)
        before = before + cnts[k]
    sc_ref[...] = slots.astype(jnp.int32)
    sr_ref[...] = slots.T[0:8, :].astype(jnp.int32)
    nb_ref[...] = jnp.broadcast_to(n_b, (8, 128)).astype(jnp.int32)


def _merge(xa, xb, o_ssm, o_mla_p, o_mla_s, o_conv_p, o_conv_s, proj, w_out, modt, norm_ffn_g3, rw_p, rb_p, l):
    n_a = xa.shape[0] // TB if xb is not xa else NTB
    row = lambda i: (i, 0)
    row_p = lambda i: (jnp.minimum(i, NTB_P - 1), 0)
    row_s = lambda i: (jnp.maximum(i - NTB_P, 0), 0)
    blk = lambda m: pl.BlockSpec((TB, D_MODEL), m)
    return pl.pallas_call(
        functools.partial(_merge_kernel, n_a=n_a),
        grid=(NTB,),
        in_specs=[blk(lambda i: (jnp.minimum(i, n_a - 1), 0)), blk(lambda i: (jnp.maximum(i - n_a, 0), 0)),
                  blk(row), blk(row_p), blk(row_s), blk(row_p), blk(row_s),
                  blk(lambda i: (i, 2)), blk(lambda i: (i, 3)), blk(lambda i: (i, 4)),
                  pl.BlockSpec((None, D_MODEL, D_MODEL), lambda i: (l, 0, 0)),
                  pl.BlockSpec((None, 6, D_MODEL), lambda i: (_mod_row(i), 0, 0)),
                  pl.BlockSpec((None, 1, D_MODEL), lambda i: (l, 0, 0)),
                  pl.BlockSpec((None, D_MODEL, 128), lambda i: (l, 0, 0)),
                  pl.BlockSpec((None, 1, 128), lambda i: (l, 0, 0))],
        out_specs=[blk(row), blk(row), pl.BlockSpec((TB, 128), row), pl.BlockSpec((TB, 128), row),
                   pl.BlockSpec((None, 8, TB), lambda i: (i, 0, 0)),
                   pl.BlockSpec((None, 8, 128), lambda i: (i, 0, 0))],
        out_shape=(jax.ShapeDtypeStruct((T, D_MODEL), F32), jax.ShapeDtypeStruct((T, D_MODEL), BF16),
                   jax.ShapeDtypeStruct((T, 128), F32), jax.ShapeDtypeStruct((T, 128), jnp.int32),
                   jax.ShapeDtypeStruct((NTB, 8, TB), jnp.int32),
                   jax.ShapeDtypeStruct((NTB, 8, 128), jnp.int32)),
        scratch_shapes=[pltpu.VMEM((D_MODEL, D_MODEL), BF16),
                        pltpu.VMEM((D_MODEL, 128), BF16), pltpu.VMEM((D_MODEL, 128), BF16)],
        compiler_params=_cp(("arbitrary",), VMEM_LIMIT),
        name="merge",
    )(xa, xb, o_ssm, o_mla_p, o_mla_s, o_conv_p, o_conv_s, proj, proj, proj, w_out, modt, norm_ffn_g3, rw_p, rb_p)


PAIRS = TB * TOP_K
G_ROWS = -(-(PAIRS + (SEG_ALIGN - 1) * N_EXPERTS) // 256) * 256
SEG_BITS = 9


def _aligned(off):
    return off if isinstance(off, int) else pl.multiple_of(off, SEG_ALIGN)


def _segment_copies(src_ref, dst_ref, sem, n, src_off, dst_off):
    copies = []
    done = 0
    for bit in range(SEG_BITS - 1, SEG_ALIGN.bit_length() - 2, -1):
        size = 1 << bit
        part = n & size
        copies.append((part != 0,
                       pltpu.make_async_copy(src_ref.at[pl.ds(_aligned(src_off + done), size)],
                                             dst_ref.at[pl.ds(_aligned(dst_off + done), size)], sem)))
        done = done + part
    return copies


def _start_all(copies):
    for pred, cp in copies:
        pl.when(pred)(cp.start)


def _wait_all(copies):
    for pred, cp in copies:
        pl.when(pred)(cp.wait)


def _dispatch_kernel(seg_ref, nbe_ref, cnt_ref, pst_ref, nu_ref, h_ref, sr_ref, xs_ref, g_scr, z_scr, sem):
    b = pl.program_id(0)
    zsem = sem.at[2]

    def copies(blk, e, off):
        n = nbe_ref[blk * N_EXPERTS + e]
        dst = seg_ref[blk * N_EXPERTS + e]
        slot = blk % 2
        return n, _segment_copies(g_scr.at[slot], xs_ref, sem.at[slot], n, off, dst)

    def drain(blk):
        def wait(e, off):
            n, cps = copies(blk, e, off)
            _wait_all(cps)
            return off + n

        lax.fori_loop(0, N_EXPERTS, wait, 0)

    @pl.when(b < NTB)
    def _():
        sub = lax.broadcasted_iota(jnp.int32, (G_ROWS, TB), 0)
        sr = sr_ref[...]
        hit = sub == sr[0:1, :]
        for k in range(1, TOP_K):
            hit = hit | (sub == sr[k:k + 1, :])
        g_scr[b % 2] = jnp.dot(jnp.where(hit, 1.0, 0.0).astype(BF16), h_ref[...], preferred_element_type=F32)

        def start(e, off):
            n, cps = copies(b, e, off)
            _start_all(cps)
            return off + n

        lax.fori_loop(0, N_EXPERTS, start, 0)

    @pl.when(b > 0)
    def _():
        drain(b - 1)

    @pl.when(b == NTB)
    def _():
        sem = zsem
        z_scr[...] = jnp.zeros_like(z_scr)

        def pad(e):
            cnt = cnt_ref[e]
            padded = (cnt + MOE_BM - 1) // MOE_BM * MOE_BM
            return padded - cnt, pst_ref[e] + cnt

        def start(e, carry):
            n, dst = pad(e)
            _start_all(_segment_copies(z_scr, xs_ref, sem, n, 0, dst))
            return carry

        def wait(e, carry):
            n, dst = pad(e)
            _wait_all(_segment_copies(z_scr, xs_ref, sem, n, 0, dst))
            return carry

        lax.fori_loop(0, N_EXPERTS, start, 0)
        lax.fori_loop(0, N_EXPERTS, wait, 0)

        def tail_copy(blk):
            return pltpu.make_async_copy(z_scr, xs_ref.at[pl.ds(blk * MOE_BM, MOE_BM)], sem)

        def tail_start(blk, carry):
            tail_copy(blk).start()
            return carry

        def tail_wait(blk, carry):
            tail_copy(blk).wait()
            return carry

        lax.fori_loop(nu_ref[0], MOE_NB, tail_start, 0)
        lax.fori_loop(nu_ref[0], MOE_NB, tail_wait, 0)


def _dispatch(seg_start, n_be, counts, pad_start, n_used, h2, slot_row):
    last = lambda b, *_: (jnp.minimum(b, NTB - 1), 0)
    grid_spec = pltpu.PrefetchScalarGridSpec(
        num_scalar_prefetch=5,
        grid=(NTB + 1,),
        in_specs=[pl.BlockSpec((TB, D_MODEL), last),
                  pl.BlockSpec((None, 8, TB), lambda b, *_: (jnp.minimum(b, NTB - 1), 0, 0))],
        out_specs=pl.BlockSpec(memory_space=pl.ANY),
        scratch_shapes=[pltpu.VMEM((2, G_ROWS, D_MODEL), F32),
                        pltpu.VMEM((MOE_BM, D_MODEL), F32),
                        pltpu.SemaphoreType.DMA((3,))],
    )
    return pl.pallas_call(
        _dispatch_kernel,
        grid_spec=grid_spec,
        out_shape=jax.ShapeDtypeStruct((MOE_ROWS, D_MODEL), F32),
        compiler_params=_cp(("arbitrary",), VMEM_LIMIT),
        name="moe_dispatch",
    )(seg_start, n_be, counts, pad_start, n_used, h2, slot_row)


def _expert_kernel(be_ref, slot_ref, nu_ref, x_ref, w1_ref, b1_ref, w2_ref, b2_ref, y_ref, w1_b, w2_b):
    i = pl.program_id(0)
    blk = i - 1
    cur = be_ref[jnp.maximum(blk, 0)]
    nxt = be_ref[jnp.minimum(i, MOE_NB - 1)]

    @pl.when((i == 0) | (nxt != cur))
    def _():
        w1_b[slot_ref[jnp.minimum(i, MOE_NB - 1)]] = w1_ref[...].astype(BF16)

    @pl.when((blk == 0) | ((blk > 0) & (cur != be_ref[jnp.maximum(blk - 1, 0)])))
    def _():
        w2_b[...] = w2_ref[...].astype(BF16)

    @pl.when((blk >= 0) & (blk < nu_ref[0]))
    def _():
        w1 = w1_b[slot_ref[jnp.maximum(blk, 0)]]
        gu = jnp.dot(x_ref[...].astype(BF16), w1, preferred_element_type=F32) + b1_ref[...]
        g = jnp.minimum(gu[:, :D_FF], SWIGLU_LIMIT)
        u = jnp.clip(gu[:, D_FF:], -SWIGLU_LIMIT, SWIGLU_LIMIT)
        act = (u + 1.0) * (g * _sigmoid(SWIGLU_ALPHA * g))
        y_ref[...] = jnp.dot(act.astype(BF16), w2_b[...], preferred_element_type=F32) + b2_ref[...]

    @pl.when(blk >= nu_ref[0])
    def _():
        y_ref[...] = jnp.zeros_like(y_ref)


def _experts(block_e, w1_slot, n_used, xs, w1, b1_4, w2, b2_4, l):
    prev = lambda i: jnp.maximum(i - 1, 0)
    grid_spec = pltpu.PrefetchScalarGridSpec(
        num_scalar_prefetch=3,
        grid=(MOE_NB + 1,),
        in_specs=[pl.BlockSpec((MOE_BM, D_MODEL), lambda i, be, sl, nu: (jnp.minimum(prev(i), nu[0] - 1), 0)),
                  pl.BlockSpec((None, None, D_MODEL, 2 * D_FF),
                               lambda i, be, sl, nu: (l, be[jnp.minimum(i, MOE_NB - 1)], 0, 0)),
                  pl.BlockSpec((None, None, 1, 2 * D_FF), lambda i, be, sl, nu: (l, be[prev(i)], 0, 0)),
                  pl.BlockSpec((None, None, D_FF, D_MODEL), lambda i, be, sl, nu: (l, be[prev(i)], 0, 0)),
                  pl.BlockSpec((None, None, 1, D_MODEL), lambda i, be, sl, nu: (l, be[prev(i)], 0, 0))],
        out_specs=pl.BlockSpec((MOE_BM, D_MODEL), lambda i, be, sl, nu: (prev(i), 0)),
        scratch_shapes=[pltpu.VMEM((2, D_MODEL, 2 * D_FF), BF16), pltpu.VMEM((D_FF, D_MODEL), BF16)],
    )
    return pl.pallas_call(
        _expert_kernel,
        grid_spec=grid_spec,
        out_shape=jax.ShapeDtypeStruct((MOE_ROWS, D_MODEL), F32),
        compiler_params=_cp(("arbitrary",), VMEM_LIMIT),
        name="moe_experts",
    )(block_e, w1_slot, n_used, xs, w1, b1_4, w2, b2_4)


def _combine_kernel(seg_ref, nbe_ref, x1_ref, tw_ref, sc_ref, mod_ref, fg_ref, ys_ref, *rest, final):
    if final:
        yp_ref, ys_out_ref, g_scr, sem = rest
    else:
        x2_ref, g_scr, sem = rest
    b = pl.program_id(0)

    def copies(blk, e, off):
        n = nbe_ref[blk * N_EXPERTS + e]
        src = seg_ref[blk * N_EXPERTS + e]
        slot = blk % 2
        return n, _segment_copies(ys_ref, g_scr.at[slot], sem.at[slot], n, src, off)

    def fetch(blk):
        g_scr[blk % 2, PAIRS:, :] = jnp.zeros((G_ROWS - PAIRS, D_MODEL), F32)

        def start(e, off):
            n, cps = copies(blk, e, off)
            _start_all(cps)
            return off + n

        lax.fori_loop(0, N_EXPERTS, start, 0)

    @pl.when(b == 0)
    def _():
        fetch(b)

    @pl.when(b + 1 < NTB)
    def _():
        fetch(b + 1)

    def wait(e, off):
        n, cps = copies(b, e, off)
        _wait_all(cps)
        return off + n

    lax.fori_loop(0, N_EXPERTS, wait, 0)

    sc = sc_ref[...]
    tw = tw_ref[...]
    lane = lax.broadcasted_iota(jnp.int32, (TB, G_ROWS), 1)
    pw = jnp.zeros((TB, G_ROWS), F32)
    for k in range(TOP_K):
        pw = pw + jnp.where(lane == sc[:, k:k + 1], tw[:, k:k + 1], 0.0)
    moe = jnp.dot(pw.astype(BF16), g_scr[b % 2].astype(BF16), preferred_element_type=F32)
    x2 = x1_ref[...] + mod_ref[5:6, :] * moe
    if final:
        yn = x2 * lax.rsqrt(jnp.mean(x2 * x2, axis=-1, keepdims=True) + EPS) * fg_ref[...]

        @pl.when(b < NTB_P)
        def _():
            yp_ref[...] = yn

        @pl.when(b >= NTB_P)
        def _():
            ys_out_ref[...] = yn
    else:
        x2_ref[...] = x2


def _combine(seg_start, n_be, x1, topw, slot_col, modt, final_g2, ys, final):
    row = lambda b, *_: (b, 0)
    out_blk = pl.BlockSpec((TB, D_MODEL), row)
    out_sds = jax.ShapeDtypeStruct((T, D_MODEL), F32)
    fin_blks = [pl.BlockSpec((TB, D_MODEL), lambda b, *_: (jnp.minimum(b, NTB_P - 1), 0)),
                pl.BlockSpec((TB, D_MODEL), lambda b, *_: (jnp.maximum(b - NTB_P, 0), 0))]
    fin_sds = (jax.ShapeDtypeStruct((T_P, D_MODEL), F32), jax.ShapeDtypeStruct((T_S, D_MODEL), F32))
    grid_spec = pltpu.PrefetchScalarGridSpec(
        num_scalar_prefetch=2,
        grid=(NTB,),
        in_specs=[pl.BlockSpec((TB, D_MODEL), row),
                  pl.BlockSpec((TB, 128), row),
                  pl.BlockSpec((TB, 128), row),
                  pl.BlockSpec((None, 6, D_MODEL), lambda b, *_: (_mod_row(b), 0, 0)),
                  pl.BlockSpec((1, D_MODEL), lambda b, *_: (0, 0)),
                  pl.BlockSpec(memory_space=pl.ANY)],
        out_specs=fin_blks if final else [out_blk],
        scratch_shapes=[pltpu.VMEM((2, G_ROWS, D_MODEL), F32), pltpu.SemaphoreType.DMA((2,))],
    )
    return pl.pallas_call(
        functools.partial(_combine_kernel, final=final),
        grid_spec=grid_spec,
        out_shape=fin_sds if final else (out_sds,),
        compiler_params=_cp(("arbitrary",), VMEM_LIMIT),
        name="moe_combine",
    )(seg_start, n_be, x1, topw, slot_col, modt, final_g2, ys)


def _rot_cols(w):
    q = ROPE_DIM // 4
    parts = []
    for half in range(2):
        a = w[..., half * 2 * q:half * 2 * q + q]
        b = w[..., half * 2 * q + q:(half + 1) * 2 * q]
        parts += [-b, a]
    return jnp.concatenate(parts, axis=-1)


def _rope_tables():
    rows = DEC_SEQ // GRID_W
    row = jnp.repeat(jnp.arange(rows), GRID_W).astype(F32)
    col = jnp.tile(jnp.arange(GRID_W), rows).astype(F32)
    half = ROPE_DIM // 2
    freqs = ROPE_BASE ** (-jnp.arange(0, half, 2, dtype=F32) / half)
    ang_r = row[:, None] * freqs
    ang_c = col[:, None] * freqs
    ang = jnp.concatenate([ang_r, ang_r, ang_c, ang_c], axis=-1)
    cos = jnp.concatenate([jnp.cos(ang), jnp.ones((MLA_TM, ROPE_DIM), F32)], axis=0)
    sin = jnp.concatenate([jnp.sin(ang), jnp.zeros((MLA_TM, ROPE_DIM), F32)], axis=0)
    n = cos.shape[0]
    z = lambda w: jnp.zeros((n, w), F32)
    ca = jnp.concatenate([jnp.ones((n, NOPE_DIM), F32), cos, z(32)], axis=1)
    sb = jnp.concatenate([z(NOPE_DIM), sin, z(32)], axis=1)
    ck = jnp.concatenate([cos, z(96)], axis=1)
    sk = jnp.concatenate([sin, z(96)], axis=1)
    return ca, sb, ck, sk


def kernel(x_prompt, x_sample, c, cache_kv_latent, cache_k_rope, state_ssm, c_ctx, norm_mix_g, norm_ffn_g, final_norm_g, w_ada, b_ada, w_in, q_norm_g, w_uq, kv_norm_g, w_ukv, w_o_mla, ssm_lam_re, ssm_lam_im, ssm_log_dt, ssm_b_re, ssm_b_im, ssm_c_re, ssm_c_im, ssm_d, w_ssm_glu, b_ssm_glu, conv_dw_w, conv_dw_b, conv_ln_g, conv_ln_b, w_conv_pw2, b_conv_pw2, w_out, router_w, router_b, moe_w1, moe_b1, moe_w2, moe_b2):
    L = DEPTH
    v3 = lambda a: a.reshape(L, 1, a.shape[-1])

    xa, xb = x_prompt.reshape(T_P, D_MODEL), x_sample.reshape(T_S, D_MODEL)
    cc = jnp.concatenate([c, c_ctx[None, :], jnp.zeros((16 - DEC_BATCH - 1, D_MODEL), F32)], axis=0)


    wq = w_uq.reshape(L, Q_LORA, N_HEADS, NOPE_DIM + ROPE_DIM)
    zq = lambda w: jnp.zeros((L, Q_LORA, N_HEADS, w), F32)
    wqa = jnp.concatenate([wq, zq(32)], axis=-1).reshape(L, Q_LORA, 1024)
    wqb = jnp.concatenate([zq(NOPE_DIM), _rot_cols(wq[..., NOPE_DIM:]), zq(32)], axis=-1).reshape(L, Q_LORA, 1024)
    wkv = w_ukv.reshape(L, KV_LORA, N_HEADS, NOPE_DIM + V_DIM)
    wk_lat = jnp.concatenate([wkv[..., :NOPE_DIM], jnp.zeros((L, KV_LORA, N_HEADS, 64), F32)], axis=-1)
    place = jnp.concatenate([jnp.zeros((ROPE_DIM, NOPE_DIM), F32), jnp.eye(ROPE_DIM, dtype=F32),
                             jnp.zeros((ROPE_DIM, 32), F32)], axis=-1)
    wk_rope = jnp.broadcast_to(place[None, :, None, :], (L, ROPE_DIM, N_HEADS, 128))
    wk2 = jnp.concatenate([wk_lat, wk_rope, jnp.zeros((L, 96, N_HEADS, 128), F32)], axis=1).reshape(L, 256, 1024)
    wv = wkv[..., NOPE_DIM:].reshape(L, KV_LORA, N_HEADS * V_DIM)
    tabs = _rope_tables()

    a_re, a_im, f_re, f_im = [v.reshape(L, 2, N_SSM_GROUPS, N_STATE)
                              for v in _s5_disc(ssm_lam_re, ssm_lam_im, ssm_log_dt)]
    bb_re = f_re[..., None] * ssm_b_re - f_im[..., None] * ssm_b_im
    bb_im = f_re[..., None] * ssm_b_im + f_im[..., None] * ssm_b_re
    eye16 = jnp.eye(16, dtype=F32)
    eye8 = jnp.eye(8, dtype=F32)

    def bd_in(bb):
        bb = bb.reshape(L, 2, 2, 16, N_STATE, SSM_GROUP)
        return jnp.einsum('ldhgpc,gk->ldhgckp', bb, eye16).reshape(L, 2, 2, 256, 1024)

    def bd_out(cm):
        cm = cm.reshape(L, 2, 4, 8, SSM_GROUP, N_STATE)
        return jnp.einsum('ldqgcp,gk->ldqgpkc', cm, eye8).reshape(L, 2, 4, 512, 128)

    wb = jnp.stack([bd_in(bb_re), bd_in(bb_im)], axis=2).astype(BF16)
    wc = jnp.stack([bd_out(ssm_c_re), bd_out(-ssm_c_im)], axis=2).astype(BF16)
    a_re4 = a_re.reshape(L, 2, 1, SSM_STATE)
    a_im4 = a_im.reshape(L, 2, 1, SSM_STATE)

    rw_p = jnp.concatenate([router_w, jnp.zeros((L, D_MODEL, 128 - N_EXPERTS), F32)], axis=-1)
    rb_p = jnp.concatenate([router_b, jnp.full((L, 128 - N_EXPERTS), -1e30, F32)], axis=-1).reshape(L, 1, 128)

    kv_out, kr_out, ssm_out = [], [], []
    for l in range(L):
        modt = _ada(cc, w_ada, v3(b_ada), l).reshape(16, 6, D_MODEL)
        proj = _win(xa, xb, v3(norm_mix_g), modt, w_in, l)

        u_p = proj[:T_P, :D_SSM].reshape(4, 8, SEQ, D_SSM).transpose(0, 2, 1, 3)
        u_s = proj[T_P:, :D_SSM].reshape(DEC_BATCH, 4, S5_CHUNK, D_SSM).transpose(1, 2, 0, 3)
        st = state_ssm[:, l]
        h0 = st.transpose(4, 1, 0, 2, 3).reshape(2, 2, DEC_BATCH, SSM_STATE)
        y_f, y_r, fin = _s5(u_p, u_s, wb, wc, a_re4, a_im4, h0[0], h0[1], l)
        ysum = y_f + y_r
        o_ssm = _ssm_glu(proj, ysum[:4].transpose(0, 2, 1, 3).reshape(T_P, D_SSM),
                         ysum[4:].transpose(2, 0, 1, 3).reshape(T_S, D_SSM),
                         v3(ssm_d), w_ssm_glu, v3(b_ssm_glu), l)
        ssm_out.append(fin.reshape(4, 2, 2, 8, N_SSM_GROUPS, N_STATE).transpose(0, 3, 1, 4, 5, 2)
                       .reshape(BATCH, 2, N_SSM_GROUPS, N_STATE, 2))

        q, ckvn, krr = _mla_prep(proj, v3(q_norm_g), v3(kv_norm_g), wqa, wqb, tabs, l)
        kvin_p = jnp.concatenate([ckvn[:T_P], krr[:T_P]], axis=-1).reshape(BATCH, SEQ, 256)
        cache_kr = jnp.concatenate([cache_k_rope[:, l], jnp.zeros((DEC_BATCH, PAST_LEN, 96), F32)], axis=-1)
        kvin_s = jnp.concatenate([
            jnp.concatenate([cache_kv_latent[:, l], cache_kr], axis=-1),
            jnp.concatenate([ckvn[T_P:], krr[T_P:]], axis=-1).reshape(DEC_BATCH, DEC_SEQ, 256)], axis=1)
        o_mla_p = _attn(q, kvin_p, wk2, wv, w_o_mla, l, BATCH, 1, SEQ, 0)
        o_mla_s = _attn(q, kvin_s, wk2, wv, w_o_mla, l, DEC_BATCH, DEC_SEQ // ATT_TQ, ATT_TQ, T_P // ATT_TQ)
        kv_out.append(ckvn[:T_P].reshape(BATCH, SEQ, KV_LORA))
        kr_out.append(proj[:T_P, KR0:KR0 + ROPE_DIM].reshape(BATCH, SEQ, ROPE_DIM))

        conv_args = (conv_dw_w, v3(conv_dw_b), v3(conv_ln_g), v3(conv_ln_b), w_conv_pw2, v3(b_conv_pw2), l)
        o_conv_p = _conv(proj, *conv_args, BATCH, SEQ, 0)
        o_conv_s = _conv(proj, *conv_args, DEC_BATCH, DEC_SEQ, T_P // DEC_SEQ)

        x1, h2, topw, slot_col, slot_row, n_blk = _merge(
            xa, xb, o_ssm, o_mla_p, o_mla_s, o_conv_p, o_conv_s, proj, w_out, modt, v3(norm_ffn_g), rw_p, rb_p, l)

        n_be = n_blk[:, 0, :N_EXPERTS]
        counts = jnp.sum(n_be, axis=0)
        padded = (counts + MOE_BM - 1) // MOE_BM * MOE_BM
        pad_end = jnp.cumsum(padded)
        pad_start = pad_end - padded
        seg_start = (pad_start[None, :] + jnp.cumsum(n_be, axis=0) - n_be).reshape(-1).astype(jnp.int32)
        blk_row = jnp.arange(MOE_NB, dtype=jnp.int32) * MOE_BM
        block_e = jnp.minimum(jnp.sum((pad_end[None, :] <= blk_row[:, None]).astype(jnp.int32), axis=1),
                              N_EXPERTS - 1).astype(jnp.int32)
        n_used = (pad_end[-1:] // MOE_BM).astype(jnp.int32)
        blk_id = jnp.arange(MOE_NB, dtype=jnp.int32)
        block_e = jnp.where(blk_id < n_used[0], block_e, jnp.max(jnp.where(blk_id < n_used[0], block_e, 0)))
        switches = jnp.concatenate([jnp.zeros((1,), jnp.int32), (block_e[1:] != block_e[:-1]).astype(jnp.int32)])
        w1_slot = (jnp.cumsum(switches) % 2).astype(jnp.int32)
        n_be = n_be.reshape(-1)

        xs = _dispatch(seg_start, n_be, counts.astype(jnp.int32), pad_start.astype(jnp.int32), n_used, h2, slot_row)
        ys = _experts(block_e, w1_slot, n_used, xs, moe_w1, moe_b1.reshape(L, N_EXPERTS, 1, 2 * D_FF),
                      moe_w2, moe_b2.reshape(L, N_EXPERTS, 1, D_MODEL), l)
        outs = _combine(seg_start, n_be, x1, topw, slot_col, modt, final_norm_g.reshape(1, D_MODEL), ys,
                        final=(l == L - 1))
        if l < L - 1:
            xa = xb = outs[0]

    y_prompt = outs[0].reshape(BATCH, SEQ, D_MODEL)
    y_sample = outs[1].reshape(DEC_BATCH, DEC_SEQ, D_MODEL)
    return (y_prompt, y_sample, jnp.stack(kv_out, axis=1), jnp.stack(kr_out, axis=1), jnp.stack(ssm_out, axis=1))
```

```python
import functools
import math

import numpy as np
import jax
import jax.numpy as jnp
from jax import lax
from jax.experimental import pallas as pl
from jax.experimental.pallas import tpu as pltpu

F32 = jnp.float32
BF16 = jnp.bfloat16

D_MODEL = 1024
BATCH = 32
SEQ = 256
DEPTH = 2
DEC_BATCH = 8
DEC_SEQ = 1024
PAST_LEN = 256
GRID_W = 64
EPS = 1e-6
D_SSM = 512
SSM_GROUP = 16
N_SSM_GROUPS = 32
N_STATE = 64
N_HEADS = 8
NOPE_DIM = 64
ROPE_DIM = 32
V_DIM = 64
Q_LORA = 256
KV_LORA = 128
ROPE_BASE = 10000.0
D_CONV = 512
CONV_K = 31
N_EXPERTS = 32
TOP_K = 4
D_FF = 1024
SWIGLU_ALPHA = 1.702
SWIGLU_LIMIT = 7.0

T_P = BATCH * SEQ
T_S = DEC_BATCH * DEC_SEQ
T = T_P + T_S
TB = 256
NTB = T // TB
NTB_P = T_P // TB
TB_PER_S = DEC_SEQ // TB
SSM_STATE = N_SSM_GROUPS * N_STATE
PROJ_W = 5120
MOE_BM = 512
SEG_ALIGN = 8
MOE_NB = (T * TOP_K + (SEG_ALIGN - 1) * NTB * N_EXPERTS) // MOE_BM + N_EXPERTS
MOE_ROWS = MOE_NB * MOE_BM
VMEM_LIMIT = 56 * 1024 * 1024
BRANCH_DTYPE = BF16


def _cp(sem, vmem=None):
    return pltpu.CompilerParams(dimension_semantics=sem, vmem_limit_bytes=vmem)


def _sigmoid(x):
    return 0.5 * jnp.tanh(0.5 * x) + 0.5


def _mod_row(i):
    return jnp.where(i < NTB_P, DEC_BATCH, (i - NTB_P) // TB_PER_S)


def _ada_kernel(c_ref, w_ref, b_ref, o_ref):
    c = c_ref[...]
    s = c * _sigmoid(c)
    w = w_ref[...]
    s_hi = s.astype(BF16)
    s_lo = (s - s_hi.astype(F32)).astype(BF16)
    w_hi = w.astype(BF16)
    w_lo = (w - w_hi.astype(F32)).astype(BF16)
    o_ref[...] = (jnp.dot(s_hi, w_hi, preferred_element_type=F32)
                  + jnp.dot(s_lo, w_hi, preferred_element_type=F32)
                  + jnp.dot(s_hi, w_lo, preferred_element_type=F32)) + b_ref[...]


def _ada(cc, w_ada, b_ada3, l):
    tn = 1024
    return pl.pallas_call(
        _ada_kernel,
        grid=(6 * D_MODEL // tn,),
        in_specs=[pl.BlockSpec((16, D_MODEL), lambda j: (0, 0)),
                  pl.BlockSpec((None, D_MODEL, tn), lambda j: (l, 0, j)),
                  pl.BlockSpec((None, 1, tn), lambda j: (l, 0, j))],
        out_specs=pl.BlockSpec((16, tn), lambda j: (0, j)),
        out_shape=jax.ShapeDtypeStruct((16, 6 * D_MODEL), F32),
        compiler_params=_cp(("arbitrary",)),
        name="ada",
    )(cc, w_ada, b_ada3)


WIN_TM = 1024
WIN_TN = 1024
KR0 = D_SSM + Q_LORA + KV_LORA
IN_W = KR0 + ROPE_DIM + 2 * D_CONV + 3 * D_MODEL
PROJ_SHIFT = PROJ_W - IN_W


def _win_kernel(xa_ref, xb_ref, g_ref, mod_ref, w_ref, o_ref, wb_ref, carry_ref, h_ref, *, n_a):
    i = pl.program_id(0)
    j = pl.program_id(1)

    @pl.when((i == 0) & (j == 0))
    def _():
        w = w_ref[...]
        kr = w[:, KR0:KR0 + ROPE_DIM]
        wb_ref[0] = jnp.concatenate(
            [w[:, :KR0 + ROPE_DIM], _rot_cols(kr), jnp.zeros((D_MODEL, PROJ_SHIFT - ROPE_DIM), F32)],
            axis=1).astype(BF16)

    @pl.when((i == 0) & (j > 0))
    def _():
        wb_ref[j] = jnp.concatenate([carry_ref[:, :PROJ_SHIFT], w_ref[:, :WIN_TN - PROJ_SHIFT]],
                                    axis=1).astype(BF16)

    @pl.when((i == 0) & (j < PROJ_W // WIN_TN - 1))
    def _():
        carry_ref[:, :PROJ_SHIFT] = w_ref[:, WIN_TN - PROJ_SHIFT:]

    @pl.when(j == 0)
    def _():
        x = jnp.where(i < n_a, xa_ref[...], xb_ref[...])
        y = x * lax.rsqrt(jnp.mean(x * x, axis=-1, keepdims=True) + EPS) * g_ref[...]
        h_ref[...] = (y * (1.0 + mod_ref[1:2, :]) + mod_ref[0:1, :]).astype(BF16)

    o_ref[...] = jnp.dot(h_ref[...], wb_ref[j], preferred_element_type=F32)


def _win(xa, xb, g3, modt, w_in, l):
    n_a = xa.shape[0] // WIN_TM if xb is not xa else T // WIN_TM
    ni = T // WIN_TM
    nj = PROJ_W // WIN_TN
    nbp = T_P // WIN_TM
    return pl.pallas_call(
        functools.partial(_win_kernel, n_a=n_a),
        grid=(ni, nj),
        in_specs=[pl.BlockSpec((WIN_TM, D_MODEL), lambda i, j: (jnp.minimum(i, n_a - 1), 0)),
                  pl.BlockSpec((WIN_TM, D_MODEL), lambda i, j: (jnp.maximum(i - n_a, 0), 0)),
                  pl.BlockSpec((None, 1, D_MODEL), lambda i, j: (l, 0, 0)),
                  pl.BlockSpec((None, 6, D_MODEL),
                               lambda i, j: (jnp.where(i < nbp, DEC_BATCH, i - nbp), 0, 0)),
                  pl.BlockSpec((None, D_MODEL, WIN_TN), lambda i, j: (l, 0, jnp.where(i == 0, j, nj - 1)))],
        out_specs=pl.BlockSpec((WIN_TM, WIN_TN), lambda i, j: (i, j)),
        out_shape=jax.ShapeDtypeStruct((T, PROJ_W), F32),
        scratch_shapes=[pltpu.VMEM((nj, D_MODEL, WIN_TN), BF16), pltpu.VMEM((D_MODEL, 128), F32),
                        pltpu.VMEM((WIN_TM, D_MODEL), BF16)],
        compiler_params=_cp(("arbitrary", "arbitrary"), VMEM_LIMIT),
        name="w_in",
    )(xa, xb, g3, modt, w_in)


def _s5_disc_kernel(lr_ref, li_ref, ldt_ref, ar_ref, ai_ref, fr_ref, fi_ref):
    dt = jnp.exp(ldt_ref[...])
    lr = lr_ref[...]
    li = li_ref[...]
    mag = jnp.exp(lr * dt)
    a_re = mag * jnp.cos(li * dt)
    a_im = mag * jnp.sin(li * dt)
    den = lr * lr + li * li
    ar_ref[...] = a_re
    ai_ref[...] = a_im
    fr_ref[...] = ((a_re - 1.0) * lr + a_im * li) / den
    fi_ref[...] = (a_im * lr - (a_re - 1.0) * li) / den


def _s5_disc(lam_re, lam_im, log_dt):
    rows = DEPTH * 2 * N_SSM_GROUPS
    shp = jax.ShapeDtypeStruct((rows, N_STATE), F32)
    return pl.pallas_call(
        _s5_disc_kernel,
        out_shape=(shp, shp, shp, shp),
        name="s5_disc",
    )(lam_re.reshape(rows, N_STATE), lam_im.reshape(rows, N_STATE), log_dt.reshape(rows, 1))


S5_CHUNK = 256
S5_SUB = 64
S5_NSUB = S5_CHUNK // S5_SUB
S5_NCHUNK = 8


def _s5_kernel(up_ref, usf_ref, usr_ref, wb_ref, wc_ref, are_ref, aim_ref, h0r_ref, h0i_ref,
               yf_ref, yr_ref, fin_ref,
               hre_ref, him_ref, ab_re, ab_im, xre_ref, xim_ref):
    k = pl.program_id(0)

    @pl.when(k < 4)
    def _():
        hre_ref[...] = jnp.zeros_like(hre_ref)
        him_ref[...] = jnp.zeros_like(him_ref)

    @pl.when(k == 4)
    def _():
        hre_ref[...] = h0r_ref[...]
        him_ref[...] = h0i_ref[...]

    for d, (u_ref, y_ref) in enumerate(((usf_ref, yf_ref), (usr_ref, yr_ref))):
        reverse = d == 1
        ab_re[...] = jnp.broadcast_to(are_ref[d], (8, SSM_STATE))
        ab_im[...] = jnp.broadcast_to(aim_ref[d], (8, SSM_STATE))

        def sub(s, carry, d=d, reverse=reverse, u_ref=u_ref, y_ref=y_ref):
            ss = (S5_NSUB - 1 - s) if reverse else s
            rows = pl.ds(pl.multiple_of(ss * S5_SUB, S5_SUB), S5_SUB)
            u2 = jnp.where(k < 4, up_ref[rows], u_ref[rows]).reshape(S5_SUB * 8, D_SSM).astype(BF16)
            for half in range(2):
                uh = u2[:, half * 256:(half + 1) * 256]
                cols = slice(half * 1024, (half + 1) * 1024)
                xre_ref[:, cols] = jnp.dot(uh, wb_ref[d, 0, half], preferred_element_type=F32)
                xim_ref[:, cols] = jnp.dot(uh, wb_ref[d, 1, half], preferred_element_type=F32)

            def step(t, h, reverse=reverse):
                tt = (S5_SUB - 1 - t) if reverse else t
                r = pl.ds(pl.multiple_of(tt * 8, 8), 8)
                h_re, h_im = h
                a_re = ab_re[...]
                a_im = ab_im[...]
                n_re = a_re * h_re - a_im * h_im + xre_ref[r, :]
                n_im = a_re * h_im + a_im * h_re + xim_ref[r, :]
                xre_ref[r, :] = n_re
                xim_ref[r, :] = n_im
                return n_re, n_im

            h_re, h_im = lax.fori_loop(0, S5_SUB, step, (hre_ref[d], him_ref[d]), unroll=2)
            hre_ref[d] = h_re
            him_ref[d] = h_im

            @pl.when((s == 0) & (k < 4))
            def _(d=d, reverse=reverse):
                r0 = (S5_SUB - 1) * 8 if reverse else 0
                fin_ref[d, 0] = xre_ref[r0:r0 + 8, :]
                fin_ref[d, 1] = xim_ref[r0:r0 + 8, :]

            x_re = xre_ref[...].astype(BF16)
            x_im = xim_ref[...].astype(BF16)
            for q in range(4):
                ks = slice(q * 512, (q + 1) * 512)
                yq = (jnp.dot(x_re[:, ks], wc_ref[d, 0, q], preferred_element_type=F32)
                      + jnp.dot(x_im[:, ks], wc_ref[d, 1, q], preferred_element_type=F32))
                y_ref[rows, :, q * 128:(q + 1) * 128] = yq.reshape(S5_SUB, 8, 128)
            return carry

        lax.fori_loop(0, S5_NSUB, sub, 0)


def _s5(u_p, u_s, wb, wc, a_re, a_im, h0_re, h0_im, l):
    def rev_chunk(k):
        return jnp.where(k < 4, k, 11 - k)

    blk = (None, S5_CHUNK, 8, D_SSM)
    y_shape = jax.ShapeDtypeStruct((S5_NCHUNK, S5_CHUNK, 8, D_SSM), F32)
    return pl.pallas_call(
        _s5_kernel,
        grid=(S5_NCHUNK,),
        in_specs=[pl.BlockSpec(blk, lambda k: (jnp.minimum(k, 3), 0, 0, 0)),
                  pl.BlockSpec(blk, lambda k: (jnp.maximum(k - 4, 0), 0, 0, 0)),
                  pl.BlockSpec(blk, lambda k: (jnp.minimum(7 - k, 3), 0, 0, 0)),
                  pl.BlockSpec((None, 2, 2, 2, 256, 1024), lambda k: (l, 0, 0, 0, 0, 0)),
                  pl.BlockSpec((None, 2, 2, 4, 512, 128), lambda k: (l, 0, 0, 0, 0, 0)),
                  pl.BlockSpec((None, 2, 1, SSM_STATE), lambda k: (l, 0, 0, 0)),
                  pl.BlockSpec((None, 2, 1, SSM_STATE), lambda k: (l, 0, 0, 0)),
                  pl.BlockSpec((2, 8, SSM_STATE), lambda k: (0, 0, 0)),
                  pl.BlockSpec((2, 8, SSM_STATE), lambda k: (0, 0, 0))],
        out_specs=[pl.BlockSpec(blk, lambda k: (k, 0, 0, 0)),
                   pl.BlockSpec(blk, lambda k: (rev_chunk(k), 0, 0, 0)),
                   pl.BlockSpec((None, 2, 2, 8, SSM_STATE), lambda k: (jnp.minimum(k, 3), 0, 0, 0, 0))],
        out_shape=(y_shape, y_shape,
                   jax.ShapeDtypeStruct((4, 2, 2, 8, SSM_STATE), F32)),
        scratch_shapes=[pltpu.VMEM((2, 8, SSM_STATE), F32), pltpu.VMEM((2, 8, SSM_STATE), F32),
                        pltpu.VMEM((8, SSM_STATE), F32), pltpu.VMEM((8, SSM_STATE), F32),
                        pltpu.VMEM((S5_SUB * 8, SSM_STATE), F32), pltpu.VMEM((S5_SUB * 8, SSM_STATE), F32)],
        compiler_params=_cp(("arbitrary",), VMEM_LIMIT),
        name="s5_scan",
    )(u_p, u_s, u_s, wb, wc, a_re, a_im, h0_re, h0_im)


GLU_TM = 1024


def _ssm_glu_kernel(u_ref, ysp_ref, yss_ref, d_ref, w_ref, b_ref, o_ref, w_b):
    i = pl.program_id(0)

    @pl.when(i == 0)
    def _():
        w_b[...] = w_ref[...].astype(BF16)

    ys = jnp.where(i < T_P // GLU_TM, ysp_ref[...], yss_ref[...])
    y = jax.nn.gelu(d_ref[...] * u_ref[...] + ys).astype(BF16)
    glu = jnp.dot(y, w_b[...], preferred_element_type=F32) + b_ref[...]
    o_ref[...] = (glu[:, :D_MODEL] * _sigmoid(glu[:, D_MODEL:])).astype(o_ref.dtype)


def _ssm_glu(proj, ys_p, ys_s, ssm_d3, w_glu, b_glu3, l):
    nbp = T_P // GLU_TM
    return pl.pallas_call(
        _ssm_glu_kernel,
        grid=(T // GLU_TM,),
        in_specs=[pl.BlockSpec((GLU_TM, D_SSM), lambda i: (i, 0)),
                  pl.BlockSpec((GLU_TM, D_SSM), lambda i: (jnp.minimum(i, nbp - 1), 0)),
                  pl.BlockSpec((GLU_TM, D_SSM), lambda i: (jnp.maximum(i - nbp, 0), 0)),
                  pl.BlockSpec((None, 1, D_SSM), lambda i: (l, 0, 0)),
                  pl.BlockSpec((None, D_SSM, 2 * D_MODEL), lambda i: (l, 0, 0)),
                  pl.BlockSpec((None, 1, 2 * D_MODEL), lambda i: (l, 0, 0))],
        out_specs=pl.BlockSpec((GLU_TM, D_MODEL), lambda i: (i, 0)),
        out_shape=jax.ShapeDtypeStruct((T, D_MODEL), BRANCH_DTYPE),
        scratch_shapes=[pltpu.VMEM((D_SSM, 2 * D_MODEL), BF16)],
        compiler_params=_cp(("arbitrary",), VMEM_LIMIT),
        name="ssm_glu",
    )(proj, ys_p, ys_s, ssm_d3, w_glu, b_glu3)


MLA_TM = DEC_SEQ


def _mla_prep_kernel(cq_ref, ckv_ref, kr_ref, qg_ref, kvg_ref, wqa_ref, wqb_ref,
                     ca_ref, sb_ref, ck_ref, sk_ref,
                     q_ref, ckvn_ref, krr_ref, wqa_b, wqb_b):
    @pl.when(pl.program_id(0) == 0)
    def _():
        wqa_b[...] = wqa_ref[...].astype(BF16)
        wqb_b[...] = wqb_ref[...].astype(BF16)

    cq = cq_ref[...]
    cqn = (cq * lax.rsqrt(jnp.mean(cq * cq, axis=-1, keepdims=True) + EPS) * qg_ref[...]).astype(BF16)
    qa = jnp.dot(cqn, wqa_b[...], preferred_element_type=F32)
    qb = jnp.dot(cqn, wqb_b[...], preferred_element_type=F32)
    ca = ca_ref[...]
    sb = sb_ref[...]
    for h in range(N_HEADS):
        cols = slice(h * 128, (h + 1) * 128)
        q_ref[:, cols] = (qa[:, cols] * ca + qb[:, cols] * sb).astype(q_ref.dtype)

    ckv = ckv_ref[...]
    ckvn_ref[...] = ckv * lax.rsqrt(jnp.mean(ckv * ckv, axis=-1, keepdims=True) + EPS) * kvg_ref[...]

    kr = kr_ref[...]
    krr_ref[...] = kr * ck_ref[...] + pltpu.roll(kr, 128 - ROPE_DIM, 1) * sk_ref[...]


def _mla_prep(proj, q_norm_g3, kv_norm_g3, wqa, wqb, tabs, l):
    ca, sb, ck, sk = tabs
    tab_spec = pl.BlockSpec((MLA_TM, 128), lambda i: (jnp.where(i < T_P // MLA_TM, 1, 0), 0))
    return pl.pallas_call(
        _mla_prep_kernel,
        grid=(T // MLA_TM,),
        in_specs=[pl.BlockSpec((MLA_TM, Q_LORA), lambda i: (i, 2)),
                  pl.BlockSpec((MLA_TM, 128), lambda i: (i, 6)),
                  pl.BlockSpec((MLA_TM, 128), lambda i: (i, 7)),
                  pl.BlockSpec((None, 1, Q_LORA), lambda i: (l, 0, 0)),
                  pl.BlockSpec((None, 1, KV_LORA), lambda i: (l, 0, 0)),
                  pl.BlockSpec((None, Q_LORA, 1024), lambda i: (l, 0, 0)),
                  pl.BlockSpec((None, Q_LORA, 1024), lambda i: (l, 0, 0)),
                  tab_spec, tab_spec, tab_spec, tab_spec],
        out_specs=[pl.BlockSpec((MLA_TM, 1024), lambda i: (i, 0)),
                   pl.BlockSpec((MLA_TM, 128), lambda i: (i, 0)),
                   pl.BlockSpec((MLA_TM, 128), lambda i: (i, 0))],
        out_shape=(jax.ShapeDtypeStruct((T, 1024), BF16),
                   jax.ShapeDtypeStruct((T, 128), F32),
                   jax.ShapeDtypeStruct((T, 128), F32)),
        scratch_shapes=[pltpu.VMEM((Q_LORA, 1024), BF16), pltpu.VMEM((Q_LORA, 1024), BF16)],
        compiler_params=_cp(("arbitrary",)),
        name="mla_prep",
    )(proj, proj, proj, q_norm_g3, kv_norm_g3, wqa, wqb, ca, sb, ck, sk)


ATT_TQ = 1024


def _attn_kernel(q_ref, kvin_ref, wk_ref, wv_ref, wo_ref, o_ref,
                 k_scr, v_scr, o_scr, wk_b, wv_b, wo_b):
    s_id = pl.program_id(0)
    qb = pl.program_id(1)

    @pl.when((s_id == 0) & (qb == 0))
    def _():
        wk_b[...] = wk_ref[...].astype(BF16)
        wv_b[...] = wv_ref[...].astype(BF16)
        wo_b[...] = wo_ref[...].astype(BF16)

    @pl.when(qb == 0)
    def _():
        kv = kvin_ref[...].astype(BF16)
        k_scr[...] = jnp.dot(kv, wk_b[...], preferred_element_type=F32).astype(BF16)
        v_scr[...] = jnp.dot(kv[:, :KV_LORA], wv_b[...], preferred_element_type=F32).astype(BF16)

    scale = (NOPE_DIM + ROPE_DIM) ** -0.5
    for h in range(N_HEADS):
        qh = q_ref[:, h * 128:(h + 1) * 128]
        kh = k_scr[:, h * 128:(h + 1) * 128]
        s = lax.dot_general(qh, kh, (((1,), (1,)), ((), ())), preferred_element_type=F32) * scale
        m = jnp.max(s, axis=-1, keepdims=True)
        p = jnp.exp(s - m)
        den = jnp.sum(p, axis=-1, keepdims=True)
        oh = jnp.dot(p.astype(BF16), v_scr[:, h * V_DIM:(h + 1) * V_DIM], preferred_element_type=F32)
        o_scr[:, h * V_DIM:(h + 1) * V_DIM] = oh / den
    o_ref[...] = jnp.dot(o_scr[...].astype(BF16), wo_b[...], preferred_element_type=F32).astype(o_ref.dtype)


def _attn(q, kvin, wk2, wv, w_o, l, nseq, nqb, tq, row_blk0):
    lk = kvin.shape[1]
    return pl.pallas_call(
        _attn_kernel,
        grid=(nseq, nqb),
        in_specs=[pl.BlockSpec((tq, 1024), lambda s, b: (row_blk0 + s * nqb + b, 0)),
                  pl.BlockSpec((None, lk, 256), lambda s, b: (s, 0, 0)),
                  pl.BlockSpec((None, 256, 1024), lambda s, b: (l, 0, 0)),
                  pl.BlockSpec((None, KV_LORA, 512), lambda s, b: (l, 0, 0)),
                  pl.BlockSpec((None, 512, D_MODEL), lambda s, b: (l, 0, 0))],
        out_specs=pl.BlockSpec((tq, D_MODEL), lambda s, b: (s * nqb + b, 0)),
        out_shape=jax.ShapeDtypeStruct((nseq * nqb * tq, D_MODEL), BRANCH_DTYPE),
        scratch_shapes=[pltpu.VMEM((lk, 1024), BF16), pltpu.VMEM((lk, 512), BF16),
                        pltpu.VMEM((tq, 512), F32),
                        pltpu.VMEM((256, 1024), BF16), pltpu.VMEM((KV_LORA, 512), BF16),
                        pltpu.VMEM((512, D_MODEL), BF16)],
        compiler_params=_cp(("arbitrary", "arbitrary"), VMEM_LIMIT),
        name="attn",
    )(q, kvin, wk2, wv, w_o)


CONV_PAD = 16
CONV_CB = 128
CONV_RB = 128


def _conv_kernel(cin_ref, w_ref, b_ref, lg_ref, lb_ref, pw_ref, pb_ref, o_ref,
                 pad_scr, conv_scr, act_scr, pw_b, *, seq):
    @pl.when(pl.program_id(0) == 0)
    def _():
        pw_b[...] = pw_ref[...].astype(BF16)
        pad_scr[0:CONV_PAD, :] = jnp.zeros((CONV_PAD, D_CONV), F32)
        pad_scr[CONV_PAD + seq:, :] = jnp.zeros((CONV_PAD, D_CONV), F32)

    pad_scr[CONV_PAD:CONV_PAD + seq, :] = cin_ref[:, :D_CONV] * _sigmoid(cin_ref[:, D_CONV:])

    win_rows = CONV_CB + 2 * CONV_PAD
    for c in range(D_CONV // 128):
        lanes = slice(c * 128, (c + 1) * 128)

        def cblk(rb, carry, lanes=lanes):
            base = pl.multiple_of(rb * CONV_CB, CONV_CB)
            win = pad_scr[pl.ds(base, win_rows), lanes]
            acc = jnp.broadcast_to(b_ref[:, lanes], (CONV_CB, 128))
            for b in range(8):
                wb = win if b == 0 else pltpu.roll(win, win_rows - b, 0)
                for a in range(2 * CONV_PAD // 8):
                    k = 8 * a + b - (CONV_PAD - CONV_K // 2)
                    if 0 <= k < CONV_K:
                        acc = acc + w_ref[k:k + 1, lanes] * wb[8 * a:8 * a + CONV_CB]
            conv_scr[pl.ds(base, CONV_CB), lanes] = acc
            return carry

        lax.fori_loop(0, seq // CONV_CB, cblk, 0)

    def blk(rb, carry):
        base = pl.multiple_of(rb * CONV_RB, CONV_RB)
        acc = conv_scr[pl.ds(base, CONV_RB), :]
        mu = jnp.mean(acc, axis=-1, keepdims=True)
        xc = acc - mu
        var = jnp.mean(xc * xc, axis=-1, keepdims=True)
        y = xc * lax.rsqrt(var + EPS) * lg_ref[...] + lb_ref[...]
        act_scr[pl.ds(base, CONV_RB), :] = (y * _sigmoid(y)).astype(BF16)
        return carry

    lax.fori_loop(0, seq // CONV_RB, blk, 0)
    o_ref[...] = (jnp.dot(act_scr[...], pw_b[...], preferred_element_type=F32) + pb_ref[...]).astype(o_ref.dtype)


def _conv(proj, dw_w, dw_b3, ln_g3, ln_b3, pw, pb3, l, nseq, seq, row_blk0):
    vec = lambda n: pl.BlockSpec((None, 1, n), lambda s: (l, 0, 0))
    return pl.pallas_call(
        functools.partial(_conv_kernel, seq=seq),
        grid=(nseq,),
        in_specs=[pl.BlockSpec((seq, 2 * D_CONV), lambda s: (row_blk0 + s, 1)),
                  pl.BlockSpec((None, CONV_K, D_CONV), lambda s: (l, 0, 0)),
                  vec(D_CONV), vec(D_CONV), vec(D_CONV),
                  pl.BlockSpec((None, D_CONV, D_MODEL), lambda s: (l, 0, 0)),
                  vec(D_MODEL)],
        out_specs=pl.BlockSpec((seq, D_MODEL), lambda s: (s, 0)),
        out_shape=jax.ShapeDtypeStruct((nseq * seq, D_MODEL), BRANCH_DTYPE),
        scratch_shapes=[pltpu.VMEM((seq + 2 * CONV_PAD, D_CONV), F32),
                        pltpu.VMEM((seq, D_CONV), F32),
                        pltpu.VMEM((seq, D_CONV), BF16),
                        pltpu.VMEM((D_CONV, D_MODEL), BF16)],
        compiler_params=_cp(("arbitrary",), VMEM_LIMIT),
        name="conv",
    )(proj, dw_w, dw_b3, ln_g3, ln_b3, pw, pb3)


def _merge_kernel(xa_ref, xb_ref, os_ref, omp_ref, oms_ref, ocp_ref, ocs_ref, ga_ref, gb_ref, gc_ref,
                  wo_ref, mod_ref, ng_ref, rw_ref, rb_ref,
                  x1_ref, h2_ref, tw_ref, sc_ref, sr_ref, nb_ref, wo_b, rw_hi, rw_lo, *, n_a):
    i = pl.program_id(0)

    @pl.when(i == 0)
    def _():
        wo_b[...] = wo_ref[...].astype(BF16)
        rw = rw_ref[...]
        rw_hi[...] = rw.astype(BF16)
        rw_lo[...] = (rw - rw_hi[...].astype(F32)).astype(BF16)

    is_prompt = i < NTB_P
    o_mla = jnp.where(is_prompt, omp_ref[...], oms_ref[...])
    o_conv = jnp.where(is_prompt, ocp_ref[...], ocs_ref[...])
    merged = (_sigmoid(ga_ref[...]) * os_ref[...]
              + _sigmoid(gb_ref[...]) * o_mla
              + _sigmoid(gc_ref[...]) * o_conv)
    out = jnp.dot(merged.astype(BF16), wo_b[...], preferred_element_type=F32)
    x1 = jnp.where(i < n_a, xa_ref[...], xb_ref[...]) + mod_ref[2:3, :] * out
    x1_ref[...] = x1
    y = x1 * lax.rsqrt(jnp.mean(x1 * x1, axis=-1, keepdims=True) + EPS) * ng_ref[...]
    h2 = y * (1.0 + mod_ref[4:5, :]) + mod_ref[3:4, :]
    h2_hi = h2.astype(BF16)
    h2_ref[...] = h2_hi

    h2_lo = (h2 - h2_hi.astype(F32)).astype(BF16)
    logits = (jnp.dot(h2_hi, rw_hi[...], preferred_element_type=F32)
              + jnp.dot(h2_lo, rw_hi[...], preferred_element_type=F32)
              + jnp.dot(h2_hi, rw_lo[...], preferred_element_type=F32)) + rb_ref[...]
    lane_i = lax.broadcasted_iota(jnp.int32, logits.shape, 1)
    lane = lane_i.astype(F32)
    cur = logits
    vals, idxs = [], []
    for _ in range(TOP_K):
        m = jnp.max(cur, axis=-1, keepdims=True)
        idx = jnp.min(jnp.where(cur == m, lane, 128.0), axis=-1, keepdims=True)
        vals.append(m)
        idxs.append(idx)
        cur = jnp.where(lane == idx, -jnp.inf, cur)
    exps = [jnp.exp(v - vals[0]) for v in vals]
    tot = exps[0] + exps[1] + exps[2] + exps[3]
    tw = jnp.zeros(logits.shape, F32)
    for k in range(TOP_K):
        tw = jnp.where(lane_i == k, exps[k] / tot, tw)
    tw_ref[...] = tw

    ohs = [jnp.where(lane == idxs[k], 1.0, 0.0) for k in range(TOP_K)]
    cnts = [jnp.sum(oh, axis=0, keepdims=True) for oh in ohs]
    n_b = jnp.ceil((cnts[0] + cnts[1] + cnts[2] + cnts[3]) * (1.0 / SEG_ALIGN)) * SEG_ALIGN
    inc = jnp.broadcast_to(n_b, (8, 128))
    lane8 = lax.broadcasted_iota(jnp.int32, (8, 128), 1)
    for s in (1, 2, 4, 8, 16, 32, 64):
        inc = inc + jnp.where(lane8 >= s, pltpu.roll(inc, s, 1), 0.0)
    before = (inc - n_b)[0:1, :]
    r_i = lax.broadcasted_iota(jnp.int32, (TB, TB), 0)
    c_i = lax.broadcasted_iota(jnp.int32, (TB, TB), 1)
    earlier = jnp.where(r_i > c_i, 1.0, 0.0).astype(BF16)
    slots = jnp.zeros(logits.shape, F32)
    for k in range(TOP_K):
        pre = jnp.dot(earlier, ohs[k].astype(BF16), preferred_element_type=F32)
        slot_k = jnp.sum(ohs[k] * (pre + before), axis=-1, keepdims=True)
        slots = jnp.where(lane_i == k, slot_k, slots)
        before = before + cnts[k]
    sc_ref[...] = slots.astype(jnp.int32)
    sr_ref[...] = slots.T[0:8, :].astype(jnp.int32)
    nb_ref[...] = jnp.broadcast_to(n_b, (8, 128)).astype(jnp.int32)


def _merge(xa, xb, o_ssm, o_mla_p, o_mla_s, o_conv_p, o_conv_s, proj, w_out, modt, norm_ffn_g3, rw_p, rb_p, l):
    n_a = xa.shape[0] // TB if xb is not xa else NTB
    row = lambda i: (i, 0)
    row_p = lambda i: (jnp.minimum(i, NTB_P - 1), 0)
    row_s = lambda i: (jnp.maximum(i - NTB_P, 0), 0)
    blk = lambda m: pl.BlockSpec((TB, D_MODEL), m)
    return pl.pallas_call(
        functools.partial(_merge_kernel, n_a=n_a),
        grid=(NTB,),
        in_specs=[blk(lambda i: (jnp.minimum(i, n_a - 1), 0)), blk(lambda i: (jnp.maximum(i - n_a, 0), 0)),
                  blk(row), blk(row_p), blk(row_s), blk(row_p), blk(row_s),
                  blk(lambda i: (i, 2)), blk(lambda i: (i, 3)), blk(lambda i: (i, 4)),
                  pl.BlockSpec((None, D_MODEL, D_MODEL), lambda i: (l, 0, 0)),
                  pl.BlockSpec((None, 6, D_MODEL), lambda i: (_mod_row(i), 0, 0)),
                  pl.BlockSpec((None, 1, D_MODEL), lambda i: (l, 0, 0)),
                  pl.BlockSpec((None, D_MODEL, 128), lambda i: (l, 0, 0)),
                  pl.BlockSpec((None, 1, 128), lambda i: (l, 0, 0))],
        out_specs=[blk(row), blk(row), pl.BlockSpec((TB, 128), row), pl.BlockSpec((TB, 128), row),
                   pl.BlockSpec((None, 8, TB), lambda i: (i, 0, 0)),
                   pl.BlockSpec((None, 8, 128), lambda i: (i, 0, 0))],
        out_shape=(jax.ShapeDtypeStruct((T, D_MODEL), F32), jax.ShapeDtypeStruct((T, D_MODEL), BF16),
                   jax.ShapeDtypeStruct((T, 128), F32), jax.ShapeDtypeStruct((T, 128), jnp.int32),
                   jax.ShapeDtypeStruct((NTB, 8, TB), jnp.int32),
                   jax.ShapeDtypeStruct((NTB, 8, 128), jnp.int32)),
        scratch_shapes=[pltpu.VMEM((D_MODEL, D_MODEL), BF16),
                        pltpu.VMEM((D_MODEL, 128), BF16), pltpu.VMEM((D_MODEL, 128), BF16)],
        compiler_params=_cp(("arbitrary",), VMEM_LIMIT),
        name="merge",
    )(xa, xb, o_ssm, o_mla_p, o_mla_s, o_conv_p, o_conv_s, proj, proj, proj, w_out, modt, norm_ffn_g3, rw_p, rb_p)


PAIRS = TB * TOP_K
G_ROWS = -(-(PAIRS + (SEG_ALIGN - 1) * N_EXPERTS) // 256) * 256
SEG_BITS = 9


def _aligned(off):
    return off if isinstance(off, int) else pl.multiple_of(off, SEG_ALIGN)


def _segment_copies(src_ref, dst_ref, sem, n, src_off, dst_off):
    copies = []
    done = 0
    for bit in range(SEG_BITS - 1, SEG_ALIGN.bit_length() - 2, -1):
        size = 1 << bit
        part = n & size
        copies.append((part != 0,
                       pltpu.make_async_copy(src_ref.at[pl.ds(_aligned(src_off + done), size)],
                                             dst_ref.at[pl.ds(_aligned(dst_off + done), size)], sem)))
        done = done + part
    return copies


def _start_all(copies):
    for k, (pred, cp) in enumerate(copies):
        pl.when(pred)(functools.partial(cp.start, k % 2))


def _wait_all(copies):
    for pred, cp in copies:
        pl.when(pred)(cp.wait)


def _dispatch_kernel(seg_ref, nbe_ref, cnt_ref, pst_ref, nu_ref, h_ref, sr_ref, xs_ref, g_scr, z_scr, sem):
    b = pl.program_id(0)
    zsem = sem.at[2]

    def copies(blk, e, off):
        n = nbe_ref[blk * N_EXPERTS + e]
        dst = seg_ref[blk * N_EXPERTS + e]
        slot = blk % 2
        return n, _segment_copies(g_scr.at[slot], xs_ref, sem.at[slot], n, off, dst)

    def drain(blk):
        def wait(e, off):
            n, cps = copies(blk, e, off)
            _wait_all(cps)
            return off + n

        lax.fori_loop(0, N_EXPERTS, wait, 0)

    @pl.when(b < NTB)
    def _():
        sub = lax.broadcasted_iota(jnp.int32, (G_ROWS, TB), 0)
        sr = sr_ref[...]
        hit = sub == sr[0:1, :]
        for k in range(1, TOP_K):
            hit = hit | (sub == sr[k:k + 1, :])
        g_scr[b % 2] = jnp.dot(jnp.where(hit, 1.0, 0.0).astype(BF16), h_ref[...], preferred_element_type=F32)

        def start(e, off):
            n, cps = copies(b, e, off)
            _start_all(cps)
            return off + n

        lax.fori_loop(0, N_EXPERTS, start, 0)

    @pl.when(b > 0)
    def _():
        drain(b - 1)

    @pl.when(b == NTB)
    def _():
        sem = zsem
        z_scr[...] = jnp.zeros_like(z_scr)

        def pad(e):
            cnt = cnt_ref[e]
            padded = (cnt + MOE_BM - 1) // MOE_BM * MOE_BM
            return padded - cnt, pst_ref[e] + cnt

        def start(e, carry):
            n, dst = pad(e)
            _start_all(_segment_copies(z_scr, xs_ref, sem, n, 0, dst))
            return carry

        def wait(e, carry):
            n, dst = pad(e)
            _wait_all(_segment_copies(z_scr, xs_ref, sem, n, 0, dst))
            return carry

        lax.fori_loop(0, N_EXPERTS, start, 0)
        lax.fori_loop(0, N_EXPERTS, wait, 0)

        def tail_copy(blk):
            return pltpu.make_async_copy(z_scr, xs_ref.at[pl.ds(blk * MOE_BM, MOE_BM)], sem)

        def tail_start(blk, carry):
            tail_copy(blk).start()
            return carry

        def tail_wait(blk, carry):
            tail_copy(blk).wait()
            return carry

        lax.fori_loop(nu_ref[0], MOE_NB, tail_start, 0)
        lax.fori_loop(nu_ref[0], MOE_NB, tail_wait, 0)


def _dispatch(seg_start, n_be, counts, pad_start, n_used, h2, slot_row):
    last = lambda b, *_: (jnp.minimum(b, NTB - 1), 0)
    grid_spec = pltpu.PrefetchScalarGridSpec(
        num_scalar_prefetch=5,
        grid=(NTB + 1,),
        in_specs=[pl.BlockSpec((TB, D_MODEL), last),
                  pl.BlockSpec((None, 8, TB), lambda b, *_: (jnp.minimum(b, NTB - 1), 0, 0))],
        out_specs=pl.BlockSpec(memory_space=pl.ANY),
        scratch_shapes=[pltpu.VMEM((2, G_ROWS, D_MODEL), F32),
                        pltpu.VMEM((MOE_BM, D_MODEL), F32),
                        pltpu.SemaphoreType.DMA((3,))],
    )
    return pl.pallas_call(
        _dispatch_kernel,
        grid_spec=grid_spec,
        out_shape=jax.ShapeDtypeStruct((MOE_ROWS, D_MODEL), F32),
        compiler_params=_cp(("arbitrary",), VMEM_LIMIT),
        name="moe_dispatch",
    )(seg_start, n_be, counts, pad_start, n_used, h2, slot_row)


def _expert_kernel(be_ref, slot_ref, nu_ref, x_ref, w1_ref, b1_ref, w2_ref, b2_ref, y_ref, w1_b, w2_b):
    i = pl.program_id(0)
    blk = i - 1
    cur = be_ref[jnp.maximum(blk, 0)]
    nxt = be_ref[jnp.minimum(i, MOE_NB - 1)]

    @pl.when((i == 0) | (nxt != cur))
    def _():
        w1_b[slot_ref[jnp.minimum(i, MOE_NB - 1)]] = w1_ref[...].astype(BF16)

    @pl.when((blk == 0) | ((blk > 0) & (cur != be_ref[jnp.maximum(blk - 1, 0)])))
    def _():
        w2_b[...] = w2_ref[...].astype(BF16)

    @pl.when((blk >= 0) & (blk < nu_ref[0]))
    def _():
        w1 = w1_b[slot_ref[jnp.maximum(blk, 0)]]
        gu = jnp.dot(x_ref[...].astype(BF16), w1, preferred_element_type=F32) + b1_ref[...]
        g = jnp.minimum(gu[:, :D_FF], SWIGLU_LIMIT)
        u = jnp.clip(gu[:, D_FF:], -SWIGLU_LIMIT, SWIGLU_LIMIT)
        act = (u + 1.0) * (g * _sigmoid(SWIGLU_ALPHA * g))
        y_ref[...] = jnp.dot(act.astype(BF16), w2_b[...], preferred_element_type=F32) + b2_ref[...]

    @pl.when(blk >= nu_ref[0])
    def _():
        y_ref[...] = jnp.zeros_like(y_ref)


def _experts(block_e, w1_slot, n_used, xs, w1, b1_4, w2, b2_4, l):
    prev = lambda i: jnp.maximum(i - 1, 0)
    grid_spec = pltpu.PrefetchScalarGridSpec(
        num_scalar_prefetch=3,
        grid=(MOE_NB + 1,),
        in_specs=[pl.BlockSpec((MOE_BM, D_MODEL), lambda i, be, sl, nu: (jnp.minimum(prev(i), nu[0] - 1), 0)),
                  pl.BlockSpec((None, None, D_MODEL, 2 * D_FF),
                               lambda i, be, sl, nu: (l, be[jnp.minimum(i, MOE_NB - 1)], 0, 0)),
                  pl.BlockSpec((None, None, 1, 2 * D_FF), lambda i, be, sl, nu: (l, be[prev(i)], 0, 0)),
                  pl.BlockSpec((None, None, D_FF, D_MODEL), lambda i, be, sl, nu: (l, be[prev(i)], 0, 0)),
                  pl.BlockSpec((None, None, 1, D_MODEL), lambda i, be, sl, nu: (l, be[prev(i)], 0, 0))],
        out_specs=pl.BlockSpec((MOE_BM, D_MODEL), lambda i, be, sl, nu: (prev(i), 0)),
        scratch_shapes=[pltpu.VMEM((2, D_MODEL, 2 * D_FF), BF16), pltpu.VMEM((D_FF, D_MODEL), BF16)],
    )
    return pl.pallas_call(
        _expert_kernel,
        grid_spec=grid_spec,
        out_shape=jax.ShapeDtypeStruct((MOE_ROWS, D_MODEL), F32),
        compiler_params=_cp(("arbitrary",), VMEM_LIMIT),
        name="moe_experts",
    )(block_e, w1_slot, n_used, xs, w1, b1_4, w2, b2_4)


def _combine_kernel(seg_ref, nbe_ref, x1_ref, tw_ref, sc_ref, mod_ref, fg_ref, ys_ref, *rest, final):
    if final:
        yp_ref, ys_out_ref, g_scr, sem = rest
    else:
        x2_ref, g_scr, sem = rest
    b = pl.program_id(0)

    def copies(blk, e, off):
        n = nbe_ref[blk * N_EXPERTS + e]
        src = seg_ref[blk * N_EXPERTS + e]
        slot = blk % 2
        return n, _segment_copies(ys_ref, g_scr.at[slot], sem.at[slot], n, src, off)

    def fetch(blk):
        g_scr[blk % 2, PAIRS:, :] = jnp.zeros((G_ROWS - PAIRS, D_MODEL), F32)

        def start(e, off):
            n, cps = copies(blk, e, off)
            _start_all(cps)
            return off + n

        lax.fori_loop(0, N_EXPERTS, start, 0)

    @pl.when(b == 0)
    def _():
        fetch(b)

    @pl.when(b + 1 < NTB)
    def _():
        fetch(b + 1)

    def wait(e, off):
        n, cps = copies(b, e, off)
        _wait_all(cps)
        return off + n

    lax.fori_loop(0, N_EXPERTS, wait, 0)

    sc = sc_ref[...]
    tw = tw_ref[...]
    lane = lax.broadcasted_iota(jnp.int32, (TB, G_ROWS), 1)
    pw = jnp.zeros((TB, G_ROWS), F32)
    for k in range(TOP_K):
        pw = pw + jnp.where(lane == sc[:, k:k + 1], tw[:, k:k + 1], 0.0)
    moe = jnp.dot(pw.astype(BF16), g_scr[b % 2].astype(BF16), preferred_element_type=F32)
    x2 = x1_ref[...] + mod_ref[5:6, :] * moe
    if final:
        yn = x2 * lax.rsqrt(jnp.mean(x2 * x2, axis=-1, keepdims=True) + EPS) * fg_ref[...]

        @pl.when(b < NTB_P)
        def _():
            yp_ref[...] = yn

        @pl.when(b >= NTB_P)
        def _():
            ys_out_ref[...] = yn
    else:
        x2_ref[...] = x2


def _combine(seg_start, n_be, x1, topw, slot_col, modt, final_g2, ys, final):
    row = lambda b, *_: (b, 0)
    out_blk = pl.BlockSpec((TB, D_MODEL), row)
    out_sds = jax.ShapeDtypeStruct((T, D_MODEL), F32)
    fin_blks = [pl.BlockSpec((TB, D_MODEL), lambda b, *_: (jnp.minimum(b, NTB_P - 1), 0)),
                pl.BlockSpec((TB, D_MODEL), lambda b, *_: (jnp.maximum(b - NTB_P, 0), 0))]
    fin_sds = (jax.ShapeDtypeStruct((T_P, D_MODEL), F32), jax.ShapeDtypeStruct((T_S, D_MODEL), F32))
    grid_spec = pltpu.PrefetchScalarGridSpec(
        num_scalar_prefetch=2,
        grid=(NTB,),
        in_specs=[pl.BlockSpec((TB, D_MODEL), row),
                  pl.BlockSpec((TB, 128), row),
                  pl.BlockSpec((TB, 128), row),
                  pl.BlockSpec((None, 6, D_MODEL), lambda b, *_: (_mod_row(b), 0, 0)),
                  pl.BlockSpec((1, D_MODEL), lambda b, *_: (0, 0)),
                  pl.BlockSpec(memory_space=pl.ANY)],
        out_specs=fin_blks if final else [out_blk],
        scratch_shapes=[pltpu.VMEM((2, G_ROWS, D_MODEL), F32), pltpu.SemaphoreType.DMA((2,))],
    )
    return pl.pallas_call(
        functools.partial(_combine_kernel, final=final),
        grid_spec=grid_spec,
        out_shape=fin_sds if final else (out_sds,),
        compiler_params=_cp(("arbitrary",), VMEM_LIMIT),
        name="moe_combine",
    )(seg_start, n_be, x1, topw, slot_col, modt, final_g2, ys)


def _rot_cols(w):
    q = ROPE_DIM // 4
    parts = []
    for half in range(2):
        a = w[..., half * 2 * q:half * 2 * q + q]
        b = w[..., half * 2 * q + q:(half + 1) * 2 * q]
        parts += [-b, a]
    return jnp.concatenate(parts, axis=-1)


def _rope_tables():
    rows = DEC_SEQ // GRID_W
    row = jnp.repeat(jnp.arange(rows), GRID_W).astype(F32)
    col = jnp.tile(jnp.arange(GRID_W), rows).astype(F32)
    half = ROPE_DIM // 2
    freqs = ROPE_BASE ** (-jnp.arange(0, half, 2, dtype=F32) / half)
    ang_r = row[:, None] * freqs
    ang_c = col[:, None] * freqs
    ang = jnp.concatenate([ang_r, ang_r, ang_c, ang_c], axis=-1)
    cos = jnp.concatenate([jnp.cos(ang), jnp.ones((MLA_TM, ROPE_DIM), F32)], axis=0)
    sin = jnp.concatenate([jnp.sin(ang), jnp.zeros((MLA_TM, ROPE_DIM), F32)], axis=0)
    n = cos.shape[0]
    z = lambda w: jnp.zeros((n, w), F32)
    ca = jnp.concatenate([jnp.ones((n, NOPE_DIM), F32), cos, z(32)], axis=1)
    sb = jnp.concatenate([z(NOPE_DIM), sin, z(32)], axis=1)
    ck = jnp.concatenate([cos, z(96)], axis=1)
    sk = jnp.concatenate([sin, z(96)], axis=1)
    return ca, sb, ck, sk


def kernel(x_prompt, x_sample, c, cache_kv_latent, cache_k_rope, state_ssm, c_ctx, norm_mix_g, norm_ffn_g, final_norm_g, w_ada, b_ada, w_in, q_norm_g, w_uq, kv_norm_g, w_ukv, w_o_mla, ssm_lam_re, ssm_lam_im, ssm_log_dt, ssm_b_re, ssm_b_im, ssm_c_re, ssm_c_im, ssm_d, w_ssm_glu, b_ssm_glu, conv_dw_w, conv_dw_b, conv_ln_g, conv_ln_b, w_conv_pw2, b_conv_pw2, w_out, router_w, router_b, moe_w1, moe_b1, moe_w2, moe_b2):
    L = DEPTH
    v3 = lambda a: a.reshape(L, 1, a.shape[-1])

    xa, xb = x_prompt.reshape(T_P, D_MODEL), x_sample.reshape(T_S, D_MODEL)
    cc = jnp.concatenate([c, c_ctx[None, :], jnp.zeros((16 - DEC_BATCH - 1, D_MODEL), F32)], axis=0)


    wq = w_uq.reshape(L, Q_LORA, N_HEADS, NOPE_DIM + ROPE_DIM)
    zq = lambda w: jnp.zeros((L, Q_LORA, N_HEADS, w), F32)
    wqa = jnp.concatenate([wq, zq(32)], axis=-1).reshape(L, Q_LORA, 1024)
    wqb = jnp.concatenate([zq(NOPE_DIM), _rot_cols(wq[..., NOPE_DIM:]), zq(32)], axis=-1).reshape(L, Q_LORA, 1024)
    wkv = w_ukv.reshape(L, KV_LORA, N_HEADS, NOPE_DIM + V_DIM)
    wk_lat = jnp.concatenate([wkv[..., :NOPE_DIM], jnp.zeros((L, KV_LORA, N_HEADS, 64), F32)], axis=-1)
    place = jnp.concatenate([jnp.zeros((ROPE_DIM, NOPE_DIM), F32), jnp.eye(ROPE_DIM, dtype=F32),
                             jnp.zeros((ROPE_DIM, 32), F32)], axis=-1)
    wk_rope = jnp.broadcast_to(place[None, :, None, :], (L, ROPE_DIM, N_HEADS, 128))
    wk2 = jnp.concatenate([wk_lat, wk_rope, jnp.zeros((L, 96, N_HEADS, 128), F32)], axis=1).reshape(L, 256, 1024)
    wv = wkv[..., NOPE_DIM:].reshape(L, KV_LORA, N_HEADS * V_DIM)
    tabs = _rope_tables()

    a_re, a_im, f_re, f_im = [v.reshape(L, 2, N_SSM_GROUPS, N_STATE)
                              for v in _s5_disc(ssm_lam_re, ssm_lam_im, ssm_log_dt)]
    bb_re = f_re[..., None] * ssm_b_re - f_im[..., None] * ssm_b_im
    bb_im = f_re[..., None] * ssm_b_im + f_im[..., None] * ssm_b_re
    eye16 = jnp.eye(16, dtype=F32)
    eye8 = jnp.eye(8, dtype=F32)

    def bd_in(bb):
        bb = bb.reshape(L, 2, 2, 16, N_STATE, SSM_GROUP)
        return jnp.einsum('ldhgpc,gk->ldhgckp', bb, eye16).reshape(L, 2, 2, 256, 1024)

    def bd_out(cm):
        cm = cm.reshape(L, 2, 4, 8, SSM_GROUP, N_STATE)
        return jnp.einsum('ldqgcp,gk->ldqgpkc', cm, eye8).reshape(L, 2, 4, 512, 128)

    wb = jnp.stack([bd_in(bb_re), bd_in(bb_im)], axis=2).astype(BF16)
    wc = jnp.stack([bd_out(ssm_c_re), bd_out(-ssm_c_im)], axis=2).astype(BF16)
    a_re4 = a_re.reshape(L, 2, 1, SSM_STATE)
    a_im4 = a_im.reshape(L, 2, 1, SSM_STATE)

    rw_p = jnp.concatenate([router_w, jnp.zeros((L, D_MODEL, 128 - N_EXPERTS), F32)], axis=-1)
    rb_p = jnp.concatenate([router_b, jnp.full((L, 128 - N_EXPERTS), -1e30, F32)], axis=-1).reshape(L, 1, 128)

    kv_out, kr_out, ssm_out = [], [], []
    for l in range(L):
        modt = _ada(cc, w_ada, v3(b_ada), l).reshape(16, 6, D_MODEL)
        proj = _win(xa, xb, v3(norm_mix_g), modt, w_in, l)

        u_p = proj[:T_P, :D_SSM].reshape(4, 8, SEQ, D_SSM).transpose(0, 2, 1, 3)
        u_s = proj[T_P:, :D_SSM].reshape(DEC_BATCH, 4, S5_CHUNK, D_SSM).transpose(1, 2, 0, 3)
        st = state_ssm[:, l]
        h0 = st.transpose(4, 1, 0, 2, 3).reshape(2, 2, DEC_BATCH, SSM_STATE)
        y_f, y_r, fin = _s5(u_p, u_s, wb, wc, a_re4, a_im4, h0[0], h0[1], l)
        ysum = y_f + y_r
        o_ssm = _ssm_glu(proj, ysum[:4].transpose(0, 2, 1, 3).reshape(T_P, D_SSM),
                         ysum[4:].transpose(2, 0, 1, 3).reshape(T_S, D_SSM),
                         v3(ssm_d), w_ssm_glu, v3(b_ssm_glu), l)
        ssm_out.append(fin.reshape(4, 2, 2, 8, N_SSM_GROUPS, N_STATE).transpose(0, 3, 1, 4, 5, 2)
                       .reshape(BATCH, 2, N_SSM_GROUPS, N_STATE, 2))

        q, ckvn, krr = _mla_prep(proj, v3(q_norm_g), v3(kv_norm_g), wqa, wqb, tabs, l)
        kvin_p = jnp.concatenate([ckvn[:T_P], krr[:T_P]], axis=-1).reshape(BATCH, SEQ, 256)
        cache_kr = jnp.concatenate([cache_k_rope[:, l], jnp.zeros((DEC_BATCH, PAST_LEN, 96), F32)], axis=-1)
        kvin_s = jnp.concatenate([
            jnp.concatenate([cache_kv_latent[:, l], cache_kr], axis=-1),
            jnp.concatenate([ckvn[T_P:], krr[T_P:]], axis=-1).reshape(DEC_BATCH, DEC_SEQ, 256)], axis=1)
        o_mla_p = _attn(q, kvin_p, wk2, wv, w_o_mla, l, BATCH, 1, SEQ, 0)
        o_mla_s = _attn(q, kvin_s, wk2, wv, w_o_mla, l, DEC_BATCH, DEC_SEQ // ATT_TQ, ATT_TQ, T_P // ATT_TQ)
        kv_out.append(ckvn[:T_P].reshape(BATCH, SEQ, KV_LORA))
        kr_out.append(proj[:T_P, KR0:KR0 + ROPE_DIM].reshape(BATCH, SEQ, ROPE_DIM))

        conv_args = (conv_dw_w, v3(conv_dw_b), v3(conv_ln_g), v3(conv_ln_b), w_conv_pw2, v3(b_conv_pw2), l)
        o_conv_p = _conv(proj, *conv_args, BATCH, SEQ, 0)
        o_conv_s = _conv(proj, *conv_args, DEC_BATCH, DEC_SEQ, T_P // DEC_SEQ)

        x1, h2, topw, slot_col, slot_row, n_blk = _merge(
            xa, xb, o_ssm, o_mla_p, o_mla_s, o_conv_p, o_conv_s, proj, w_out, modt, v3(norm_ffn_g), rw_p, rb_p, l)

        n_be = n_blk[:, 0, :N_EXPERTS]
        counts = jnp.sum(n_be, axis=0)
        padded = (counts + MOE_BM - 1) // MOE_BM * MOE_BM
        pad_end = jnp.cumsum(padded)
        pad_start = pad_end - padded
        seg_start = (pad_start[None, :] + jnp.cumsum(n_be, axis=0) - n_be).reshape(-1).astype(jnp.int32)
        blk_row = jnp.arange(MOE_NB, dtype=jnp.int32) * MOE_BM
        block_e = jnp.minimum(jnp.sum((pad_end[None, :] <= blk_row[:, None]).astype(jnp.int32), axis=1),
                              N_EXPERTS - 1).astype(jnp.int32)
        n_used = (pad_end[-1:] // MOE_BM).astype(jnp.int32)
        blk_id = jnp.arange(MOE_NB, dtype=jnp.int32)
        block_e = jnp.where(blk_id < n_used[0], block_e, jnp.max(jnp.where(blk_id < n_used[0], block_e, 0)))
        switches = jnp.concatenate([jnp.zeros((1,), jnp.int32), (block_e[1:] != block_e[:-1]).astype(jnp.int32)])
        w1_slot = (jnp.cumsum(switches) % 2).astype(jnp.int32)
        n_be = n_be.reshape(-1)

        xs = _dispatch(seg_start, n_be, counts.astype(jnp.int32), pad_start.astype(jnp.int32), n_used, h2, slot_row)
        ys = _experts(block_e, w1_slot, n_used, xs, moe_w1, moe_b1.reshape(L, N_EXPERTS, 1, 2 * D_FF),
                      moe_w2, moe_b2.reshape(L, N_EXPERTS, 1, D_MODEL), l)
        outs = _combine(seg_start, n_be, x1, topw, slot_col, modt, final_norm_g.reshape(1, D_MODEL), ys,
                        final=(l == L - 1))
        if l < L - 1:
            xa = xb = outs[0]

    y_prompt = outs[0].reshape(BATCH, SEQ, D_MODEL)
    y_sample = outs[1].reshape(DEC_BATCH, DEC_SEQ, D_MODEL)
    return (y_prompt, y_sample, jnp.stack(kv_out, axis=1), jnp.stack(kr_out, axis=1), jnp.stack(ssm_out, axis=1))
```
